```python
import math
import jax, jax.numpy as jnp
from jax import lax
import numpy as np

D_MODEL = 1024
BATCH = 8
SEQ = 4096
DEPTH = 4

D_MIX = D_MODEL
M_HEADS = 4
M_DH = D_MODEL // 16
M_W = M_HEADS * M_DH
CONV_W = 4
CHUNK = 64
DF_HEADS = 4
DF_DQK = D_MODEL // 16
DF_DV = 2 * DF_DQK
DF_QK_W = DF_HEADS * 2 * DF_DQK
DF_W = DF_HEADS * DF_DV
H_HEADS = 4
H_DK = D_MODEL // 16
H_DV = D_MODEL // 16
H_W = H_HEADS * H_DV
D_FF = 2816
REL_BUCKETS = 32
REL_MAX_EXACT = 16
REL_MAX_DIST = 128
Q_BLOCK = 128
EPS = 1e-6
SPLIT_SIZES = (2 * M_W, M_W, M_W, M_HEADS, M_HEADS,
               DF_QK_W, DF_QK_W, DF_W,
               H_W, H_W, H_W, H_W)
N_IN = sum(SPLIT_SIZES)

kernel_name = 'hymba_mlstm_diffattn_hgrn2_macaron'


def rms_norm(x, w):
    xf = x.astype(jnp.float32)
    y = xf * lax.rsqrt(jnp.mean(xf * xf, axis=-1, keepdims=True) + EPS)
    return (y * w.astype(jnp.float32)).astype(x.dtype)


def head_layer_norm(x, w):
    xf = x.astype(jnp.float32)
    xc = xf - jnp.mean(xf, axis=-1, keepdims=True)
    y = xc * lax.rsqrt(jnp.mean(xc * xc, axis=-1, keepdims=True) + EPS)
    return (y * w.astype(jnp.float32)).astype(x.dtype)


def swiglu(x, wi, wo):
    g, u = jnp.split(x @ wi, 2, axis=-1)
    return (jax.nn.silu(g) * u) @ wo


def causal_short_conv(x, w, b):
    S = x.shape[1]
    K = w.shape[0]
    xp = jnp.pad(x, ((0, 0), (K - 1, 0), (0, 0)))
    y = b
    for j in range(K):
        y = y + xp[:, j:j + S, :] * w[j]
    return y


def to_heads(t, H):
    B, S, _ = t.shape
    return t.reshape(B, S, H, -1).transpose(0, 2, 1, 3)


def from_heads(t):
    return t.transpose(0, 2, 1, 3)


def to_chunks(t):
    B, H, S = t.shape[:3]
    return jnp.moveaxis(t.reshape(B, H, S // CHUNK, CHUNK, *t.shape[3:]), 2, 0)


def from_chunks(t):
    NC, B, H, L = t.shape[:4]
    return jnp.moveaxis(t, 0, 2).reshape(B, H, NC * L, *t.shape[4:])


def mlstm_chunkwise(q, k, v, log_i, log_f):
    B, H, S, d = q.shape
    tri = jnp.tril(jnp.ones((CHUNK, CHUNK), dtype=bool))

    def step(carry, xs):
        C, n, m = carry
        qc, kc, vc, li, lf = xs
        b = jnp.cumsum(lf, axis=-1)
        dmat = jnp.where(tri, b[..., :, None] - b[..., None, :] + li[..., None, :], -jnp.inf)
        m_inter = b + m[..., None]
        m_t = jnp.maximum(jnp.max(dmat, axis=-1), m_inter)
        w = jnp.exp(dmat - m_t[..., None])
        sc = jnp.einsum('bhtd,bhsd->bhts', qc, kc) * w
        g = jnp.exp(m_inter - m_t)
        num = jnp.einsum('bhts,bhse->bhte', sc, vc) + g[..., None] * jnp.einsum('bhtd,bhde->bhte', qc, C)
        den = jnp.sum(sc, axis=-1) + g * jnp.einsum('bhtd,bhd->bht', qc, n)
        h = num / jnp.maximum(jnp.abs(den), jnp.exp(-m_t))[..., None]
        bl = b[..., -1]
        a = bl[..., None] - b + li
        m_new = jnp.maximum(bl + m, jnp.max(a, axis=-1))
        decay = jnp.exp(bl + m - m_new)
        wa = jnp.exp(a - m_new[..., None])
        C = decay[..., None, None] * C + jnp.einsum('bhs,bhsd,bhse->bhde', wa, kc, vc)
        n = decay[..., None] * n + jnp.einsum('bhs,bhsd->bhd', wa, kc)
        return (C, n, m_new), h

    init = (jnp.zeros((B, H, d, v.shape[-1]), jnp.float32), jnp.zeros((B, H, d), jnp.float32),
            jnp.zeros((B, H), jnp.float32))
    _, h = lax.scan(step, init, (to_chunks(q), to_chunks(k), to_chunks(v), to_chunks(log_i), to_chunks(log_f)))
    return from_chunks(h)


def hgrn2_chunkwise(q, k, v, log_f):
    B, H, S, dk = q.shape
    tri = jnp.tril(jnp.ones((CHUNK, CHUNK), dtype=bool))[:, :, None]

    def step(St, xs):
        qc, kc, vc, lf = xs
        b = jnp.cumsum(lf, axis=2)
        diff = b[:, :, :, None, :] - b[:, :, None, :, :]
        decay = jnp.exp(jnp.where(tri, diff, -jnp.inf))
        A = jnp.einsum('bhtsc,bhsc->bhts', decay * qc[:, :, :, None, :], kc)
        o = jnp.einsum('bhts,bhsv->bhtv', A, vc) + jnp.einsum('bhtc,bhcv->bhtv', qc * jnp.exp(b), St)
        bl = b[:, :, -1:, :]
        St = jnp.exp(bl[:, :, 0, :])[..., None] * St + jnp.einsum('bhsc,bhsv->bhcv', kc * jnp.exp(bl - b), vc)
        return St, o

    init = jnp.zeros((B, H, dk, v.shape[-1]), jnp.float32)
    _, o = lax.scan(step, init, (to_chunks(q), to_chunks(k), to_chunks(v), to_chunks(log_f)))
    return from_chunks(o)


def t5_bucket(rel):
    n = jnp.maximum(rel, 0)
    is_small = n < REL_MAX_EXACT
    nf = jnp.maximum(n, 1).astype(jnp.float32)
    large = REL_MAX_EXACT + (jnp.log(nf / REL_MAX_EXACT) / math.log(REL_MAX_DIST / REL_MAX_EXACT)
                             * (REL_BUCKETS - REL_MAX_EXACT)).astype(jnp.int32)
    large = jnp.minimum(large, REL_BUCKETS - 1)
    return jnp.where(is_small, n, large)


def diff_attention(q1, q2, k1, k2, v, lam, rel_bias):
    B, H, S, d = q1.shape
    nb = S // Q_BLOCK
    scale = d ** -0.5
    kpos = jnp.arange(S, dtype=jnp.int32)
    table = rel_bias.astype(jnp.float32)

    def qblocks(t):
        return jnp.moveaxis(t.reshape(B, H, nb, Q_BLOCK, t.shape[-1]), 2, 0)

    def one_block(args):
        q1b, q2b, j = args
        qpos = j * Q_BLOCK + jnp.arange(Q_BLOCK, dtype=jnp.int32)
        rel = qpos[:, None] - kpos[None, :]
        causal = rel >= 0
        bias = jnp.transpose(table[t5_bucket(rel)], (2, 0, 1))

        def probs(qb, kk):
            s = jnp.einsum('bhqd,bhkd->bhqk', qb, kk) * scale + bias
            return jax.nn.softmax(jnp.where(causal, s, -jnp.inf), axis=-1)

        a = probs(q1b, k1) - lam * probs(q2b, k2)
        return jnp.einsum('bhqk,bhkd->bhqd', a, v)

    out = lax.map(one_block, (qblocks(q1), qblocks(q2), jnp.arange(nb, dtype=jnp.int32)))
    return jnp.moveaxis(out, 0, 2).reshape(B, H, S, v.shape[-1])


def hybrid_mixer(h, layer, w_in, w_out, conv_w, conv_b, ig_b, fg_b, m_norm_w,
                 lam_vecs, df_norm_w, rel_bias, lb, hg_norm_w):
    f32 = jnp.float32
    B, S, _ = h.shape
    u = (h @ w_in).astype(f32)
    idx = [int(i) for i in np.cumsum(SPLIT_SIZES)[:-1]]
    (qk_m, v_m, o_m, i_m, f_m, q_d, k_d, v_d, q_h, f_h, i_h, g_h) = jnp.split(u, idx, axis=-1)

    qk_m = jax.nn.silu(causal_short_conv(qk_m, conv_w.astype(f32), conv_b.astype(f32)))
    q_mc, k_mc = jnp.split(qk_m, 2, axis=-1)
    qm = to_heads(q_mc, M_HEADS)
    km = to_heads(k_mc, M_HEADS) * (M_DH ** -0.5)
    vm = to_heads(v_m, M_HEADS)
    log_i = (i_m + ig_b.astype(f32)).transpose(0, 2, 1)
    log_f = jax.nn.log_sigmoid(f_m + fg_b.astype(f32)).transpose(0, 2, 1)
    hm = from_heads(mlstm_chunkwise(qm, km, vm, log_i, log_f))
    y_m = head_layer_norm(hm, m_norm_w.reshape(M_HEADS, M_DH)).reshape(B, S, M_W) * jax.nn.sigmoid(o_m)

    lam_init = 0.8 - 0.6 * math.exp(-0.3 * layer)
    lv = lam_vecs.astype(f32)
    lam = jnp.exp(jnp.sum(lv[0] * lv[1])) - jnp.exp(jnp.sum(lv[2] * lv[3])) + lam_init
    qd = q_d.reshape(B, S, DF_HEADS, 2, DF_DQK).transpose(0, 2, 1, 3, 4)
    kd = k_d.reshape(B, S, DF_HEADS, 2, DF_DQK).transpose(0, 2, 1, 3, 4)
    vd = to_heads(v_d, DF_HEADS)
    od = diff_attention(qd[..., 0, :], qd[..., 1, :], kd[..., 0, :], kd[..., 1, :], vd, lam, rel_bias)
    y_d = (rms_norm(from_heads(od), df_norm_w.reshape(DF_HEADS, DF_DV)) * (1.0 - lam_init)).reshape(B, S, DF_W)

    qh = jax.nn.silu(to_heads(q_h, H_HEADS))
    fp = to_heads(f_h, H_HEADS)
    lbh = lb.astype(f32).reshape(H_HEADS, 1, H_DK)
    log_fh = jnp.logaddexp(jnp.log(lbh), jnp.log1p(-lbh) + jax.nn.log_sigmoid(fp))
    kh = (1.0 - lbh) * jax.nn.sigmoid(-fp)
    vh = to_heads(i_h, H_HEADS)
    oh = from_heads(hgrn2_chunkwise(qh, kh, vh, log_fh))
    y_h = rms_norm(oh, hg_norm_w.reshape(H_HEADS, H_DV)).reshape(B, S, H_W) * jax.nn.silu(g_h)

    y = jnp.concatenate([y_m, y_d, y_h], axis=-1).astype(h.dtype)
    return y @ w_out


def setup_inputs(seed: int = 0) -> dict:
    key = jax.random.key(seed)
    ks = jax.random.split(key, 20)
    nrm = lambda k, shape, s: jax.random.normal(k, shape, jnp.float32) * s
    fg_base = jnp.linspace(3.0, 6.0, M_HEADS, dtype=jnp.float32)
    return {
        'x': nrm(ks[0], (BATCH, SEQ, D_MODEL), 1.0),
        'norm_w': 1.0 + nrm(ks[1], (DEPTH, 6, D_MODEL), 0.05),
        'ffn1_wi': nrm(ks[2], (DEPTH, D_MODEL, 2 * D_FF), D_MODEL ** -0.5),
        'ffn1_wo': nrm(ks[3], (DEPTH, D_FF, D_MODEL), D_FF ** -0.5),
        'ffn2_wi': nrm(ks[4], (DEPTH, D_MODEL, 2 * D_FF), D_MODEL ** -0.5),
        'ffn2_wo': nrm(ks[5], (DEPTH, D_FF, D_MODEL), D_FF ** -0.5),
        'w_in': nrm(ks[6], (DEPTH, D_MODEL, N_IN), D_MODEL ** -0.5),
        'w_out': nrm(ks[7], (DEPTH, D_MIX, D_MODEL), D_MIX ** -0.5),
        'mlstm_conv_w': nrm(ks[8], (DEPTH, CONV_W, 2 * M_W), CONV_W ** -0.5),
        'mlstm_conv_b': nrm(ks[9], (DEPTH, 2 * M_W), 0.01),
        'mlstm_igate_b': nrm(ks[10], (DEPTH, M_HEADS), 0.1),
        'mlstm_fgate_b': fg_base + nrm(ks[11], (DEPTH, M_HEADS), 0.1),
        'mlstm_norm_w': 1.0 + nrm(ks[12], (DEPTH, M_W), 0.05),
        'diff_lambda': nrm(ks[13], (DEPTH, 4, DF_DQK), 0.1),
        'diff_norm_w': 1.0 + nrm(ks[14], (DEPTH, DF_W), 0.05),
        'rel_bias': nrm(ks[15], (REL_BUCKETS, DF_HEADS), 0.5),
        'hgrn_lb_logits': nrm(ks[16], (DEPTH, H_HEADS * H_DK), 0.5),
        'hgrn_norm_w': 1.0 + nrm(ks[17], (DEPTH, H_W), 0.05),
    }


def reference(x, norm_w, ffn1_wi, ffn1_wo, ffn2_wi, ffn2_wo, w_in, w_out, mlstm_conv_w, mlstm_conv_b,
              mlstm_igate_b, mlstm_fgate_b, mlstm_norm_w, diff_lambda, diff_norm_w, rel_bias,
              hgrn_lb_logits, hgrn_norm_w):
    lb_all = jnp.cumsum(jax.nn.softmax(hgrn_lb_logits.astype(jnp.float32), axis=0), axis=0)
    lb_all = jnp.maximum(lb_all - lb_all[0:1], 0.0)
    for l in range(DEPTH):
        nw = norm_w[l]
        h = swiglu(rms_norm(x, nw[0]), ffn1_wi[l], ffn1_wo[l])
        x = x + 0.5 * rms_norm(h, nw[1])
        h = hybrid_mixer(rms_norm(x, nw[2]), l, w_in[l], w_out[l], mlstm_conv_w[l], mlstm_conv_b[l],
                         mlstm_igate_b[l], mlstm_fgate_b[l], mlstm_norm_w[l], diff_lambda[l],
                         diff_norm_w[l], rel_bias, lb_all[l], hgrn_norm_w[l])
        x = x + rms_norm(h, nw[3])
        h = swiglu(rms_norm(x, nw[4]), ffn2_wi[l], ffn2_wo[l])
        x = x + 0.5 * rms_norm(h, nw[5])
    return x
```

```python
import functools
import math

import numpy as np
import jax
import jax.numpy as jnp
from jax import lax
from jax.experimental import pallas as pl
from jax.experimental.pallas import tpu as pltpu

F32 = jnp.float32
BF16 = jnp.bfloat16
EPS = 1e-6
NEG = -1e30

HEADS = 4
HEAD_W = 64
M_W = HEADS * HEAD_W
DF_HW = 128
DF_W = HEADS * DF_HW
CONV_W = 4
REL_BUCKETS = 32
REL_MAX_EXACT = 16
REL_MAX_DIST = 128
GATE_PAD = 128
SPLIT_SIZES = (2 * M_W, M_W, M_W, HEADS, HEADS, DF_W, DF_W, DF_W, M_W, M_W, M_W, M_W)
VMEM_LIMIT = 56 * 1024 * 1024


def _cparams(sem):
    return pltpu.CompilerParams(dimension_semantics=sem, vmem_limit_bytes=VMEM_LIMIT)


def _sigmoid(x):
    return 1.0 / (1.0 + jnp.exp(-x))


def _log_sigmoid(x):
    return jnp.minimum(x, 0.0) - jnp.log1p(jnp.exp(-jnp.abs(x)))


def _rms(x, w):
    return x * lax.rsqrt(jnp.mean(x * x, axis=-1, keepdims=True) + EPS) * w


def _split_bf16(x, parts):
    out = []
    r = x
    for _ in range(parts):
        p = r.astype(BF16)
        out.append(p)
        r = r - p.astype(F32)
    return out


def _dot01_rhs(x, m01, parts):
    acc = None
    for p in _split_bf16(x, parts):
        t = jnp.dot(p, m01, preferred_element_type=F32)
        acc = t if acc is None else acc + t
    return acc


def _dot01_lhs(m01, x, parts):
    acc = None
    for p in _split_bf16(x, parts):
        t = jnp.dot(m01, p, preferred_element_type=F32)
        acc = t if acc is None else acc + t
    return acc


def _head_of(idx):
    return lax.shift_right_logical(idx, 6)


def _lb_kernel(lg_ref, o_ref):
    lg = lg_ref[...]
    e = jnp.exp(lg - jnp.max(lg, axis=0, keepdims=True))
    sm = e / jnp.sum(e, axis=0, keepdims=True)
    depth = lg.shape[0]
    rows = []
    run = sm[0:1]
    first = run
    for i in range(depth):
        if i > 0:
            run = run + sm[i:i + 1]
        rows.append(jnp.maximum(run - first, 0.0))
    o_ref[...] = jnp.concatenate(rows, axis=0)


def _hgrn_lower_bounds(logits):
    return pl.pallas_call(
        _lb_kernel, out_shape=jax.ShapeDtypeStruct(logits.shape, F32), name="hgrn_lb")(logits.astype(F32))


def _bias_kernel(tab_ref, o_ref, *, tile):
    h = pl.program_id(0)
    typ = pl.program_id(1)
    r = lax.broadcasted_iota(jnp.int32, (tile, tile), 0)
    c = lax.broadcasted_iota(jnp.int32, (tile, tile), 1)
    rel = r - c + typ * tile
    n = jnp.maximum(rel, 0)
    nf = jnp.maximum(n, 1).astype(F32)
    large = REL_MAX_EXACT + (jnp.log(nf / REL_MAX_EXACT) / math.log(REL_MAX_DIST / REL_MAX_EXACT)
                             * (REL_BUCKETS - REL_MAX_EXACT)).astype(jnp.int32)
    large = jnp.minimum(large, REL_BUCKETS - 1)
    bucket = jnp.where(n < REL_MAX_EXACT, n, large)
    bias = jnp.zeros((tile, tile), F32)
    for b in range(REL_BUCKETS):
        bias = jnp.where(bucket == b, tab_ref[b, h], bias)
    o_ref[...] = jnp.where(rel >= 0, bias, NEG)


def _rel_bias_tiles(rel_bias, tile):
    return pl.pallas_call(
        functools.partial(_bias_kernel, tile=tile),
        grid=(HEADS, 2),
        in_specs=[pl.BlockSpec(memory_space=pltpu.SMEM)],
        out_specs=pl.BlockSpec((None, None, tile, tile), lambda h, t: (h, t, 0, 0)),
        out_shape=jax.ShapeDtypeStruct((HEADS, 2, tile, tile), F32),
        name="rel_bias_tiles",
    )(rel_bias.astype(F32))


def _ffn_kernel(x_ref, nwi_ref, wg_ref, wu_ref, wo_ref, nwo_ref, o_ref, xn_ref, acc_ref):
    j = pl.program_id(1)

    @pl.when(j == 0)
    def _():
        xn_ref[...] = _rms(x_ref[...], nwi_ref[...]).astype(BF16)
        acc_ref[...] = jnp.zeros_like(acc_ref)

    xn = xn_ref[...]
    g = jnp.dot(xn, wg_ref[...], preferred_element_type=F32)
    u = jnp.dot(xn, wu_ref[...], preferred_element_type=F32)
    a = (g * _sigmoid(g) * u).astype(BF16)
    acc_ref[...] += jnp.dot(a, wo_ref[...], preferred_element_type=F32)

    @pl.when(j == pl.num_programs(1) - 1)
    def _():
        o_ref[...] = x_ref[...] + 0.5 * _rms(acc_ref[...], nwo_ref[...])


def _ffn(x, nw, layer, row_in, row_out, wi, wo, *, tm, tf):
    m, d = x.shape
    dff = wo.shape[1]
    nf = dff // tf
    return pl.pallas_call(
        _ffn_kernel,
        grid=(m // tm, nf),
        in_specs=[
            pl.BlockSpec((tm, d), lambda i, j: (i, 0)),
            pl.BlockSpec((None, None, 1, d), lambda i, j: (layer, row_in, 0, 0)),
            pl.BlockSpec((None, d, tf), lambda i, j: (layer, 0, j)),
            pl.BlockSpec((None, d, tf), lambda i, j: (layer, 0, nf + j)),
            pl.BlockSpec((None, tf, d), lambda i, j: (layer, j, 0)),
            pl.BlockSpec((None, None, 1, d), lambda i, j: (layer, row_out, 0, 0)),
        ],
        out_specs=pl.BlockSpec((tm, d), lambda i, j: (i, 0)),
        out_shape=jax.ShapeDtypeStruct((m, d), F32),
        scratch_shapes=[pltpu.VMEM((tm, d), BF16), pltpu.VMEM((tm, d), F32)],
        compiler_params=_cparams(("parallel", "arbitrary")),
        name="ffn",
    )(x, nw, wi, wi, wo, nw)


def _proj_kernel(x_ref, nw_ref, w_ref, wgt_ref, cw_ref, cb_ref, gbc_ref, gbr_ref, lb_ref,
                 qm_ref, km_ref, vm_ref, om_ref, gc_ref, gr_ref, qd_ref, kd_ref, vd_ref,
                 qh_ref, lfh_ref, kh_ref, vh_ref, gh_ref, cbuf_ref):
    t = pl.program_id(1)
    tm = x_ref.shape[0]
    xn = _rms(x_ref[...], nw_ref[...]).astype(BF16)
    u = jnp.dot(xn, w_ref[...], preferred_element_type=F32)

    @pl.when(t == 0)
    def _():
        cbuf_ref[0:8, :] = jnp.zeros((8, 2 * M_W), F32)

    qk = u[:, 0:2 * M_W]
    cbuf_ref[8:8 + tm, :] = qk
    cw = cw_ref[...]
    y = cb_ref[...] + cw[CONV_W - 1:CONV_W] * qk
    for d in range(1, CONV_W):
        y = y + cw[CONV_W - 1 - d:CONV_W - d] * cbuf_ref[8 - d:8 - d + tm, :]
    cbuf_ref[0:8, :] = cbuf_ref[tm:tm + 8, :]
    y = y * _sigmoid(y)
    qm_ref[...] = y[:, 0:M_W].astype(qm_ref.dtype)
    km_ref[...] = (y[:, M_W:2 * M_W] * (HEAD_W ** -0.5)).astype(km_ref.dtype)
    vm_ref[...] = u[:, 512:768].astype(vm_ref.dtype)
    om_ref[...] = u[:, 768:1024]

    qd_ref[...] = (u[:, 1024:1536] * (HEAD_W ** -0.5)).astype(qd_ref.dtype)
    kd_ref[...] = u[:, 1536:2048].astype(kd_ref.dtype)
    vd_ref[...] = u[:, 2048:2560].astype(vd_ref.dtype)

    qh = u[:, 2560:2816]
    qh_ref[...] = (qh * _sigmoid(qh)).astype(qh_ref.dtype)
    fp = u[:, 2816:3072]
    lb = lb_ref[...]
    a = jnp.log(lb)
    bb = jnp.log1p(-lb) + _log_sigmoid(fp)
    lfh_ref[...] = jnp.maximum(a, bb) + jnp.log1p(jnp.exp(-jnp.abs(a - bb)))
    kh_ref[...] = ((1.0 - lb) * _sigmoid(-fp)).astype(kh_ref.dtype)
    vh_ref[...] = u[:, 3072:3328].astype(vh_ref.dtype)
    gh_ref[...] = u[:, 3328:3584]

    zc = u[:, 3584:3584 + GATE_PAD] + gbc_ref[...]
    lane = lax.broadcasted_iota(jnp.int32, (1, GATE_PAD), 1)
    gc_ref[...] = jnp.where(lane < HEADS, zc, _log_sigmoid(zc))
    zr = lax.dot_general(wgt_ref[...], xn, (((1,), (1,)), ((), ())), preferred_element_type=F32) + gbr_ref[...]
    row = lax.broadcasted_iota(jnp.int32, (2 * HEADS, 1), 0)
    gr_ref[...] = jnp.where(row < HEADS, zr, _log_sigmoid(zr))


def _proj(x, nw, layer, w, wgt, cw, cb, gbc, gbr, lb, *, batch, tm, act_dtype):
    m, d = x.shape
    seq = m // batch
    nt = seq // tm
    npc = w.shape[-1]
    tok = lambda width: pl.BlockSpec((tm, width), lambda b, t: (b * nt + t, 0))
    lay = lambda *shape: pl.BlockSpec((None,) + shape, lambda b, t: (layer,) + (0,) * len(shape))
    sds = lambda width, dt: jax.ShapeDtypeStruct((m, width), dt)
    out_shape = [sds(M_W, act_dtype), sds(M_W, act_dtype), sds(M_W, act_dtype), sds(M_W, F32),
                 sds(GATE_PAD, F32), jax.ShapeDtypeStruct((batch, 2 * HEADS, seq), F32),
                 sds(DF_W, act_dtype), sds(DF_W, act_dtype), sds(DF_W, act_dtype),
                 sds(M_W, act_dtype), sds(M_W, F32), sds(M_W, act_dtype), sds(M_W, act_dtype), sds(M_W, F32)]
    out_specs = [tok(M_W), tok(M_W), tok(M_W), tok(M_W), tok(GATE_PAD),
                 pl.BlockSpec((None, 2 * HEADS, tm), lambda b, t: (b, 0, t)),
                 tok(DF_W), tok(DF_W), tok(DF_W), tok(M_W), tok(M_W), tok(M_W), tok(M_W), tok(M_W)]
    return pl.pallas_call(
        _proj_kernel,
        grid=(batch, nt),
        in_specs=[tok(d),
                  pl.BlockSpec((None, None, 1, d), lambda b, t: (layer, 2, 0, 0)),
                  lay(d, npc), lay(2 * HEADS, d), lay(CONV_W, 2 * M_W), lay(1, 2 * M_W),
                  lay(1, GATE_PAD), lay(2 * HEADS, 1), lay(1, M_W)],
        out_specs=out_specs,
        out_shape=out_shape,
        scratch_shapes=[pltpu.VMEM((tm + 8, 2 * M_W), F32)],
        compiler_params=_cparams(("parallel", "arbitrary")),
        name="mixer_proj",
    )(x, nw, w, wgt, cw, cb, gbc, gbr, lb)


def _mlstm_kernel(q_ref, k_ref, v_ref, gc_ref, gr_ref, h_ref, c_ref, n_ref, m_ref):
    @pl.when(pl.program_id(1) == 0)
    def _():
        c_ref[...] = jnp.zeros_like(c_ref)
        n_ref[...] = jnp.zeros_like(n_ref)
        m_ref[...] = jnp.zeros_like(m_ref)

    L = q_ref.shape[0]
    q = q_ref[...].astype(F32)
    k = k_ref[...].astype(F32)
    vb = v_ref[...].astype(BF16)
    kb = k.astype(BF16)
    gc = gc_ref[...]
    gr = gr_ref[...]
    row = lax.broadcasted_iota(jnp.int32, (L, L), 0)
    col = lax.broadcasted_iota(jnp.int32, (L, L), 1)
    causal = row >= col
    bcol = _dot01_rhs_t(causal, gc)
    brow = _dot01_rhs(gr, (row <= col).astype(BF16), 3)
    lane_head = _head_of(lax.broadcasted_iota(jnp.int32, (1, M_W), 1))
    cbd = c_ref[...]
    nrow = n_ref[...]
    qc = jnp.dot(q.astype(BF16), cbd.astype(BF16), preferred_element_type=F32)
    hout = jnp.zeros((L, M_W), F32)
    wa_all = jnp.zeros((L, M_W), F32)
    dec_all = jnp.zeros((1, M_W), F32)
    for h in range(HEADS):
        hm = lane_head == h
        bt = bcol[:, HEADS + h:HEADS + h + 1]
        li_c = gc[:, h:h + 1]
        rs = gr[h:h + 1, :] - brow[HEADS + h:HEADS + h + 1, :]
        mprev = m_ref[h:h + 1, 0:1]
        dmat = jnp.where(causal, bt + rs, -jnp.inf)
        m_inter = bt + mprev
        mt = jnp.maximum(jnp.max(dmat, axis=1, keepdims=True), m_inter)
        w = jnp.exp(dmat - mt)
        qh = jnp.where(hm, q, 0.0).astype(BF16)
        sc = lax.dot_general(qh, kb, (((1,), (1,)), ((), ())), preferred_element_type=F32) * w
        g = jnp.exp(m_inter - mt)
        pv = jnp.dot(sc.astype(BF16), vb, preferred_element_type=F32)
        qn = jnp.sum(jnp.where(hm, q * nrow, 0.0), axis=1, keepdims=True)
        den = jnp.sum(sc, axis=1, keepdims=True) + g * qn
        scale = 1.0 / jnp.maximum(jnp.abs(den), jnp.exp(-mt))
        hout = jnp.where(hm, (pv + g * qc) * scale, hout)
        bl = bt[L - 1:L, :]
        a = bl - bt + li_c
        mnew = jnp.maximum(bl + mprev, jnp.max(a, axis=0, keepdims=True))
        wa_all = jnp.where(hm, jnp.exp(a - mnew), wa_all)
        dec_all = jnp.where(hm, jnp.exp(bl + mprev - mnew), dec_all)
        m_ref[h:h + 1, :] = jnp.broadcast_to(mnew, (1, m_ref.shape[1]))
    h_ref[...] = hout
    kw = k * wa_all
    cnew = lax.dot_general(kw.astype(BF16), vb, (((0,), (0,)), ((), ())), preferred_element_type=F32)
    r2 = _head_of(lax.broadcasted_iota(jnp.int32, (M_W, M_W), 0))
    c2 = _head_of(lax.broadcasted_iota(jnp.int32, (M_W, M_W), 1))
    c_ref[...] = dec_all * cbd + jnp.where(r2 == c2, cnew, 0.0)
    n_ref[...] = dec_all * nrow + jnp.sum(kw, axis=0, keepdims=True)


def _dot01_rhs_t(mask, x):
    return _dot01_lhs(mask.astype(BF16), x, 3)


def _mlstm(qm, km, vm, gc, gr, *, batch, chunk):
    m = qm.shape[0]
    seq = m // batch
    nc = seq // chunk
    tok = lambda width: pl.BlockSpec((chunk, width), lambda b, c: (b * nc + c, 0))
    return pl.pallas_call(
        _mlstm_kernel,
        grid=(batch, nc),
        in_specs=[tok(M_W), tok(M_W), tok(M_W), tok(GATE_PAD),
                  pl.BlockSpec((None, 2 * HEADS, chunk), lambda b, c: (b, 0, c))],
        out_specs=tok(M_W),
        out_shape=jax.ShapeDtypeStruct((m, M_W), F32),
        scratch_shapes=[pltpu.VMEM((M_W, M_W), F32), pltpu.VMEM((1, M_W), F32), pltpu.VMEM((8, 128), F32)],
        compiler_params=_cparams(("parallel", "arbitrary")),
        name="mlstm",
    )(qm, km, vm, gc, gr)


HG_BLK = 16


def _hgrn_kernel(q_ref, k_ref, v_ref, lf_ref, o_ref, st_ref):
    @pl.when(pl.program_id(1) == 0)
    def _():
        st_ref[...] = jnp.zeros_like(st_ref)

    nblk = q_ref.shape[0] // HG_BLK
    r16 = lax.broadcasted_iota(jnp.int32, (HG_BLK, HG_BLK), 0)
    c16 = lax.broadcasted_iota(jnp.int32, (HG_BLK, HG_BLK), 1)
    tri16 = (r16 >= c16).astype(BF16)
    rowid = lax.broadcasted_iota(jnp.int32, (HG_BLK, M_W), 0)
    r2 = _head_of(lax.broadcasted_iota(jnp.int32, (M_W, M_W), 0))
    c2 = _head_of(lax.broadcasted_iota(jnp.int32, (M_W, M_W), 1))
    same_head = r2 == c2
    ones_bd = same_head.astype(BF16)

    def body(i, carry):
        r0 = pl.multiple_of(i * HG_BLK, HG_BLK)
        q = q_ref[pl.ds(r0, HG_BLK), :].astype(F32)
        k = k_ref[pl.ds(r0, HG_BLK), :].astype(F32)
        v = v_ref[pl.ds(r0, HG_BLK), :].astype(F32)
        lf = lf_ref[pl.ds(r0, HG_BLK), :]
        b = _dot01_lhs(tri16, lf, 3)
        bl = b[HG_BLK - 1:HG_BLK, :]
        st = st_ref[...]
        o = lax.dot_general((q * jnp.exp(b)).astype(BF16), st.astype(BF16), (((1,), (1,)), ((), ())),
                            preferred_element_type=F32)
        ps = []
        for s in range(HG_BLK):
            dd = jnp.where(rowid >= s, b - b[s:s + 1, :], -jnp.inf)
            ps.append(q * k[s:s + 1, :] * jnp.exp(dd))
        p = jnp.concatenate(ps, axis=0).astype(BF16)
        abig = jnp.dot(p, ones_bd, preferred_element_type=F32)
        for s in range(HG_BLK):
            o = o + abig[s * HG_BLK:(s + 1) * HG_BLK, :] * v[s:s + 1, :]
        o_ref[pl.ds(r0, HG_BLK), :] = o
        ke = (k * jnp.exp(bl - b)).astype(BF16)
        upd = lax.dot_general(v.astype(BF16), ke, (((0,), (0,)), ((), ())), preferred_element_type=F32)
        st_ref[...] = st * jnp.exp(bl) + jnp.where(same_head, upd, 0.0)
        return carry

    lax.fori_loop(0, nblk, body, 0)


def _hgrn(qh, kh, vh, lfh, *, batch, tile):
    m = qh.shape[0]
    seq = m // batch
    nt = seq // tile
    tok = pl.BlockSpec((tile, M_W), lambda b, t: (b * nt + t, 0))
    return pl.pallas_call(
        _hgrn_kernel,
        grid=(batch, nt),
        in_specs=[tok, tok, tok, tok],
        out_specs=tok,
        out_shape=jax.ShapeDtypeStruct((m, M_W), F32),
        scratch_shapes=[pltpu.VMEM((M_W, M_W), F32)],
        compiler_params=_cparams(("parallel", "arbitrary")),
        name="hgrn2",
    )(qh, kh, vh, lfh)


def _attn_kernel(qi_ref, ki_ref, q_ref, k_ref, v_ref, bias_ref, c31_ref, lam0_ref, lv_ref, nw_ref, o_ref,
                 m1_ref, l1_ref, a1_ref, m2_ref, l2_ref, a2_ref):
    h = pl.program_id(1)
    p = pl.program_id(2)
    qi = qi_ref[p]
    ki = ki_ref[p]

    @pl.when(ki == 0)
    def _():
        for m_ref, l_ref, a_ref in ((m1_ref, l1_ref, a1_ref), (m2_ref, l2_ref, a2_ref)):
            m_ref[...] = jnp.full_like(m_ref, NEG)
            l_ref[...] = jnp.zeros_like(l_ref)
            a_ref[...] = jnp.zeros_like(a_ref)

    q = q_ref[...]
    kb = k_ref[...].astype(BF16)
    vb = v_ref[...].astype(BF16)
    lane = lax.broadcasted_iota(jnp.int32, (1, DF_HW), 1)
    first = lane < HEAD_W
    zero = jnp.zeros_like(q)
    bias = jnp.where(ki >= qi - 1, bias_ref[...], c31_ref[h])
    for qq, m_ref, l_ref, a_ref in ((jnp.where(first, q, zero), m1_ref, l1_ref, a1_ref),
                                    (jnp.where(first, zero, q), m2_ref, l2_ref, a2_ref)):
        s = lax.dot_general(qq.astype(BF16), kb, (((1,), (1,)), ((), ())), preferred_element_type=F32) + bias
        m_old = m_ref[...]
        m_new = jnp.maximum(m_old, jnp.max(s, axis=1, keepdims=True))
        alpha = jnp.exp(m_old - m_new)
        pr = jnp.exp(s - m_new)
        l_ref[...] = alpha * l_ref[...] + jnp.sum(pr, axis=1, keepdims=True)
        a_ref[...] = alpha * a_ref[...] + jnp.dot(pr.astype(BF16), vb, preferred_element_type=F32)
        m_ref[...] = m_new

    @pl.when(ki == qi)
    def _():
        lv = lv_ref[...]
        lam0 = lam0_ref[0]
        lam = (jnp.exp(jnp.sum(lv[0:1] * lv[1:2], axis=1, keepdims=True))
               - jnp.exp(jnp.sum(lv[2:3] * lv[3:4], axis=1, keepdims=True)) + lam0)
        od = a1_ref[...] / l1_ref[...] - lam * (a2_ref[...] / l2_ref[...])
        o_ref[...] = (_rms(od, nw_ref[...]) * (1.0 - lam0)).astype(o_ref.dtype)


def _diff_attn(qd, kd, vd, bias_tiles, c31, lam0, lam_vecs, norm_w, layer, *, batch, tile):
    m = qd.shape[0]
    seq = m // batch
    nq = seq // tile
    pairs = [(a, b) for a in range(nq) for b in range(a + 1)]
    qi_tab = jnp.asarray([a for a, _ in pairs], jnp.int32)
    ki_tab = jnp.asarray([b for _, b in pairs], jnp.int32)
    grid_spec = pltpu.PrefetchScalarGridSpec(
        num_scalar_prefetch=2,
        grid=(batch, HEADS, len(pairs)),
        in_specs=[
            pl.BlockSpec((tile, DF_HW), lambda b, h, p, qt, kt: (b * nq + qt[p], h)),
            pl.BlockSpec((tile, DF_HW), lambda b, h, p, qt, kt: (b * nq + kt[p], h)),
            pl.BlockSpec((tile, DF_HW), lambda b, h, p, qt, kt: (b * nq + kt[p], h)),
            pl.BlockSpec((None, None, tile, tile),
                         lambda b, h, p, qt, kt: (h, jnp.minimum(qt[p] - kt[p], 1), 0, 0)),
            pl.BlockSpec(memory_space=pltpu.SMEM),
            pl.BlockSpec(memory_space=pltpu.SMEM),
            pl.BlockSpec((None, 4, HEAD_W), lambda b, h, p, qt, kt: (layer, 0, 0)),
            pl.BlockSpec((None, 1, DF_HW), lambda b, h, p, qt, kt: (layer, 0, h)),
        ],
        out_specs=pl.BlockSpec((tile, DF_HW), lambda b, h, p, qt, kt: (b * nq + qt[p], h)),
        scratch_shapes=[pltpu.VMEM((tile, 1), F32), pltpu.VMEM((tile, 1), F32), pltpu.VMEM((tile, DF_HW), F32),
                        pltpu.VMEM((tile, 1), F32), pltpu.VMEM((tile, 1), F32), pltpu.VMEM((tile, DF_HW), F32)],
    )
    return pl.pallas_call(
        _attn_kernel,
        grid_spec=grid_spec,
        out_shape=jax.ShapeDtypeStruct((m, DF_W), BF16),
        compiler_params=_cparams(("parallel", "parallel", "arbitrary")),
        name="diff_attn",
    )(qi_tab, ki_tab, qd, kd, vd, bias_tiles, c31, lam0, lam_vecs, norm_w)


def _mix_out_kernel(x_ref, hm_ref, om_ref, yd_ref, oh_ref, gh_ref, mnw_ref, hnw_ref, w_ref, nw_ref, o_ref):
    r2 = _head_of(lax.broadcasted_iota(jnp.int32, (M_W, M_W), 0))
    c2 = _head_of(lax.broadcasted_iota(jnp.int32, (M_W, M_W), 1))
    ones_bd = (r2 == c2).astype(BF16)
    inv = 1.0 / HEAD_W

    hm = hm_ref[...]
    xc = hm - _dot01_rhs(hm, ones_bd, 2) * inv
    var = _dot01_rhs(xc * xc, ones_bd, 2) * inv
    ym = xc * lax.rsqrt(var + EPS) * mnw_ref[...] * _sigmoid(om_ref[...])

    oh = oh_ref[...]
    ms = _dot01_rhs(oh * oh, ones_bd, 2) * inv
    gh = gh_ref[...]
    yh = oh * lax.rsqrt(ms + EPS) * hnw_ref[...] * (gh * _sigmoid(gh))

    w = w_ref[...]
    acc = jnp.dot(ym.astype(BF16), w[0:M_W], preferred_element_type=F32)
    acc = acc + jnp.dot(yd_ref[...].astype(BF16), w[M_W:M_W + DF_W], preferred_element_type=F32)
    acc = acc + jnp.dot(yh.astype(BF16), w[M_W + DF_W:], preferred_element_type=F32)
    o_ref[...] = x_ref[...] + _rms(acc, nw_ref[...])


def _mix_out(x, hm, om, yd, oh, gh, mnw, hnw, w_out, nw, layer, *, tm):
    m, d = x.shape
    tok = lambda width: pl.BlockSpec((tm, width), lambda i: (i, 0))
    lay = lambda *shape: pl.BlockSpec((None,) + shape, lambda i: (layer,) + (0,) * len(shape))
    return pl.pallas_call(
        _mix_out_kernel,
        grid=(m // tm,),
        in_specs=[tok(d), tok(M_W), tok(M_W), tok(DF_W), tok(M_W), tok(M_W),
                  lay(1, M_W), lay(1, M_W), lay(d, d),
                  pl.BlockSpec((None, None, 1, d), lambda i: (layer, 3, 0, 0))],
        out_specs=tok(d),
        out_shape=jax.ShapeDtypeStruct((m, d), F32),
        compiler_params=_cparams(("parallel",)),
        name="mix_out",
    )(x, hm, om, yd, oh, gh, mnw, hnw, w_out, nw)


def _tiles(batch, seq):
    return dict(
        ffn_tm=min(512, seq), proj_tm=min(256, seq), mlstm_chunk=min(256, seq), hgrn_tile=min(256, seq),
        attn_tile=min(512, seq), out_tm=min(512, seq))


def _ffn_tf(dff):
    return dff // 2 if (dff // 2) % 128 == 0 else dff


def kernel(x, norm_w, ffn1_wi, ffn1_wo, ffn2_wi, ffn2_wo, w_in, w_out, mlstm_conv_w, mlstm_conv_b, mlstm_igate_b,
           mlstm_fgate_b, mlstm_norm_w, diff_lambda, diff_norm_w, rel_bias, hgrn_lb_logits, hgrn_norm_w):
    batch, seq, d = x.shape
    depth = norm_w.shape[0]
    tl = _tiles(batch, seq)
    tf = _ffn_tf(ffn1_wo.shape[1])
    act_dtype = BF16

    off = np.concatenate([[0], np.cumsum(SPLIT_SIZES)])
    seg = lambda i: np.arange(off[i], off[i + 1])
    order = np.concatenate([seg(0), seg(1), seg(2), seg(5), seg(6), seg(7), seg(8), seg(9), seg(10), seg(11),
                            seg(3), seg(4)])
    w_main = jnp.take(w_in, jnp.asarray(order), axis=2)
    w_proj = jnp.pad(w_main, ((0, 0), (0, 0), (0, GATE_PAD - 2 * HEADS))).astype(BF16)
    w_gate_t = jnp.swapaxes(w_in[:, :, off[3]:off[5]], 1, 2).astype(BF16)
    gate_b = jnp.concatenate([mlstm_igate_b, mlstm_fgate_b], axis=1).astype(F32)
    gate_b_col = jnp.pad(gate_b, ((0, 0), (0, GATE_PAD - 2 * HEADS)))[:, None, :]
    gate_b_row = gate_b[:, :, None]
    wi1, wo1 = ffn1_wi.astype(BF16), ffn1_wo.astype(BF16)
    wi2, wo2 = ffn2_wi.astype(BF16), ffn2_wo.astype(BF16)
    w_o = w_out.astype(BF16)
    nw = norm_w.astype(F32)[:, :, None, :]
    conv_w = mlstm_conv_w.astype(F32)
    conv_b = mlstm_conv_b.astype(F32)[:, None, :]
    mnw = mlstm_norm_w.astype(F32)[:, None, :]
    hnw = hgrn_norm_w.astype(F32)[:, None, :]
    dnw = diff_norm_w.astype(F32)[:, None, :]
    lam_vecs = diff_lambda.astype(F32)

    lb_all = _hgrn_lower_bounds(hgrn_lb_logits)[:, None, :]
    bias_tiles = _rel_bias_tiles(rel_bias, tl["attn_tile"])
    c31 = rel_bias[REL_BUCKETS - 1].astype(F32)

    xf = x.reshape(batch * seq, d)
    for l in range(depth):
        lam0 = jnp.full((1,), 0.8 - 0.6 * math.exp(-0.3 * l), F32)
        xf = _ffn(xf, nw, l, 0, 1, wi1, wo1, tm=tl["ffn_tm"], tf=tf)
        (qm, km, vm, om, gc, gr, qd, kd, vd, qh, lfh, kh, vh, gh) = _proj(
            xf, nw, l, w_proj, w_gate_t, conv_w, conv_b, gate_b_col, gate_b_row, lb_all,
            batch=batch, tm=tl["proj_tm"], act_dtype=act_dtype)
        hm = _mlstm(qm, km, vm, gc, gr, batch=batch, chunk=tl["mlstm_chunk"])
        yd = _diff_attn(qd, kd, vd, bias_tiles, c31, lam0, lam_vecs, dnw, l, batch=batch, tile=tl["attn_tile"])
        oh = _hgrn(qh, kh, vh, lfh, batch=batch, tile=tl["hgrn_tile"])
        xf = _mix_out(xf, hm, om, yd, oh, gh, mnw, hnw, w_o, nw, l, tm=tl["out_tm"])
        xf = _ffn(xf, nw, l, 4, 5, wi2, wo2, tm=tl["ffn_tm"], tf=tf)
    return xf.reshape(batch, seq, d)
```

```python
import functools
import math

import numpy as np
import jax
import jax.numpy as jnp
from jax import lax
from jax.experimental import pallas as pl
from jax.experimental.pallas import tpu as pltpu

F32 = jnp.float32
BF16 = jnp.bfloat16
EPS = 1e-6
NEG = -1e30
LOG2E = 1.4426950408889634

HEADS = 4
HEAD_W = 64
M_W = HEADS * HEAD_W
DF_HW = 128
DF_W = HEADS * DF_HW
CONV_W = 4
REL_BUCKETS = 32
REL_MAX_EXACT = 16
REL_MAX_DIST = 128
GATE_PAD = 128
SPLIT_SIZES = (2 * M_W, M_W, M_W, HEADS, HEADS, DF_W, DF_W, DF_W, M_W, M_W, M_W, M_W)
VMEM_LIMIT = 56 * 1024 * 1024


def _cparams(sem):
    return pltpu.CompilerParams(dimension_semantics=sem, vmem_limit_bytes=VMEM_LIMIT)


def _sigmoid(x):
    return 1.0 / (1.0 + jnp.exp(-x))


def _log_sigmoid(x):
    return jnp.minimum(x, 0.0) - jnp.log1p(jnp.exp(-jnp.abs(x)))


def _rms(x, w):
    return x * lax.rsqrt(jnp.mean(x * x, axis=-1, keepdims=True) + EPS) * w


def _split_bf16(x, parts):
    out = []
    r = x
    for _ in range(parts):
        p = r.astype(BF16)
        out.append(p)
        r = r - p.astype(F32)
    return out


def _dot01_rhs(x, m01, parts):
    acc = None
    for p in _split_bf16(x, parts):
        t = jnp.dot(p, m01, preferred_element_type=F32)
        acc = t if acc is None else acc + t
    return acc


def _dot01_lhs(m01, x, parts):
    acc = None
    for p in _split_bf16(x, parts):
        t = jnp.dot(m01, p, preferred_element_type=F32)
        acc = t if acc is None else acc + t
    return acc


def _head_of(idx):
    return lax.shift_right_logical(idx, 6)


def _lb_kernel(lg_ref, o_ref):
    lg = lg_ref[...]
    e = jnp.exp(lg - jnp.max(lg, axis=0, keepdims=True))
    sm = e / jnp.sum(e, axis=0, keepdims=True)
    depth = lg.shape[0]
    rows = []
    run = sm[0:1]
    first = run
    for i in range(depth):
        if i > 0:
            run = run + sm[i:i + 1]
        rows.append(jnp.maximum(run - first, 0.0))
    o_ref[...] = jnp.concatenate(rows, axis=0)


def _hgrn_lower_bounds(logits):
    return pl.pallas_call(
        _lb_kernel, out_shape=jax.ShapeDtypeStruct(logits.shape, F32), name="hgrn_lb")(logits.astype(F32))


def _bias_kernel(tab_ref, o_ref, *, tile):
    h = pl.program_id(0)
    typ = pl.program_id(1)
    r = lax.broadcasted_iota(jnp.int32, (tile, tile), 0)
    c = lax.broadcasted_iota(jnp.int32, (tile, tile), 1)
    rel = c - r + typ * tile
    n = jnp.maximum(rel, 0)
    nf = jnp.maximum(n, 1).astype(F32)
    large = REL_MAX_EXACT + (jnp.log(nf / REL_MAX_EXACT) / math.log(REL_MAX_DIST / REL_MAX_EXACT)
                             * (REL_BUCKETS - REL_MAX_EXACT)).astype(jnp.int32)
    large = jnp.minimum(large, REL_BUCKETS - 1)
    bucket = jnp.where(n < REL_MAX_EXACT, n, large)
    bias = jnp.zeros((tile, tile), F32)
    for b in range(REL_BUCKETS):
        bias = jnp.where(bucket == b, tab_ref[b, h], bias)
    o_ref[...] = jnp.where(rel >= 0, bias * LOG2E, NEG)


def _rel_bias_tiles(rel_bias, tile):
    return pl.pallas_call(
        functools.partial(_bias_kernel, tile=tile),
        grid=(HEADS, 2),
        in_specs=[pl.BlockSpec(memory_space=pltpu.SMEM)],
        out_specs=pl.BlockSpec((None, None, tile, tile), lambda h, t: (h, t, 0, 0)),
        out_shape=jax.ShapeDtypeStruct((HEADS, 2, tile, tile), F32),
        name="rel_bias_tiles",
    )(rel_bias.astype(F32))


def _ffn_kernel(x_ref, nwi_ref, wg_ref, wu_ref, wo_ref, nwo_ref, o_ref, xn_ref, acc_ref):
    j = pl.program_id(1)

    @pl.when(j == 0)
    def _():
        xn_ref[...] = _rms(x_ref[...], nwi_ref[...]).astype(BF16)
        acc_ref[...] = jnp.zeros_like(acc_ref)

    xn = xn_ref[...]
    g = jnp.dot(xn, wg_ref[...], preferred_element_type=F32)
    u = jnp.dot(xn, wu_ref[...], preferred_element_type=F32)
    a = (g * _sigmoid(g) * u).astype(BF16)
    acc_ref[...] += jnp.dot(a, wo_ref[...], preferred_element_type=F32)

    @pl.when(j == pl.num_programs(1) - 1)
    def _():
        o_ref[...] = x_ref[...] + 0.5 * _rms(acc_ref[...], nwo_ref[...])


def _ffn(x, nw, layer, row_in, row_out, wi, wo, *, tm, tf):
    m, d = x.shape
    dff = wo.shape[1]
    nf = dff // tf
    return pl.pallas_call(
        _ffn_kernel,
        grid=(m // tm, nf),
        in_specs=[
            pl.BlockSpec((tm, d), lambda i, j: (i, 0)),
            pl.BlockSpec((None, None, 1, d), lambda i, j: (layer, row_in, 0, 0)),
            pl.BlockSpec((None, d, tf), lambda i, j: (layer, 0, j)),
            pl.BlockSpec((None, d, tf), lambda i, j: (layer, 0, nf + j)),
            pl.BlockSpec((None, tf, d), lambda i, j: (layer, j, 0)),
            pl.BlockSpec((None, None, 1, d), lambda i, j: (layer, row_out, 0, 0)),
        ],
        out_specs=pl.BlockSpec((tm, d), lambda i, j: (i, 0)),
        out_shape=jax.ShapeDtypeStruct((m, d), F32),
        scratch_shapes=[pltpu.VMEM((tm, d), BF16), pltpu.VMEM((tm, d), F32)],
        compiler_params=_cparams(("parallel", "arbitrary")),
        name="ffn",
    )(x, nw, wi, wi, wo, nw)


def _proj_kernel(x_ref, nw_ref, w_ref, wvt_ref, wgt_ref, cw_ref, cb_ref, gbc_ref, gbr_ref, lb_ref,
                 qm_ref, km_ref, vm_ref, om_ref, gc_ref, gr_ref, qd_ref, kd_ref, vdt_ref,
                 qh_ref, lfh_ref, kh_ref, vh_ref, gh_ref, cbuf_ref):
    t = pl.program_id(1)
    tm = x_ref.shape[0]
    xn = _rms(x_ref[...], nw_ref[...]).astype(BF16)
    u = jnp.dot(xn, w_ref[...], preferred_element_type=F32)

    @pl.when(t == 0)
    def _():
        cbuf_ref[0:8, :] = jnp.zeros((8, 2 * M_W), F32)

    qk = u[:, 0:2 * M_W]
    cbuf_ref[8:8 + tm, :] = qk
    cw = cw_ref[...]
    y = cb_ref[...] + cw[CONV_W - 1:CONV_W] * qk
    for d in range(1, CONV_W):
        y = y + cw[CONV_W - 1 - d:CONV_W - d] * cbuf_ref[8 - d:8 - d + tm, :]
    cbuf_ref[0:8, :] = cbuf_ref[tm:tm + 8, :]
    y = y * _sigmoid(y)
    qm_ref[...] = y[:, 0:M_W].astype(qm_ref.dtype)
    km_ref[...] = (y[:, M_W:2 * M_W] * (HEAD_W ** -0.5)).astype(km_ref.dtype)
    vm_ref[...] = u[:, 512:768].astype(vm_ref.dtype)
    om_ref[...] = u[:, 768:1024]

    qd_ref[...] = (u[:, 1024:1536] * (HEAD_W ** -0.5 * LOG2E)).astype(qd_ref.dtype)
    kd_ref[...] = u[:, 1536:2048].astype(kd_ref.dtype)
    vdt_ref[...] = lax.dot_general(wvt_ref[...], xn, (((1,), (1,)), ((), ())),
                                   preferred_element_type=F32).astype(vdt_ref.dtype)

    qh = u[:, 2048:2304]
    qh_ref[...] = (qh * _sigmoid(qh)).astype(qh_ref.dtype)
    fp = u[:, 2304:2560]
    lb = lb_ref[...]
    a = jnp.log(lb)
    bb = jnp.log1p(-lb) + _log_sigmoid(fp)
    lfh_ref[...] = jnp.maximum(a, bb) + jnp.log1p(jnp.exp(-jnp.abs(a - bb)))
    kh_ref[...] = ((1.0 - lb) * _sigmoid(-fp)).astype(kh_ref.dtype)
    vh_ref[...] = u[:, 2560:2816].astype(vh_ref.dtype)
    gh_ref[...] = u[:, 2816:3072]

    zc = u[:, 3072:3072 + GATE_PAD] + gbc_ref[...]
    lane = lax.broadcasted_iota(jnp.int32, (1, GATE_PAD), 1)
    gc_ref[...] = jnp.where(lane < HEADS, zc, _log_sigmoid(zc))
    zr = lax.dot_general(wgt_ref[...], xn, (((1,), (1,)), ((), ())), preferred_element_type=F32) + gbr_ref[...]
    row = lax.broadcasted_iota(jnp.int32, (2 * HEADS, 1), 0)
    gr_ref[...] = jnp.where(row < HEADS, zr, _log_sigmoid(zr))


def _proj(x, nw, layer, w, wvt, wgt, cw, cb, gbc, gbr, lb, *, batch, tm, act_dtype):
    m, d = x.shape
    seq = m // batch
    nt = seq // tm
    npc = w.shape[-1]
    tok = lambda width: pl.BlockSpec((tm, width), lambda b, t: (b * nt + t, 0))
    lay = lambda *shape: pl.BlockSpec((None,) + shape, lambda b, t: (layer,) + (0,) * len(shape))
    sds = lambda width, dt: jax.ShapeDtypeStruct((m, width), dt)
    out_shape = [sds(M_W, act_dtype), sds(M_W, act_dtype), sds(M_W, act_dtype), sds(M_W, F32),
                 sds(GATE_PAD, F32), jax.ShapeDtypeStruct((batch, 2 * HEADS, seq), F32),
                 sds(DF_W, act_dtype), sds(DF_W, act_dtype),
                 jax.ShapeDtypeStruct((batch, nt, DF_W, tm), act_dtype),
                 sds(M_W, act_dtype), sds(M_W, F32), sds(M_W, act_dtype), sds(M_W, act_dtype), sds(M_W, F32)]
    out_specs = [tok(M_W), tok(M_W), tok(M_W), tok(M_W), tok(GATE_PAD),
                 pl.BlockSpec((None, 2 * HEADS, tm), lambda b, t: (b, 0, t)),
                 tok(DF_W), tok(DF_W),
                 pl.BlockSpec((None, None, DF_W, tm), lambda b, t: (b, t, 0, 0)),
                 tok(M_W), tok(M_W), tok(M_W), tok(M_W), tok(M_W)]
    return pl.pallas_call(
        _proj_kernel,
        grid=(batch, nt),
        in_specs=[tok(d),
                  pl.BlockSpec((None, None, 1, d), lambda b, t: (layer, 2, 0, 0)),
                  lay(d, npc), lay(DF_W, d), lay(2 * HEADS, d), lay(CONV_W, 2 * M_W), lay(1, 2 * M_W),
                  lay(1, GATE_PAD), lay(2 * HEADS, 1), lay(1, M_W)],
        out_specs=out_specs,
        out_shape=out_shape,
        scratch_shapes=[pltpu.VMEM((tm + 8, 2 * M_W), F32)],
        compiler_params=_cparams(("parallel", "arbitrary")),
        name="mixer_proj",
    )(x, nw, w, wvt, wgt, cw, cb, gbc, gbr, lb)


def _mlstm_kernel(q_ref, k_ref, v_ref, gc_ref, gr_ref, h_ref, c_ref, n_ref, m_ref):
    @pl.when(pl.program_id(1) == 0)
    def _():
        c_ref[...] = jnp.zeros_like(c_ref)
        n_ref[...] = jnp.zeros_like(n_ref)
        m_ref[...] = jnp.zeros_like(m_ref)

    L = q_ref.shape[0]
    q = q_ref[...].astype(F32)
    k = k_ref[...].astype(F32)
    vb = v_ref[...].astype(BF16)
    kb = k.astype(BF16)
    gc = gc_ref[...]
    gr = gr_ref[...]
    row = lax.broadcasted_iota(jnp.int32, (L, L), 0)
    col = lax.broadcasted_iota(jnp.int32, (L, L), 1)
    causal = row >= col
    bcol = _dot01_rhs_t(causal, gc)
    brow = _dot01_rhs(gr, (row <= col).astype(BF16), 3)
    lane_head = _head_of(lax.broadcasted_iota(jnp.int32, (1, M_W), 1))
    cbd = c_ref[...]
    nrow = n_ref[...]
    qc = jnp.dot(q.astype(BF16), cbd.astype(BF16), preferred_element_type=F32)
    hout = jnp.zeros((L, M_W), F32)
    wa_all = jnp.zeros((L, M_W), F32)
    dec_all = jnp.zeros((1, M_W), F32)
    for h in range(HEADS):
        hm = lane_head == h
        bt = bcol[:, HEADS + h:HEADS + h + 1]
        li_c = gc[:, h:h + 1]
        rs = gr[h:h + 1, :] - brow[HEADS + h:HEADS + h + 1, :]
        mprev = m_ref[h:h + 1, 0:1]
        dmat = jnp.where(causal, bt + rs, -jnp.inf)
        m_inter = bt + mprev
        mt = jnp.maximum(jnp.max(dmat, axis=1, keepdims=True), m_inter)
        w = jnp.exp(dmat - mt)
        qh = jnp.where(hm, q, 0.0).astype(BF16)
        sc = lax.dot_general(qh, kb, (((1,), (1,)), ((), ())), preferred_element_type=F32) * w
        g = jnp.exp(m_inter - mt)
        pv = jnp.dot(sc.astype(BF16), vb, preferred_element_type=F32)
        qn = jnp.sum(jnp.where(hm, q * nrow, 0.0), axis=1, keepdims=True)
        den = jnp.sum(sc, axis=1, keepdims=True) + g * qn
        scale = 1.0 / jnp.maximum(jnp.abs(den), jnp.exp(-mt))
        hout = jnp.where(hm, (pv + g * qc) * scale, hout)
        bl = bt[L - 1:L, :]
        a = bl - bt + li_c
        mnew = jnp.maximum(bl + mprev, jnp.max(a, axis=0, keepdims=True))
        wa_all = jnp.where(hm, jnp.exp(a - mnew), wa_all)
        dec_all = jnp.where(hm, jnp.exp(bl + mprev - mnew), dec_all)
        m_ref[h:h + 1, :] = jnp.broadcast_to(mnew, (1, m_ref.shape[1]))
    h_ref[...] = hout
    kw = k * wa_all
    cnew = lax.dot_general(kw.astype(BF16), vb, (((0,), (0,)), ((), ())), preferred_element_type=F32)
    r2 = _head_of(lax.broadcasted_iota(jnp.int32, (M_W, M_W), 0))
    c2 = _head_of(lax.broadcasted_iota(jnp.int32, (M_W, M_W), 1))
    c_ref[...] = dec_all * cbd + jnp.where(r2 == c2, cnew, 0.0)
    n_ref[...] = dec_all * nrow + jnp.sum(kw, axis=0, keepdims=True)


def _dot01_rhs_t(mask, x):
    return _dot01_lhs(mask.astype(BF16), x, 3)


def _mlstm(qm, km, vm, gc, gr, *, batch, chunk):
    m = qm.shape[0]
    seq = m // batch
    nc = seq // chunk
    tok = lambda width: pl.BlockSpec((chunk, width), lambda b, c: (b * nc + c, 0))
    return pl.pallas_call(
        _mlstm_kernel,
        grid=(batch, nc),
        in_specs=[tok(M_W), tok(M_W), tok(M_W), tok(GATE_PAD),
                  pl.BlockSpec((None, 2 * HEADS, chunk), lambda b, c: (b, 0, c))],
        out_specs=tok(M_W),
        out_shape=jax.ShapeDtypeStruct((m, M_W), F32),
        scratch_shapes=[pltpu.VMEM((M_W, M_W), F32), pltpu.VMEM((1, M_W), F32), pltpu.VMEM((8, 128), F32)],
        compiler_params=_cparams(("parallel", "arbitrary")),
        name="mlstm",
    )(qm, km, vm, gc, gr)


HG_BLK = 16


def _hgrn_kernel(q_ref, k_ref, v_ref, lf_ref, o_ref, st_ref):
    @pl.when(pl.program_id(1) == 0)
    def _():
        st_ref[...] = jnp.zeros_like(st_ref)

    nblk = q_ref.shape[0] // HG_BLK
    r16 = lax.broadcasted_iota(jnp.int32, (HG_BLK, HG_BLK), 0)
    c16 = lax.broadcasted_iota(jnp.int32, (HG_BLK, HG_BLK), 1)
    tri16 = (r16 >= c16).astype(BF16)
    rowid = lax.broadcasted_iota(jnp.int32, (HG_BLK, M_W), 0)
    r2 = _head_of(lax.broadcasted_iota(jnp.int32, (M_W, M_W), 0))
    c2 = _head_of(lax.broadcasted_iota(jnp.int32, (M_W, M_W), 1))
    same_head = r2 == c2
    ones_bd = same_head.astype(BF16)

    def body(i, carry):
        r0 = pl.multiple_of(i * HG_BLK, HG_BLK)
        q = q_ref[pl.ds(r0, HG_BLK), :].astype(F32)
        k = k_ref[pl.ds(r0, HG_BLK), :].astype(F32)
        v = v_ref[pl.ds(r0, HG_BLK), :].astype(F32)
        lf = lf_ref[pl.ds(r0, HG_BLK), :]
        b = _dot01_lhs(tri16, lf, 3)
        bl = b[HG_BLK - 1:HG_BLK, :]
        st = st_ref[...]
        o = lax.dot_general((q * jnp.exp(b)).astype(BF16), st.astype(BF16), (((1,), (1,)), ((), ())),
                            preferred_element_type=F32)
        ps = []
        for s in range(HG_BLK):
            dd = jnp.where(rowid >= s, b - b[s:s + 1, :], -jnp.inf)
            ps.append(q * k[s:s + 1, :] * jnp.exp(dd))
        p = jnp.concatenate(ps, axis=0).astype(BF16)
        abig = jnp.dot(p, ones_bd, preferred_element_type=F32)
        for s in range(HG_BLK):
            o = o + abig[s * HG_BLK:(s + 1) * HG_BLK, :] * v[s:s + 1, :]
        o_ref[pl.ds(r0, HG_BLK), :] = o
        ke = (k * jnp.exp(bl - b)).astype(BF16)
        upd = lax.dot_general(v.astype(BF16), ke, (((0,), (0,)), ((), ())), preferred_element_type=F32)
        st_ref[...] = st * jnp.exp(bl) + jnp.where(same_head, upd, 0.0)
        return carry

    lax.fori_loop(0, nblk, body, 0)


def _hgrn(qh, kh, vh, lfh, *, batch, tile):
    m = qh.shape[0]
    seq = m // batch
    nt = seq // tile
    tok = pl.BlockSpec((tile, M_W), lambda b, t: (b * nt + t, 0))
    return pl.pallas_call(
        _hgrn_kernel,
        grid=(batch, nt),
        in_specs=[tok, tok, tok, tok],
        out_specs=tok,
        out_shape=jax.ShapeDtypeStruct((m, M_W), F32),
        scratch_shapes=[pltpu.VMEM((M_W, M_W), F32)],
        compiler_params=_cparams(("parallel", "arbitrary")),
        name="hgrn2",
    )(qh, kh, vh, lfh)


ONES_ROWS = 16


def _attn_kernel(q_ref, k_ref, vt_ref, bias_ref, c31_ref, lam0_ref, lv_ref, nwt_ref, o_ref, m_ref, acc_ref, *, tile):
    h = pl.program_id(1)
    qi = pl.program_id(2)
    vchunk = vt_ref.shape[2]
    nvc = tile // vchunk
    m_ref[...] = jnp.full_like(m_ref, NEG)
    acc_ref[...] = jnp.zeros_like(acc_ref)
    q = q_ref[...]
    first = lax.broadcasted_iota(jnp.int32, (1, DF_HW), 1) < HEAD_W
    zero = jnp.zeros_like(q)
    qs = (jnp.where(first, q, zero), jnp.where(first, zero, q))
    ones = jnp.ones((ONES_ROWS, vchunk), BF16)
    far_bias = c31_ref[h] * LOG2E

    def tile_step(ki, kind):
        kb = k_ref[pl.ds(pl.multiple_of(ki * tile, tile), tile), :]
        vts = [jnp.concatenate([vt_ref[ki * nvc + c], ones], axis=0) for c in range(nvc)]
        for j in range(2):
            st = lax.dot_general(kb, qs[j], (((1,), (1,)), ((), ())), preferred_element_type=F32)
            if kind == "sub":
                st = st + bias_ref[1]
            elif kind == "diag":
                st = st + bias_ref[0]
            mcur = jnp.max(st, axis=0, keepdims=True)
            if kind == "far":
                mcur = mcur + far_bias
            m_old = m_ref[j]
            m_new = jnp.maximum(m_old, mcur)
            alpha = jnp.exp2(m_old - m_new)
            shift = m_new - far_bias if kind == "far" else m_new
            pt = jnp.exp2((st - shift).astype(BF16))
            pv = None
            for c in range(nvc):
                t = jnp.dot(vts[c], pt[c * vchunk:(c + 1) * vchunk, :], preferred_element_type=F32)
                pv = t if pv is None else pv + t
            acc_ref[j] = alpha * acc_ref[j] + pv
            m_ref[j] = m_new

    def far_body(ki, carry):
        tile_step(ki, "far")
        return carry

    lax.fori_loop(0, qi - 1, far_body, 0)

    @pl.when(qi >= 1)
    def _():
        tile_step(qi - 1, "sub")

    tile_step(qi, "diag")

    lv = lv_ref[...]
    lam0 = lam0_ref[0]
    lam = (jnp.exp(jnp.sum(lv[0:1] * lv[1:2], axis=1, keepdims=True))
           - jnp.exp(jnp.sum(lv[2:3] * lv[3:4], axis=1, keepdims=True)) + lam0)
    a1 = acc_ref[0]
    a2 = acc_ref[1]
    od = a1[0:DF_HW] / a1[DF_HW:DF_HW + 1] - lam * (a2[0:DF_HW] / a2[DF_HW:DF_HW + 1])
    ms = jnp.mean(od * od, axis=0, keepdims=True)
    yt = od * lax.rsqrt(ms + EPS) * nwt_ref[...] * (1.0 - lam0)
    o_ref[...] = yt.T.astype(o_ref.dtype)


def _diff_attn(qd, kd, vdt, bias_tiles, c31, lam0, lam_vecs, norm_w_t, layer, *, batch, tile):
    m = qd.shape[0]
    seq = m // batch
    nq = seq // tile
    nchunk, vchunk = vdt.shape[1], vdt.shape[3]
    return pl.pallas_call(
        functools.partial(_attn_kernel, tile=tile),
        grid=(batch, HEADS, nq),
        in_specs=[
            pl.BlockSpec((tile, DF_HW), lambda b, h, i: (b * nq + i, h)),
            pl.BlockSpec((seq, DF_HW), lambda b, h, i: (b, h)),
            pl.BlockSpec((None, nchunk, DF_HW, vchunk), lambda b, h, i: (b, 0, h, 0)),
            pl.BlockSpec((None, 2, tile, tile), lambda b, h, i: (h, 0, 0, 0)),
            pl.BlockSpec(memory_space=pltpu.SMEM),
            pl.BlockSpec(memory_space=pltpu.SMEM),
            pl.BlockSpec((None, 4, HEAD_W), lambda b, h, i: (layer, 0, 0)),
            pl.BlockSpec((None, DF_HW, 1), lambda b, h, i: (layer, h, 0)),
        ],
        out_specs=pl.BlockSpec((tile, DF_HW), lambda b, h, i: (b * nq + i, h)),
        out_shape=jax.ShapeDtypeStruct((m, DF_W), BF16),
        scratch_shapes=[pltpu.VMEM((2, 1, tile), F32), pltpu.VMEM((2, DF_HW + ONES_ROWS, tile), F32)],
        compiler_params=_cparams(("parallel", "parallel", "arbitrary")),
        name="diff_attn",
    )(qd, kd, vdt, bias_tiles, c31, lam0, lam_vecs, norm_w_t)


def _mix_out_kernel(x_ref, hm_ref, om_ref, yd_ref, oh_ref, gh_ref, mnw_ref, hnw_ref, w_ref, nw_ref, o_ref):
    r2 = _head_of(lax.broadcasted_iota(jnp.int32, (M_W, M_W), 0))
    c2 = _head_of(lax.broadcasted_iota(jnp.int32, (M_W, M_W), 1))
    ones_bd = (r2 == c2).astype(BF16)
    inv = 1.0 / HEAD_W

    hm = hm_ref[...]
    xc = hm - _dot01_rhs(hm, ones_bd, 2) * inv
    var = _dot01_rhs(xc * xc, ones_bd, 2) * inv
    ym = xc * lax.rsqrt(var + EPS) * mnw_ref[...] * _sigmoid(om_ref[...])

    oh = oh_ref[...]
    ms = _dot01_rhs(oh * oh, ones_bd, 2) * inv
    gh = gh_ref[...]
    yh = oh * lax.rsqrt(ms + EPS) * hnw_ref[...] * (gh * _sigmoid(gh))

    w = w_ref[...]
    acc = jnp.dot(ym.astype(BF16), w[0:M_W], preferred_element_type=F32)
    acc = acc + jnp.dot(yd_ref[...].astype(BF16), w[M_W:M_W + DF_W], preferred_element_type=F32)
    acc = acc + jnp.dot(yh.astype(BF16), w[M_W + DF_W:], preferred_element_type=F32)
    o_ref[...] = x_ref[...] + _rms(acc, nw_ref[...])


def _mix_out(x, hm, om, yd, oh, gh, mnw, hnw, w_out, nw, layer, *, tm):
    m, d = x.shape
    tok = lambda width: pl.BlockSpec((tm, width), lambda i: (i, 0))
    lay = lambda *shape: pl.BlockSpec((None,) + shape, lambda i: (layer,) + (0,) * len(shape))
    return pl.pallas_call(
        _mix_out_kernel,
        grid=(m // tm,),
        in_specs=[tok(d), tok(M_W), tok(M_W), tok(DF_W), tok(M_W), tok(M_W),
                  lay(1, M_W), lay(1, M_W), lay(d, d),
                  pl.BlockSpec((None, None, 1, d), lambda i: (layer, 3, 0, 0))],
        out_specs=tok(d),
        out_shape=jax.ShapeDtypeStruct((m, d), F32),
        compiler_params=_cparams(("parallel",)),
        name="mix_out",
    )(x, hm, om, yd, oh, gh, mnw, hnw, w_out, nw)


def _tiles(batch, seq):
    return dict(
        ffn_tm=min(512, seq), proj_tm=min(256, seq), mlstm_chunk=min(256, seq), hgrn_tile=min(256, seq),
        attn_tile=min(512, seq), out_tm=min(512, seq))


def _ffn_tf(dff):
    return dff // 2 if (dff // 2) % 128 == 0 else dff


def kernel(x, norm_w, ffn1_wi, ffn1_wo, ffn2_wi, ffn2_wo, w_in, w_out, mlstm_conv_w, mlstm_conv_b, mlstm_igate_b,
           mlstm_fgate_b, mlstm_norm_w, diff_lambda, diff_norm_w, rel_bias, hgrn_lb_logits, hgrn_norm_w):
    batch, seq, d = x.shape
    depth = norm_w.shape[0]
    tl = _tiles(batch, seq)
    tf = _ffn_tf(ffn1_wo.shape[1])
    act_dtype = BF16

    off = np.concatenate([[0], np.cumsum(SPLIT_SIZES)])
    seg = lambda i: np.arange(off[i], off[i + 1])
    order = np.concatenate([seg(0), seg(1), seg(2), seg(5), seg(6), seg(8), seg(9), seg(10), seg(11),
                            seg(3), seg(4)])
    w_main = jnp.take(w_in, jnp.asarray(order), axis=2)
    w_proj = jnp.pad(w_main, ((0, 0), (0, 0), (0, GATE_PAD - 2 * HEADS))).astype(BF16)
    w_vd_t = jnp.swapaxes(w_in[:, :, off[7]:off[8]], 1, 2).astype(BF16)
    w_gate_t = jnp.swapaxes(w_in[:, :, off[3]:off[5]], 1, 2).astype(BF16)
    gate_b = jnp.concatenate([mlstm_igate_b, mlstm_fgate_b], axis=1).astype(F32)
    gate_b_col = jnp.pad(gate_b, ((0, 0), (0, GATE_PAD - 2 * HEADS)))[:, None, :]
    gate_b_row = gate_b[:, :, None]
    wi1, wo1 = ffn1_wi.astype(BF16), ffn1_wo.astype(BF16)
    wi2, wo2 = ffn2_wi.astype(BF16), ffn2_wo.astype(BF16)
    w_o = w_out.astype(BF16)
    nw = norm_w.astype(F32)[:, :, None, :]
    conv_w = mlstm_conv_w.astype(F32)
    conv_b = mlstm_conv_b.astype(F32)[:, None, :]
    mnw = mlstm_norm_w.astype(F32)[:, None, :]
    hnw = hgrn_norm_w.astype(F32)[:, None, :]
    dnw_t = diff_norm_w.astype(F32)[:, :, None]
    lam_vecs = diff_lambda.astype(F32)

    lb_all = _hgrn_lower_bounds(hgrn_lb_logits)[:, None, :]
    bias_tiles = _rel_bias_tiles(rel_bias, tl["attn_tile"])
    c31 = rel_bias[REL_BUCKETS - 1].astype(F32)

    xf = x.reshape(batch * seq, d)
    for l in range(depth):
        lam0 = jnp.full((1,), 0.8 - 0.6 * math.exp(-0.3 * l), F32)
        xf = _ffn(xf, nw, l, 0, 1, wi1, wo1, tm=tl["ffn_tm"], tf=tf)
        (qm, km, vm, om, gc, gr, qd, kd, vdt, qh, lfh, kh, vh, gh) = _proj(
            xf, nw, l, w_proj, w_vd_t, w_gate_t, conv_w, conv_b, gate_b_col, gate_b_row, lb_all,
            batch=batch, tm=tl["proj_tm"], act_dtype=act_dtype)
        hm = _mlstm(qm, km, vm, gc, gr, batch=batch, chunk=tl["mlstm_chunk"])
        yd = _diff_attn(qd, kd, vdt, bias_tiles, c31, lam0, lam_vecs, dnw_t, l, batch=batch, tile=tl["attn_tile"])
        oh = _hgrn(qh, kh, vh, lfh, batch=batch, tile=tl["hgrn_tile"])
        xf = _mix_out(xf, hm, om, yd, oh, gh, mnw, hnw, w_o, nw, l, tm=tl["out_tm"])
        xf = _ffn(xf, nw, l, 4, 5, wi2, wo2, tm=tl["ffn_tm"], tf=tf)
    return xf.reshape(batch, seq, d)
```

```python
import functools
import math

import numpy as np
import jax
import jax.numpy as jnp
from jax import lax
from jax.experimental import pallas as pl
from jax.experimental.pallas import tpu as pltpu

F32 = jnp.float32
BF16 = jnp.bfloat16
EPS = 1e-6
NEG = -1e30
LOG2E = 1.4426950408889634

HEADS = 4
HEAD_W = 64
M_W = HEADS * HEAD_W
DF_HW = 128
DF_W = HEADS * DF_HW
CONV_W = 4
REL_BUCKETS = 32
REL_MAX_EXACT = 16
REL_MAX_DIST = 128
GATE_PAD = 128
SPLIT_SIZES = (2 * M_W, M_W, M_W, HEADS, HEADS, DF_W, DF_W, DF_W, M_W, M_W, M_W, M_W)
VMEM_LIMIT = 56 * 1024 * 1024


def _cparams(sem):
    return pltpu.CompilerParams(dimension_semantics=sem, vmem_limit_bytes=VMEM_LIMIT)


def _sigmoid(x):
    return 1.0 / (1.0 + jnp.exp(-x))


def _log_sigmoid(x):
    return jnp.minimum(x, 0.0) - jnp.log1p(jnp.exp(-jnp.abs(x)))


def _rms(x, w):
    return x * lax.rsqrt(jnp.mean(x * x, axis=-1, keepdims=True) + EPS) * w


def _split_bf16(x, parts):
    out = []
    r = x
    for _ in range(parts):
        p = r.astype(BF16)
        out.append(p)
        r = r - p.astype(F32)
    return out


def _dot01_rhs(x, m01, parts):
    acc = None
    for p in _split_bf16(x, parts):
        t = jnp.dot(p, m01, preferred_element_type=F32)
        acc = t if acc is None else acc + t
    return acc


def _dot01_lhs(m01, x, parts):
    acc = None
    for p in _split_bf16(x, parts):
        t = jnp.dot(m01, p, preferred_element_type=F32)
        acc = t if acc is None else acc + t
    return acc


def _head_of(idx):
    return lax.shift_right_logical(idx, 6)


def _lb_kernel(lg_ref, o_ref):
    lg = lg_ref[...]
    e = jnp.exp(lg - jnp.max(lg, axis=0, keepdims=True))
    sm = e / jnp.sum(e, axis=0, keepdims=True)
    depth = lg.shape[0]
    rows = []
    run = sm[0:1]
    first = run
    for i in range(depth):
        if i > 0:
            run = run + sm[i:i + 1]
        rows.append(jnp.maximum(run - first, 0.0))
    o_ref[...] = jnp.concatenate(rows, axis=0)


def _hgrn_lower_bounds(logits):
    return pl.pallas_call(
        _lb_kernel, out_shape=jax.ShapeDtypeStruct(logits.shape, F32), name="hgrn_lb")(logits.astype(F32))


def _bias_kernel(tab_ref, o_ref, *, tile):
    h = pl.program_id(0)
    typ = pl.program_id(1)
    r = lax.broadcasted_iota(jnp.int32, (tile, tile), 0)
    c = lax.broadcasted_iota(jnp.int32, (tile, tile), 1)
    rel = c - r + typ * tile
    n = jnp.maximum(rel, 0)
    nf = jnp.maximum(n, 1).astype(F32)
    large = REL_MAX_EXACT + (jnp.log(nf / REL_MAX_EXACT) / math.log(REL_MAX_DIST / REL_MAX_EXACT)
                             * (REL_BUCKETS - REL_MAX_EXACT)).astype(jnp.int32)
    large = jnp.minimum(large, REL_BUCKETS - 1)
    bucket = jnp.where(n < REL_MAX_EXACT, n, large)
    bias = jnp.zeros((tile, tile), F32)
    for b in range(REL_BUCKETS):
        bias = jnp.where(bucket == b, tab_ref[b, h], bias)
    o_ref[...] = jnp.where(rel >= 0, bias * LOG2E, NEG)


def _rel_bias_tiles(rel_bias, tile):
    return pl.pallas_call(
        functools.partial(_bias_kernel, tile=tile),
        grid=(HEADS, 2),
        in_specs=[pl.BlockSpec(memory_space=pltpu.SMEM)],
        out_specs=pl.BlockSpec((None, None, tile, tile), lambda h, t: (h, t, 0, 0)),
        out_shape=jax.ShapeDtypeStruct((HEADS, 2, tile, tile), F32),
        name="rel_bias_tiles",
    )(rel_bias.astype(F32))


def _ffn_kernel(x_ref, nwi_ref, wi_ref, wo_ref, nwo_ref, o_ref, xn_ref, acc_ref):
    j = pl.program_id(1)

    @pl.when(j == 0)
    def _():
        xn_ref[...] = _rms(x_ref[...], nwi_ref[...]).astype(BF16)
        acc_ref[...] = jnp.zeros_like(acc_ref)

    nf = wo_ref.shape[0]
    xn = xn_ref[...]
    g = jnp.dot(xn, wi_ref[j], preferred_element_type=F32)
    u = jnp.dot(xn, wi_ref[nf + j], preferred_element_type=F32)
    a = (g * _sigmoid(g) * u).astype(BF16)
    acc_ref[...] += jnp.dot(a, wo_ref[j], preferred_element_type=F32)

    @pl.when(j == nf - 1)
    def _():
        o_ref[...] = x_ref[...] + 0.5 * _rms(acc_ref[...], nwo_ref[...])


def _ffn(x, nw, layer, row_in, row_out, wi, wo, *, tm):
    m, d = x.shape
    nf, tf = wo.shape[1], wo.shape[2]
    resident = pl.Buffered(1)
    return pl.pallas_call(
        _ffn_kernel,
        grid=(m // tm, nf),
        in_specs=[
            pl.BlockSpec((tm, d), lambda i, j: (i, 0)),
            pl.BlockSpec((None, None, 1, d), lambda i, j: (layer, row_in, 0, 0)),
            pl.BlockSpec((None, 2 * nf, d, tf), lambda i, j: (layer, 0, 0, 0), pipeline_mode=resident),
            pl.BlockSpec((None, nf, tf, d), lambda i, j: (layer, 0, 0, 0), pipeline_mode=resident),
            pl.BlockSpec((None, None, 1, d), lambda i, j: (layer, row_out, 0, 0)),
        ],
        out_specs=pl.BlockSpec((tm, d), lambda i, j: (i, 0)),
        out_shape=jax.ShapeDtypeStruct((m, d), F32),
        scratch_shapes=[pltpu.VMEM((tm, d), BF16), pltpu.VMEM((tm, d), F32)],
        compiler_params=_cparams(("parallel", "arbitrary")),
        name="ffn",
    )(x, nw, wi, wo, nw)


def _proj_kernel(x_ref, nw_ref, w_ref, wvt_ref, wgt_ref, cw_ref, cb_ref, gbc_ref, gbr_ref, lb_ref,
                 qm_ref, km_ref, vm_ref, om_ref, gc_ref, gr_ref, qd_ref, kd_ref, vdt_ref,
                 qh_ref, lfh_ref, kh_ref, vh_ref, gh_ref, cbuf_ref):
    t = pl.program_id(1)
    tm = x_ref.shape[0]
    xn = _rms(x_ref[...], nw_ref[...]).astype(BF16)
    u = jnp.dot(xn, w_ref[...], preferred_element_type=F32)

    @pl.when(t == 0)
    def _():
        cbuf_ref[0:8, :] = jnp.zeros((8, 2 * M_W), F32)

    qk = u[:, 0:2 * M_W]
    cbuf_ref[8:8 + tm, :] = qk
    cw = cw_ref[...]
    y = cb_ref[...] + cw[CONV_W - 1:CONV_W] * qk
    for d in range(1, CONV_W):
        y = y + cw[CONV_W - 1 - d:CONV_W - d] * cbuf_ref[8 - d:8 - d + tm, :]
    cbuf_ref[0:8, :] = cbuf_ref[tm:tm + 8, :]
    y = y * _sigmoid(y)
    qm_ref[...] = y[:, 0:M_W].astype(qm_ref.dtype)
    km_ref[...] = (y[:, M_W:2 * M_W] * (HEAD_W ** -0.5)).astype(km_ref.dtype)
    vm_ref[...] = u[:, 512:768].astype(vm_ref.dtype)
    om_ref[...] = u[:, 768:1024]

    qd_ref[...] = (u[:, 1024:1536] * (HEAD_W ** -0.5 * LOG2E)).astype(qd_ref.dtype)
    kd_ref[...] = u[:, 1536:2048].astype(kd_ref.dtype)
    vdt_ref[...] = lax.dot_general(wvt_ref[...], xn, (((1,), (1,)), ((), ())),
                                   preferred_element_type=F32).astype(vdt_ref.dtype)

    qh = u[:, 2048:2304]
    qh_ref[...] = (qh * _sigmoid(qh)).astype(qh_ref.dtype)
    fp = u[:, 2304:2560]
    lb = lb_ref[...]
    a = jnp.log(lb)
    bb = jnp.log1p(-lb) + _log_sigmoid(fp)
    lfh_ref[...] = jnp.maximum(a, bb) + jnp.log1p(jnp.exp(-jnp.abs(a - bb)))
    kh_ref[...] = ((1.0 - lb) * _sigmoid(-fp)).astype(kh_ref.dtype)
    vh_ref[...] = u[:, 2560:2816].astype(vh_ref.dtype)
    gh_ref[...] = u[:, 2816:3072]

    zc = u[:, 3072:3072 + GATE_PAD] + gbc_ref[...]
    lane = lax.broadcasted_iota(jnp.int32, (1, GATE_PAD), 1)
    gc_ref[...] = jnp.where(lane < HEADS, zc, _log_sigmoid(zc))
    zr = lax.dot_general(wgt_ref[...], xn, (((1,), (1,)), ((), ())), preferred_element_type=F32) + gbr_ref[...]
    row = lax.broadcasted_iota(jnp.int32, (2 * HEADS, 1), 0)
    gr_ref[...] = jnp.where(row < HEADS, zr, _log_sigmoid(zr))


def _proj(x, nw, layer, w, wvt, wgt, cw, cb, gbc, gbr, lb, *, batch, tm, act_dtype):
    m, d = x.shape
    seq = m // batch
    nt = seq // tm
    npc = w.shape[-1]
    tok = lambda width: pl.BlockSpec((tm, width), lambda b, t: (b * nt + t, 0))
    lay = lambda *shape: pl.BlockSpec((None,) + shape, lambda b, t: (layer,) + (0,) * len(shape))
    sds = lambda width, dt: jax.ShapeDtypeStruct((m, width), dt)
    out_shape = [sds(M_W, act_dtype), sds(M_W, act_dtype), sds(M_W, act_dtype), sds(M_W, F32),
                 sds(GATE_PAD, F32), jax.ShapeDtypeStruct((batch, 2 * HEADS, seq), F32),
                 sds(DF_W, act_dtype), sds(DF_W, act_dtype),
                 jax.ShapeDtypeStruct((batch, nt, DF_W, tm), act_dtype),
                 sds(M_W, act_dtype), sds(M_W, F32), sds(M_W, act_dtype), sds(M_W, act_dtype), sds(M_W, F32)]
    out_specs = [tok(M_W), tok(M_W), tok(M_W), tok(M_W), tok(GATE_PAD),
                 pl.BlockSpec((None, 2 * HEADS, tm), lambda b, t: (b, 0, t)),
                 tok(DF_W), tok(DF_W),
                 pl.BlockSpec((None, None, DF_W, tm), lambda b, t: (b, t, 0, 0)),
                 tok(M_W), tok(M_W), tok(M_W), tok(M_W), tok(M_W)]
    return pl.pallas_call(
        _proj_kernel,
        grid=(batch, nt),
        in_specs=[tok(d),
                  pl.BlockSpec((None, None, 1, d), lambda b, t: (layer, 2, 0, 0)),
                  lay(d, npc), lay(DF_W, d), lay(2 * HEADS, d), lay(CONV_W, 2 * M_W), lay(1, 2 * M_W),
                  lay(1, GATE_PAD), lay(2 * HEADS, 1), lay(1, M_W)],
        out_specs=out_specs,
        out_shape=out_shape,
        scratch_shapes=[pltpu.VMEM((tm + 8, 2 * M_W), F32)],
        compiler_params=_cparams(("parallel", "arbitrary")),
        name="mixer_proj",
    )(x, nw, w, wvt, wgt, cw, cb, gbc, gbr, lb)


def _mlstm_kernel(q_ref, k_ref, v_ref, gc_ref, gr_ref, h_ref, c_ref, n_ref, m_ref):
    @pl.when(pl.program_id(1) == 0)
    def _():
        c_ref[...] = jnp.zeros_like(c_ref)
        n_ref[...] = jnp.zeros_like(n_ref)
        m_ref[...] = jnp.zeros_like(m_ref)

    L = q_ref.shape[0]
    q = q_ref[...].astype(F32)
    k = k_ref[...].astype(F32)
    vb = v_ref[...].astype(BF16)
    kb = k.astype(BF16)
    gc = gc_ref[...]
    gr = gr_ref[...]
    row = lax.broadcasted_iota(jnp.int32, (L, L), 0)
    col = lax.broadcasted_iota(jnp.int32, (L, L), 1)
    causal = row >= col
    bcol = _dot01_rhs_t(causal, gc)
    brow = _dot01_rhs(gr, (row <= col).astype(BF16), 3)
    lane_head = _head_of(lax.broadcasted_iota(jnp.int32, (1, M_W), 1))
    cbd = c_ref[...]
    nrow = n_ref[...]
    qc = jnp.dot(q.astype(BF16), cbd.astype(BF16), preferred_element_type=F32)
    hout = jnp.zeros((L, M_W), F32)
    wa_all = jnp.zeros((L, M_W), F32)
    dec_all = jnp.zeros((1, M_W), F32)
    for h in range(HEADS):
        hm = lane_head == h
        bt = bcol[:, HEADS + h:HEADS + h + 1]
        li_c = gc[:, h:h + 1]
        rs = gr[h:h + 1, :] - brow[HEADS + h:HEADS + h + 1, :]
        mprev = m_ref[h:h + 1, 0:1]
        dmat = jnp.where(causal, bt + rs, -jnp.inf)
        m_inter = bt + mprev
        mt = jnp.maximum(jnp.max(dmat, axis=1, keepdims=True), m_inter)
        w = jnp.exp(dmat - mt)
        qh = jnp.where(hm, q, 0.0).astype(BF16)
        sc = lax.dot_general(qh, kb, (((1,), (1,)), ((), ())), preferred_element_type=F32) * w
        g = jnp.exp(m_inter - mt)
        pv = jnp.dot(sc.astype(BF16), vb, preferred_element_type=F32)
        qn = jnp.sum(jnp.where(hm, q * nrow, 0.0), axis=1, keepdims=True)
        den = jnp.sum(sc, axis=1, keepdims=True) + g * qn
        scale = 1.0 / jnp.maximum(jnp.abs(den), jnp.exp(-mt))
        hout = jnp.where(hm, (pv + g * qc) * scale, hout)
        bl = bt[L - 1:L, :]
        a = bl - bt + li_c
        mnew = jnp.maximum(bl + mprev, jnp.max(a, axis=0, keepdims=True))
        wa_all = jnp.where(hm, jnp.exp(a - mnew), wa_all)
        dec_all = jnp.where(hm, jnp.exp(bl + mprev - mnew), dec_all)
        m_ref[h:h + 1, :] = jnp.broadcast_to(mnew, (1, m_ref.shape[1]))
    h_ref[...] = hout
    kw = k * wa_all
    cnew = lax.dot_general(kw.astype(BF16), vb, (((0,), (0,)), ((), ())), preferred_element_type=F32)
    r2 = _head_of(lax.broadcasted_iota(jnp.int32, (M_W, M_W), 0))
    c2 = _head_of(lax.broadcasted_iota(jnp.int32, (M_W, M_W), 1))
    c_ref[...] = dec_all * cbd + jnp.where(r2 == c2, cnew, 0.0)
    n_ref[...] = dec_all * nrow + jnp.sum(kw, axis=0, keepdims=True)


def _dot01_rhs_t(mask, x):
    return _dot01_lhs(mask.astype(BF16), x, 3)


def _mlstm(qm, km, vm, gc, gr, *, batch, chunk):
    m = qm.shape[0]
    seq = m // batch
    nc = seq // chunk
    tok = lambda width: pl.BlockSpec((chunk, width), lambda b, c: (b * nc + c, 0))
    return pl.pallas_call(
        _mlstm_kernel,
        grid=(batch, nc),
        in_specs=[tok(M_W), tok(M_W), tok(M_W), tok(GATE_PAD),
                  pl.BlockSpec((None, 2 * HEADS, chunk), lambda b, c: (b, 0, c))],
        out_specs=tok(M_W),
        out_shape=jax.ShapeDtypeStruct((m, M_W), F32),
        scratch_shapes=[pltpu.VMEM((M_W, M_W), F32), pltpu.VMEM((1, M_W), F32), pltpu.VMEM((8, 128), F32)],
        compiler_params=_cparams(("parallel", "arbitrary")),
        name="mlstm",
    )(qm, km, vm, gc, gr)


HG_BLK = 16


def _hgrn_kernel(q_ref, k_ref, v_ref, lf_ref, o_ref, st_ref):
    @pl.when(pl.program_id(1) == 0)
    def _():
        st_ref[...] = jnp.zeros_like(st_ref)

    nb = q_ref.shape[0]
    nblk = q_ref.shape[1] // HG_BLK
    r16 = lax.broadcasted_iota(jnp.int32, (HG_BLK, HG_BLK), 0)
    c16 = lax.broadcasted_iota(jnp.int32, (HG_BLK, HG_BLK), 1)
    tri16 = (r16 >= c16).astype(BF16)
    rowid = lax.broadcasted_iota(jnp.int32, (HG_BLK, M_W), 0)
    r2 = _head_of(lax.broadcasted_iota(jnp.int32, (M_W, M_W), 0))
    c2 = _head_of(lax.broadcasted_iota(jnp.int32, (M_W, M_W), 1))
    same_head = r2 == c2
    ones_bd = same_head.astype(BF16)

    def block(i, sq):
        r0 = pl.multiple_of(i * HG_BLK, HG_BLK)
        q = q_ref[sq, pl.ds(r0, HG_BLK), :].astype(F32)
        k = k_ref[sq, pl.ds(r0, HG_BLK), :].astype(F32)
        v = v_ref[sq, pl.ds(r0, HG_BLK), :].astype(F32)
        lf = lf_ref[sq, pl.ds(r0, HG_BLK), :]
        b = _dot01_lhs(tri16, lf, 3)
        bl = b[HG_BLK - 1:HG_BLK, :]
        st = st_ref[sq]
        o = lax.dot_general((q * jnp.exp(b)).astype(BF16), st.astype(BF16), (((1,), (1,)), ((), ())),
                            preferred_element_type=F32)
        ps = []
        for s in range(HG_BLK):
            dd = jnp.where(rowid >= s, b - b[s:s + 1, :], -jnp.inf)
            ps.append(q * k[s:s + 1, :] * jnp.exp(dd))
        p = jnp.concatenate(ps, axis=0).astype(BF16)
        abig = jnp.dot(p, ones_bd, preferred_element_type=F32)
        for s in range(HG_BLK):
            o = o + abig[s * HG_BLK:(s + 1) * HG_BLK, :] * v[s:s + 1, :]
        o_ref[sq, pl.ds(r0, HG_BLK), :] = o
        ke = (k * jnp.exp(bl - b)).astype(BF16)
        upd = lax.dot_general(v.astype(BF16), ke, (((0,), (0,)), ((), ())), preferred_element_type=F32)
        st_ref[sq] = st * jnp.exp(bl) + jnp.where(same_head, upd, 0.0)

    def body(i, carry):
        for sq in range(nb):
            block(i, sq)
        return carry

    lax.fori_loop(0, nblk, body, 0)


def _hgrn(qh, kh, vh, lfh, *, batch, tile, nb):
    m = qh.shape[0]
    seq = m // batch
    view = lambda a: a.reshape(batch, seq, M_W)
    tok = pl.BlockSpec((nb, tile, M_W), lambda b, t: (b, t, 0))
    out = pl.pallas_call(
        _hgrn_kernel,
        grid=(batch // nb, seq // tile),
        in_specs=[tok, tok, tok, tok],
        out_specs=tok,
        out_shape=jax.ShapeDtypeStruct((batch, seq, M_W), F32),
        scratch_shapes=[pltpu.VMEM((nb, M_W, M_W), F32)],
        compiler_params=_cparams(("parallel", "arbitrary")),
        name="hgrn2",
    )(view(qh), view(kh), view(vh), view(lfh))
    return out.reshape(m, M_W)


ONES_ROWS = 16


def _attn_kernel(q_ref, k_ref, vt_ref, bias_ref, c31_ref, lam0_ref, lv_ref, nwt_ref, o_ref, m_ref, acc_ref, *, tile):
    h = pl.program_id(1)
    qi = pl.program_id(2)
    vchunk = vt_ref.shape[2]
    nvc = tile // vchunk
    m_ref[...] = jnp.full_like(m_ref, NEG)
    acc_ref[...] = jnp.zeros_like(acc_ref)
    q = q_ref[...]
    first = lax.broadcasted_iota(jnp.int32, (1, DF_HW), 1) < HEAD_W
    zero = jnp.zeros_like(q)
    qs = (jnp.where(first, q, zero), jnp.where(first, zero, q))
    ones = jnp.ones((ONES_ROWS, vchunk), BF16)
    far_bias = c31_ref[h] * LOG2E

    def tile_step(ki, kind):
        kb = k_ref[pl.ds(pl.multiple_of(ki * tile, tile), tile), :]
        vts = [jnp.concatenate([vt_ref[ki * nvc + c], ones], axis=0) for c in range(nvc)]
        for j in range(2):
            st = lax.dot_general(kb, qs[j], (((1,), (1,)), ((), ())), preferred_element_type=F32)
            if kind == "sub":
                st = st + bias_ref[1]
            elif kind == "diag":
                st = st + bias_ref[0]
            mcur = jnp.max(st, axis=0, keepdims=True)
            if kind == "far":
                mcur = mcur + far_bias
            m_old = m_ref[j]
            m_new = jnp.maximum(m_old, mcur)
            alpha = jnp.exp2(m_old - m_new)
            shift = m_new - far_bias if kind == "far" else m_new
            pt = jnp.exp2((st - shift).astype(BF16))
            pv = None
            for c in range(nvc):
                t = jnp.dot(vts[c], pt[c * vchunk:(c + 1) * vchunk, :], preferred_element_type=F32)
                pv = t if pv is None else pv + t
            acc_ref[j] = alpha * acc_ref[j] + pv
            m_ref[j] = m_new

    def far_body(ki, carry):
        tile_step(ki, "far")
        return carry

    lax.fori_loop(0, qi - 1, far_body, 0)

    @pl.when(qi >= 1)
    def _():
        tile_step(qi - 1, "sub")

    tile_step(qi, "diag")

    lv = lv_ref[...]
    lam0 = lam0_ref[0]
    lam = (jnp.exp(jnp.sum(lv[0:1] * lv[1:2], axis=1, keepdims=True))
           - jnp.exp(jnp.sum(lv[2:3] * lv[3:4], axis=1, keepdims=True)) + lam0)
    a1 = acc_ref[0]
    a2 = acc_ref[1]
    od = a1[0:DF_HW] / a1[DF_HW:DF_HW + 1] - lam * (a2[0:DF_HW] / a2[DF_HW:DF_HW + 1])
    ms = jnp.mean(od * od, axis=0, keepdims=True)
    yt = od * lax.rsqrt(ms + EPS) * nwt_ref[...] * (1.0 - lam0)
    o_ref[...] = yt.T.astype(o_ref.dtype)


def _diff_attn(qd, kd, vdt, bias_tiles, c31, lam0, lam_vecs, norm_w_t, layer, *, batch, tile):
    m = qd.shape[0]
    seq = m // batch
    nq = seq // tile
    nchunk, vchunk = vdt.shape[1], vdt.shape[3]
    return pl.pallas_call(
        functools.partial(_attn_kernel, tile=tile),
        grid=(batch, HEADS, nq),
        in_specs=[
            pl.BlockSpec((tile, DF_HW), lambda b, h, i: (b * nq + i, h)),
            pl.BlockSpec((seq, DF_HW), lambda b, h, i: (b, h)),
            pl.BlockSpec((None, nchunk, DF_HW, vchunk), lambda b, h, i: (b, 0, h, 0)),
            pl.BlockSpec((None, 2, tile, tile), lambda b, h, i: (h, 0, 0, 0)),
            pl.BlockSpec(memory_space=pltpu.SMEM),
            pl.BlockSpec(memory_space=pltpu.SMEM),
            pl.BlockSpec((None, 4, HEAD_W), lambda b, h, i: (layer, 0, 0)),
            pl.BlockSpec((None, DF_HW, 1), lambda b, h, i: (layer, h, 0)),
        ],
        out_specs=pl.BlockSpec((tile, DF_HW), lambda b, h, i: (b * nq + i, h)),
        out_shape=jax.ShapeDtypeStruct((m, DF_W), BF16),
        scratch_shapes=[pltpu.VMEM((2, 1, tile), F32), pltpu.VMEM((2, DF_HW + ONES_ROWS, tile), F32)],
        compiler_params=_cparams(("parallel", "parallel", "arbitrary")),
        name="diff_attn",
    )(qd, kd, vdt, bias_tiles, c31, lam0, lam_vecs, norm_w_t)


def _mix_out_kernel(x_ref, hm_ref, om_ref, yd_ref, oh_ref, gh_ref, mnw_ref, hnw_ref, w_ref, nw_ref, o_ref):
    r2 = _head_of(lax.broadcasted_iota(jnp.int32, (M_W, M_W), 0))
    c2 = _head_of(lax.broadcasted_iota(jnp.int32, (M_W, M_W), 1))
    ones_bd = (r2 == c2).astype(BF16)
    inv = 1.0 / HEAD_W

    hm = hm_ref[...]
    xc = hm - _dot01_rhs(hm, ones_bd, 2) * inv
    var = _dot01_rhs(xc * xc, ones_bd, 2) * inv
    ym = xc * lax.rsqrt(var + EPS) * mnw_ref[...] * _sigmoid(om_ref[...])

    oh = oh_ref[...]
    ms = _dot01_rhs(oh * oh, ones_bd, 2) * inv
    gh = gh_ref[...]
    yh = oh * lax.rsqrt(ms + EPS) * hnw_ref[...] * (gh * _sigmoid(gh))

    w = w_ref[...]
    acc = jnp.dot(ym.astype(BF16), w[0:M_W], preferred_element_type=F32)
    acc = acc + jnp.dot(yd_ref[...].astype(BF16), w[M_W:M_W + DF_W], preferred_element_type=F32)
    acc = acc + jnp.dot(yh.astype(BF16), w[M_W + DF_W:], preferred_element_type=F32)
    o_ref[...] = x_ref[...] + _rms(acc, nw_ref[...])


def _mix_out(x, hm, om, yd, oh, gh, mnw, hnw, w_out, nw, layer, *, tm):
    m, d = x.shape
    tok = lambda width: pl.BlockSpec((tm, width), lambda i: (i, 0))
    lay = lambda *shape: pl.BlockSpec((None,) + shape, lambda i: (layer,) + (0,) * len(shape))
    return pl.pallas_call(
        _mix_out_kernel,
        grid=(m // tm,),
        in_specs=[tok(d), tok(M_W), tok(M_W), tok(DF_W), tok(M_W), tok(M_W),
                  lay(1, M_W), lay(1, M_W), lay(d, d),
                  pl.BlockSpec((None, None, 1, d), lambda i: (layer, 3, 0, 0))],
        out_specs=tok(d),
        out_shape=jax.ShapeDtypeStruct((m, d), F32),
        compiler_params=_cparams(("parallel",)),
        name="mix_out",
    )(x, hm, om, yd, oh, gh, mnw, hnw, w_out, nw)


def _tiles(batch, seq):
    return dict(
        ffn_tm=min(512, seq), proj_tm=min(256, seq), mlstm_chunk=min(256, seq), hgrn_tile=min(256, seq),
        attn_tile=min(512, seq), out_tm=min(512, seq), hgrn_nb=4 if batch % 4 == 0 else 1)


def _ffn_tf(dff):
    return dff // 2 if (dff // 2) % 128 == 0 else dff


def kernel(x, norm_w, ffn1_wi, ffn1_wo, ffn2_wi, ffn2_wo, w_in, w_out, mlstm_conv_w, mlstm_conv_b, mlstm_igate_b,
           mlstm_fgate_b, mlstm_norm_w, diff_lambda, diff_norm_w, rel_bias, hgrn_lb_logits, hgrn_norm_w):
    batch, seq, d = x.shape
    depth = norm_w.shape[0]
    tl = _tiles(batch, seq)
    tf = _ffn_tf(ffn1_wo.shape[1])
    act_dtype = BF16

    off = np.concatenate([[0], np.cumsum(SPLIT_SIZES)])
    seg = lambda i: np.arange(off[i], off[i + 1])
    order = np.concatenate([seg(0), seg(1), seg(2), seg(5), seg(6), seg(8), seg(9), seg(10), seg(11),
                            seg(3), seg(4)])
    w_main = jnp.take(w_in, jnp.asarray(order), axis=2)
    w_proj = jnp.pad(w_main, ((0, 0), (0, 0), (0, GATE_PAD - 2 * HEADS))).astype(BF16)
    w_vd_t = jnp.swapaxes(w_in[:, :, off[7]:off[8]], 1, 2).astype(BF16)
    w_gate_t = jnp.swapaxes(w_in[:, :, off[3]:off[5]], 1, 2).astype(BF16)
    gate_b = jnp.concatenate([mlstm_igate_b, mlstm_fgate_b], axis=1).astype(F32)
    gate_b_col = jnp.pad(gate_b, ((0, 0), (0, GATE_PAD - 2 * HEADS)))[:, None, :]
    gate_b_row = gate_b[:, :, None]
    dff = ffn1_wo.shape[1]
    nf = dff // tf
    chunk_wi = lambda w: jnp.swapaxes(w.astype(BF16).reshape(depth, d, 2 * nf, tf), 1, 2)
    chunk_wo = lambda w: w.astype(BF16).reshape(depth, nf, tf, d)
    wi1, wo1 = chunk_wi(ffn1_wi), chunk_wo(ffn1_wo)
    wi2, wo2 = chunk_wi(ffn2_wi), chunk_wo(ffn2_wo)
    w_o = w_out.astype(BF16)
    nw = norm_w.astype(F32)[:, :, None, :]
    conv_w = mlstm_conv_w.astype(F32)
    conv_b = mlstm_conv_b.astype(F32)[:, None, :]
    mnw = mlstm_norm_w.astype(F32)[:, None, :]
    hnw = hgrn_norm_w.astype(F32)[:, None, :]
    dnw_t = diff_norm_w.astype(F32)[:, :, None]
    lam_vecs = diff_lambda.astype(F32)

    lb_all = _hgrn_lower_bounds(hgrn_lb_logits)[:, None, :]
    bias_tiles = _rel_bias_tiles(rel_bias, tl["attn_tile"])
    c31 = rel_bias[REL_BUCKETS - 1].astype(F32)

    xf = x.reshape(batch * seq, d)
    for l in range(depth):
        lam0 = jnp.full((1,), 0.8 - 0.6 * math.exp(-0.3 * l), F32)
        xf = _ffn(xf, nw, l, 0, 1, wi1, wo1, tm=tl["ffn_tm"])
        (qm, km, vm, om, gc, gr, qd, kd, vdt, qh, lfh, kh, vh, gh) = _proj(
            xf, nw, l, w_proj, w_vd_t, w_gate_t, conv_w, conv_b, gate_b_col, gate_b_row, lb_all,
            batch=batch, tm=tl["proj_tm"], act_dtype=act_dtype)
        hm = _mlstm(qm, km, vm, gc, gr, batch=batch, chunk=tl["mlstm_chunk"])
        yd = _diff_attn(qd, kd, vdt, bias_tiles, c31, lam0, lam_vecs, dnw_t, l, batch=batch, tile=tl["attn_tile"])
        oh = _hgrn(qh, kh, vh, lfh, batch=batch, tile=tl["hgrn_tile"], nb=tl["hgrn_nb"])
        xf = _mix_out(xf, hm, om, yd, oh, gh, mnw, hnw, w_o, nw, l, tm=tl["out_tm"])
        xf = _ffn(xf, nw, l, 4, 5, wi2, wo2, tm=tl["ffn_tm"])
    return xf.reshape(batch, seq, d)
```

```python
import functools
import math

import numpy as np
import jax
import jax.numpy as jnp
from jax import lax
from jax.experimental import pallas as pl
from jax.experimental.pallas import tpu as pltpu

F32 = jnp.float32
BF16 = jnp.bfloat16
EPS = 1e-6
NEG = -1e30
LOG2E = 1.4426950408889634

HEADS = 4
HEAD_W = 64
M_W = HEADS * HEAD_W
DF_HW = 128
DF_W = HEADS * DF_HW
CONV_W = 4
REL_BUCKETS = 32
REL_MAX_EXACT = 16
REL_MAX_DIST = 128
GATE_PAD = 128
SPLIT_SIZES = (2 * M_W, M_W, M_W, HEADS, HEADS, DF_W, DF_W, DF_W, M_W, M_W, M_W, M_W)
VMEM_LIMIT = 56 * 1024 * 1024


def _cparams(sem):
    return pltpu.CompilerParams(dimension_semantics=sem, vmem_limit_bytes=VMEM_LIMIT)


def _sigmoid(x):
    return 1.0 / (1.0 + jnp.exp(-x))


def _log_sigmoid(x):
    return jnp.minimum(x, 0.0) - jnp.log1p(jnp.exp(-jnp.abs(x)))


def _rms(x, w):
    return x * lax.rsqrt(jnp.mean(x * x, axis=-1, keepdims=True) + EPS) * w


def _split_bf16(x, parts):
    out = []
    r = x
    for _ in range(parts):
        p = r.astype(BF16)
        out.append(p)
        r = r - p.astype(F32)
    return out


def _dot01_rhs(x, m01, parts):
    acc = None
    for p in _split_bf16(x, parts):
        t = jnp.dot(p, m01, preferred_element_type=F32)
        acc = t if acc is None else acc + t
    return acc


def _dot01_lhs(m01, x, parts):
    acc = None
    for p in _split_bf16(x, parts):
        t = jnp.dot(m01, p, preferred_element_type=F32)
        acc = t if acc is None else acc + t
    return acc


def _head_of(idx):
    return lax.shift_right_logical(idx, 6)


def _lb_kernel(lg_ref, o_ref):
    lg = lg_ref[...]
    e = jnp.exp(lg - jnp.max(lg, axis=0, keepdims=True))
    sm = e / jnp.sum(e, axis=0, keepdims=True)
    depth = lg.shape[0]
    rows = []
    run = sm[0:1]
    first = run
    for i in range(depth):
        if i > 0:
            run = run + sm[i:i + 1]
        rows.append(jnp.maximum(run - first, 0.0))
    o_ref[...] = jnp.concatenate(rows, axis=0)


def _hgrn_lower_bounds(logits):
    return pl.pallas_call(
        _lb_kernel, out_shape=jax.ShapeDtypeStruct(logits.shape, F32), name="hgrn_lb")(logits.astype(F32))


def _bias_kernel(tab_ref, o_ref, *, tile):
    h = pl.program_id(0)
    typ = pl.program_id(1)
    r = lax.broadcasted_iota(jnp.int32, (tile, tile), 0)
    c = lax.broadcasted_iota(jnp.int32, (tile, tile), 1)
    rel = c - r + typ * tile
    n = jnp.maximum(rel, 0)
    nf = jnp.maximum(n, 1).astype(F32)
    large = REL_MAX_EXACT + (jnp.log(nf / REL_MAX_EXACT) / math.log(REL_MAX_DIST / REL_MAX_EXACT)
                             * (REL_BUCKETS - REL_MAX_EXACT)).astype(jnp.int32)
    large = jnp.minimum(large, REL_BUCKETS - 1)
    bucket = jnp.where(n < REL_MAX_EXACT, n, large)
    bias = jnp.zeros((tile, tile), F32)
    for b in range(REL_BUCKETS):
        bias = jnp.where(bucket == b, tab_ref[b, h], bias)
    o_ref[...] = jnp.where(rel >= 0, bias * LOG2E, NEG)


def _rel_bias_tiles(rel_bias, tile):
    return pl.pallas_call(
        functools.partial(_bias_kernel, tile=tile),
        grid=(HEADS, 2),
        in_specs=[pl.BlockSpec(memory_space=pltpu.SMEM)],
        out_specs=pl.BlockSpec((None, None, tile, tile), lambda h, t: (h, t, 0, 0)),
        out_shape=jax.ShapeDtypeStruct((HEADS, 2, tile, tile), F32),
        name="rel_bias_tiles",
    )(rel_bias.astype(F32))


def _ffn_kernel(x_ref, nwi_ref, wi_ref, wo_ref, nwo_ref, o_ref, xn_ref, acc_ref):
    j = pl.program_id(1)

    @pl.when(j == 0)
    def _():
        xn_ref[...] = _rms(x_ref[...], nwi_ref[...]).astype(BF16)
        acc_ref[...] = jnp.zeros_like(acc_ref)

    nf = wo_ref.shape[0]
    xn = xn_ref[...]
    g = jnp.dot(xn, wi_ref[j], preferred_element_type=F32)
    u = jnp.dot(xn, wi_ref[nf + j], preferred_element_type=F32)
    a = (g * _sigmoid(g) * u).astype(BF16)
    acc_ref[...] += jnp.dot(a, wo_ref[j], preferred_element_type=F32)

    @pl.when(j == nf - 1)
    def _():
        o_ref[...] = x_ref[...] + 0.5 * _rms(acc_ref[...], nwo_ref[...])


def _ffn(x, nw, layer, row_in, row_out, wi, wo, *, tm):
    m, d = x.shape
    nf, tf = wo.shape[1], wo.shape[2]
    resident = pl.Buffered(1)
    return pl.pallas_call(
        _ffn_kernel,
        grid=(m // tm, nf),
        in_specs=[
            pl.BlockSpec((tm, d), lambda i, j: (i, 0)),
            pl.BlockSpec((None, None, 1, d), lambda i, j: (layer, row_in, 0, 0)),
            pl.BlockSpec((None, 2 * nf, d, tf), lambda i, j: (layer, 0, 0, 0), pipeline_mode=resident),
            pl.BlockSpec((None, nf, tf, d), lambda i, j: (layer, 0, 0, 0), pipeline_mode=resident),
            pl.BlockSpec((None, None, 1, d), lambda i, j: (layer, row_out, 0, 0)),
        ],
        out_specs=pl.BlockSpec((tm, d), lambda i, j: (i, 0)),
        out_shape=jax.ShapeDtypeStruct((m, d), F32),
        scratch_shapes=[pltpu.VMEM((tm, d), BF16), pltpu.VMEM((tm, d), F32)],
        compiler_params=_cparams(("parallel", "arbitrary")),
        name="ffn",
    )(x, nw, wi, wo, nw)


def _proj_kernel(x_ref, nw_ref, w_ref, wvt_ref, wgt_ref, cw_ref, cb_ref, gbc_ref, gbr_ref, lb_ref,
                 qm_ref, km_ref, vm_ref, om_ref, gc_ref, gr_ref, qd_ref, kd_ref, vdt_ref,
                 qh_ref, lfh_ref, kh_ref, vh_ref, gh_ref, cbuf_ref):
    t = pl.program_id(1)
    tm = x_ref.shape[0]
    xn = _rms(x_ref[...], nw_ref[...]).astype(BF16)

    def u(lo, hi):
        return jnp.dot(xn, w_ref[:, lo:hi], preferred_element_type=F32)

    @pl.when(t == 0)
    def _():
        cbuf_ref[0:8, :] = jnp.zeros((8, 2 * M_W), F32)

    qk = u(0, 2 * M_W)
    cbuf_ref[8:8 + tm, :] = qk
    cw = cw_ref[...]
    y = cb_ref[...] + cw[CONV_W - 1:CONV_W] * qk
    for d in range(1, CONV_W):
        y = y + cw[CONV_W - 1 - d:CONV_W - d] * cbuf_ref[8 - d:8 - d + tm, :]
    cbuf_ref[0:8, :] = cbuf_ref[tm:tm + 8, :]
    y = y * _sigmoid(y)
    qm_ref[...] = y[:, 0:M_W].astype(qm_ref.dtype)
    km_ref[...] = (y[:, M_W:2 * M_W] * (HEAD_W ** -0.5)).astype(km_ref.dtype)
    vm_ref[...] = u(512, 768).astype(vm_ref.dtype)
    om_ref[...] = u(768, 1024)

    qd_ref[...] = (u(1024, 1536) * (HEAD_W ** -0.5 * LOG2E)).astype(qd_ref.dtype)
    kd_ref[...] = u(1536, 2048).astype(kd_ref.dtype)
    vdt_ref[...] = lax.dot_general(wvt_ref[...], xn, (((1,), (1,)), ((), ())),
                                   preferred_element_type=F32).astype(vdt_ref.dtype)

    qh = u(2048, 2304)
    qh_ref[...] = (qh * _sigmoid(qh)).astype(qh_ref.dtype)
    fp = u(2304, 2560)
    lb = lb_ref[...]
    a = jnp.log(lb)
    bb = jnp.log1p(-lb) + _log_sigmoid(fp)
    lfh_ref[...] = jnp.maximum(a, bb) + jnp.log1p(jnp.exp(-jnp.abs(a - bb)))
    kh_ref[...] = ((1.0 - lb) * _sigmoid(-fp)).astype(kh_ref.dtype)
    vh_ref[...] = u(2560, 2816).astype(vh_ref.dtype)
    gh_ref[...] = u(2816, 3072)

    zc = u(3072, 3072 + GATE_PAD) + gbc_ref[...]
    lane = lax.broadcasted_iota(jnp.int32, (1, GATE_PAD), 1)
    gc_ref[...] = jnp.where(lane < HEADS, zc, _log_sigmoid(zc))
    zr = lax.dot_general(wgt_ref[...], xn, (((1,), (1,)), ((), ())), preferred_element_type=F32) + gbr_ref[...]
    row = lax.broadcasted_iota(jnp.int32, (2 * HEADS, 1), 0)
    gr_ref[...] = jnp.where(row < HEADS, zr, _log_sigmoid(zr))


def _proj(x, nw, layer, w, wvt, wgt, cw, cb, gbc, gbr, lb, *, batch, tm, act_dtype):
    m, d = x.shape
    seq = m // batch
    nt = seq // tm
    npc = w.shape[-1]
    tok = lambda width: pl.BlockSpec((tm, width), lambda b, t: (b * nt + t, 0))
    lay = lambda *shape: pl.BlockSpec((None,) + shape, lambda b, t: (layer,) + (0,) * len(shape),
                                      pipeline_mode=pl.Buffered(1))
    sds = lambda width, dt: jax.ShapeDtypeStruct((m, width), dt)
    out_shape = [sds(M_W, act_dtype), sds(M_W, act_dtype), sds(M_W, act_dtype), sds(M_W, F32),
                 sds(GATE_PAD, F32), jax.ShapeDtypeStruct((batch, 2 * HEADS, seq), F32),
                 sds(DF_W, act_dtype), sds(DF_W, act_dtype),
                 jax.ShapeDtypeStruct((batch, nt, DF_W, tm), act_dtype),
                 sds(M_W, act_dtype), sds(M_W, F32), sds(M_W, act_dtype), sds(M_W, act_dtype), sds(M_W, F32)]
    out_specs = [tok(M_W), tok(M_W), tok(M_W), tok(M_W), tok(GATE_PAD),
                 pl.BlockSpec((None, 2 * HEADS, tm), lambda b, t: (b, 0, t)),
                 tok(DF_W), tok(DF_W),
                 pl.BlockSpec((None, None, DF_W, tm), lambda b, t: (b, t, 0, 0)),
                 tok(M_W), tok(M_W), tok(M_W), tok(M_W), tok(M_W)]
    return pl.pallas_call(
        _proj_kernel,
        grid=(batch, nt),
        in_specs=[tok(d),
                  pl.BlockSpec((None, None, 1, d), lambda b, t: (layer, 2, 0, 0)),
                  lay(d, npc), lay(DF_W, d), lay(2 * HEADS, d), lay(CONV_W, 2 * M_W), lay(1, 2 * M_W),
                  lay(1, GATE_PAD), lay(2 * HEADS, 1), lay(1, M_W)],
        out_specs=out_specs,
        out_shape=out_shape,
        scratch_shapes=[pltpu.VMEM((tm + 8, 2 * M_W), F32)],
        compiler_params=_cparams(("parallel", "arbitrary")),
        name="mixer_proj",
    )(x, nw, w, wvt, wgt, cw, cb, gbc, gbr, lb)


def _mlstm_kernel(q_ref, k_ref, v_ref, gc_ref, gr_ref, h_ref, c_ref, n_ref, m_ref):
    @pl.when(pl.program_id(1) == 0)
    def _():
        c_ref[...] = jnp.zeros_like(c_ref)
        n_ref[...] = jnp.zeros_like(n_ref)
        m_ref[...] = jnp.zeros_like(m_ref)

    for sq in range(q_ref.shape[0]):
        _mlstm_chunk(*(r.at[sq] for r in (q_ref, k_ref, v_ref, gc_ref, gr_ref, h_ref, c_ref, n_ref, m_ref)))


def _mlstm_chunk(q_ref, k_ref, v_ref, gc_ref, gr_ref, h_ref, c_ref, n_ref, m_ref):
    L = q_ref.shape[0]
    q = q_ref[...].astype(F32)
    k = k_ref[...].astype(F32)
    vb = v_ref[...].astype(BF16)
    kb = k.astype(BF16)
    gc = gc_ref[...]
    gr = gr_ref[...]
    row = lax.broadcasted_iota(jnp.int32, (L, L), 0)
    col = lax.broadcasted_iota(jnp.int32, (L, L), 1)
    causal = row >= col
    bcol = _dot01_rhs_t(causal, gc)
    brow = _dot01_rhs(gr, (row <= col).astype(BF16), 3)
    lane_head = _head_of(lax.broadcasted_iota(jnp.int32, (1, M_W), 1))
    cbd = c_ref[...]
    nrow = n_ref[...]
    qc = jnp.dot(q.astype(BF16), cbd.astype(BF16), preferred_element_type=F32)
    hout = jnp.zeros((L, M_W), F32)
    wa_all = jnp.zeros((L, M_W), F32)
    dec_all = jnp.zeros((1, M_W), F32)
    for h in range(HEADS):
        hm = lane_head == h
        bt = bcol[:, HEADS + h:HEADS + h + 1]
        li_c = gc[:, h:h + 1]
        rs = gr[h:h + 1, :] - brow[HEADS + h:HEADS + h + 1, :]
        mprev = m_ref[h:h + 1, 0:1]
        dmat = jnp.where(causal, bt + rs, -jnp.inf)
        m_inter = bt + mprev
        mt = jnp.maximum(jnp.max(dmat, axis=1, keepdims=True), m_inter)
        w = jnp.exp(dmat - mt)
        qh = jnp.where(hm, q, 0.0).astype(BF16)
        sc = lax.dot_general(qh, kb, (((1,), (1,)), ((), ())), preferred_element_type=F32) * w
        g = jnp.exp(m_inter - mt)
        pv = jnp.dot(sc.astype(BF16), vb, preferred_element_type=F32)
        qn = jnp.sum(jnp.where(hm, q * nrow, 0.0), axis=1, keepdims=True)
        den = jnp.sum(sc, axis=1, keepdims=True) + g * qn
        scale = 1.0 / jnp.maximum(jnp.abs(den), jnp.exp(-mt))
        hout = jnp.where(hm, (pv + g * qc) * scale, hout)
        bl = bt[L - 1:L, :]
        a = bl - bt + li_c
        mnew = jnp.maximum(bl + mprev, jnp.max(a, axis=0, keepdims=True))
        wa_all = jnp.where(hm, jnp.exp(a - mnew), wa_all)
        dec_all = jnp.where(hm, jnp.exp(bl + mprev - mnew), dec_all)
        m_ref[h:h + 1, :] = jnp.broadcast_to(mnew, (1, m_ref.shape[1]))
    h_ref[...] = hout
    kw = k * wa_all
    cnew = lax.dot_general(kw.astype(BF16), vb, (((0,), (0,)), ((), ())), preferred_element_type=F32)
    r2 = _head_of(lax.broadcasted_iota(jnp.int32, (M_W, M_W), 0))
    c2 = _head_of(lax.broadcasted_iota(jnp.int32, (M_W, M_W), 1))
    c_ref[...] = dec_all * cbd + jnp.where(r2 == c2, cnew, 0.0)
    n_ref[...] = dec_all * nrow + jnp.sum(kw, axis=0, keepdims=True)


def _dot01_rhs_t(mask, x):
    return _dot01_lhs(mask.astype(BF16), x, 3)


def _mlstm(qm, km, vm, gc, gr, *, batch, chunk, nb):
    m = qm.shape[0]
    seq = m // batch
    view = lambda a: a.reshape(batch, seq, a.shape[-1])
    tok = lambda width: pl.BlockSpec((nb, chunk, width), lambda b, c: (b, c, 0))
    out = pl.pallas_call(
        _mlstm_kernel,
        grid=(batch // nb, seq // chunk),
        in_specs=[tok(M_W), tok(M_W), tok(M_W), tok(GATE_PAD),
                  pl.BlockSpec((nb, 2 * HEADS, chunk), lambda b, c: (b, 0, c))],
        out_specs=tok(M_W),
        out_shape=jax.ShapeDtypeStruct((batch, seq, M_W), F32),
        scratch_shapes=[pltpu.VMEM((nb, M_W, M_W), F32), pltpu.VMEM((nb, 1, M_W), F32),
                        pltpu.VMEM((nb, 8, 128), F32)],
        compiler_params=_cparams(("parallel", "arbitrary")),
        name="mlstm",
    )(view(qm), view(km), view(vm), view(gc), gr)
    return out.reshape(m, M_W)


HG_BLK = 16


def _hgrn_kernel(q_ref, k_ref, v_ref, lf_ref, o_ref, st_ref):
    @pl.when(pl.program_id(1) == 0)
    def _():
        st_ref[...] = jnp.zeros_like(st_ref)

    nb = q_ref.shape[0]
    nblk = q_ref.shape[1] // HG_BLK
    r16 = lax.broadcasted_iota(jnp.int32, (HG_BLK, HG_BLK), 0)
    c16 = lax.broadcasted_iota(jnp.int32, (HG_BLK, HG_BLK), 1)
    tri16 = (r16 >= c16).astype(BF16)
    rowid = lax.broadcasted_iota(jnp.int32, (HG_BLK, M_W), 0)
    r2 = _head_of(lax.broadcasted_iota(jnp.int32, (M_W, M_W), 0))
    c2 = _head_of(lax.broadcasted_iota(jnp.int32, (M_W, M_W), 1))
    same_head = r2 == c2
    ones_bd = same_head.astype(BF16)

    def block(i, sq):
        r0 = pl.multiple_of(i * HG_BLK, HG_BLK)
        q = q_ref[sq, pl.ds(r0, HG_BLK), :].astype(F32)
        k = k_ref[sq, pl.ds(r0, HG_BLK), :].astype(F32)
        v = v_ref[sq, pl.ds(r0, HG_BLK), :].astype(F32)
        lf = lf_ref[sq, pl.ds(r0, HG_BLK), :]
        b = _dot01_lhs(tri16, lf, 3)
        bl = b[HG_BLK - 1:HG_BLK, :]
        st = st_ref[sq]
        o = lax.dot_general((q * jnp.exp(b)).astype(BF16), st.astype(BF16), (((1,), (1,)), ((), ())),
                            preferred_element_type=F32)
        ps = []
        for s in range(HG_BLK):
            dd = jnp.where(rowid >= s, b - b[s:s + 1, :], -jnp.inf)
            ps.append(q * k[s:s + 1, :] * jnp.exp(dd))
        p = jnp.concatenate(ps, axis=0).astype(BF16)
        abig = jnp.dot(p, ones_bd, preferred_element_type=F32)
        for s in range(HG_BLK):
            o = o + abig[s * HG_BLK:(s + 1) * HG_BLK, :] * v[s:s + 1, :]
        o_ref[sq, pl.ds(r0, HG_BLK), :] = o
        ke = (k * jnp.exp(bl - b)).astype(BF16)
        upd = lax.dot_general(v.astype(BF16), ke, (((0,), (0,)), ((), ())), preferred_element_type=F32)
        st_ref[sq] = st * jnp.exp(bl) + jnp.where(same_head, upd, 0.0)

    def body(i, carry):
        for sq in range(nb):
            block(i, sq)
        return carry

    lax.fori_loop(0, nblk, body, 0)


def _hgrn(qh, kh, vh, lfh, *, batch, tile, nb):
    m = qh.shape[0]
    seq = m // batch
    view = lambda a: a.reshape(batch, seq, M_W)
    tok = pl.BlockSpec((nb, tile, M_W), lambda b, t: (b, t, 0))
    out = pl.pallas_call(
        _hgrn_kernel,
        grid=(batch // nb, seq // tile),
        in_specs=[tok, tok, tok, tok],
        out_specs=tok,
        out_shape=jax.ShapeDtypeStruct((batch, seq, M_W), F32),
        scratch_shapes=[pltpu.VMEM((nb, M_W, M_W), F32)],
        compiler_params=_cparams(("parallel", "arbitrary")),
        name="hgrn2",
    )(view(qh), view(kh), view(vh), view(lfh))
    return out.reshape(m, M_W)


ONES_ROWS = 16


def _attn_kernel(q_ref, k_ref, vt_ref, bias_ref, c31_ref, lam0_ref, lv_ref, nwt_ref, o_ref, m_ref, acc_ref, *, tile):
    h = pl.program_id(1)
    qi = pl.program_id(2)
    vchunk = vt_ref.shape[2]
    nvc = tile // vchunk
    m_ref[...] = jnp.full_like(m_ref, NEG)
    acc_ref[...] = jnp.zeros_like(acc_ref)
    q = q_ref[...]
    first = lax.broadcasted_iota(jnp.int32, (1, DF_HW), 1) < HEAD_W
    zero = jnp.zeros_like(q)
    qs = (jnp.where(first, q, zero), jnp.where(first, zero, q))
    ones = jnp.ones((ONES_ROWS, vchunk), BF16)
    far_bias = c31_ref[h] * LOG2E

    def tile_step(ki, kind):
        kb = k_ref[pl.ds(pl.multiple_of(ki * tile, tile), tile), :]
        vts = [jnp.concatenate([vt_ref[ki * nvc + c], ones], axis=0) for c in range(nvc)]
        for j in range(2):
            st = lax.dot_general(kb, qs[j], (((1,), (1,)), ((), ())), preferred_element_type=F32)
            if kind == "sub":
                st = st + bias_ref[1]
            elif kind == "diag":
                st = st + bias_ref[0]
            mcur = jnp.max(st, axis=0, keepdims=True)
            if kind == "far":
                mcur = mcur + far_bias
            m_old = m_ref[j]
            m_new = jnp.maximum(m_old, mcur)
            alpha = jnp.exp2(m_old - m_new)
            shift = m_new - far_bias if kind == "far" else m_new
            pt = jnp.exp2((st - shift).astype(BF16))
            pv = None
            for c in range(nvc):
                t = jnp.dot(vts[c], pt[c * vchunk:(c + 1) * vchunk, :], preferred_element_type=F32)
                pv = t if pv is None else pv + t
            acc_ref[j] = alpha * acc_ref[j] + pv
            m_ref[j] = m_new

    def far_body(ki, carry):
        tile_step(ki, "far")
        return carry

    lax.fori_loop(0, qi - 1, far_body, 0)

    @pl.when(qi >= 1)
    def _():
        tile_step(qi - 1, "sub")

    tile_step(qi, "diag")

    lv = lv_ref[...]
    lam0 = lam0_ref[0]
    lam = (jnp.exp(jnp.sum(lv[0:1] * lv[1:2], axis=1, keepdims=True))
           - jnp.exp(jnp.sum(lv[2:3] * lv[3:4], axis=1, keepdims=True)) + lam0)
    a1 = acc_ref[0]
    a2 = acc_ref[1]
    od = a1[0:DF_HW] / a1[DF_HW:DF_HW + 1] - lam * (a2[0:DF_HW] / a2[DF_HW:DF_HW + 1])
    ms = jnp.mean(od * od, axis=0, keepdims=True)
    yt = od * lax.rsqrt(ms + EPS) * nwt_ref[...] * (1.0 - lam0)
    o_ref[...] = yt.T.astype(o_ref.dtype)


def _diff_attn(qd, kd, vdt, bias_tiles, c31, lam0, lam_vecs, norm_w_t, layer, *, batch, tile):
    m = qd.shape[0]
    seq = m // batch
    nq = seq // tile
    nchunk, vchunk = vdt.shape[1], vdt.shape[3]
    return pl.pallas_call(
        functools.partial(_attn_kernel, tile=tile),
        grid=(batch, HEADS, nq),
        in_specs=[
            pl.BlockSpec((tile, DF_HW), lambda b, h, i: (b * nq + i, h)),
            pl.BlockSpec((seq, DF_HW), lambda b, h, i: (b, h)),
            pl.BlockSpec((None, nchunk, DF_HW, vchunk), lambda b, h, i: (b, 0, h, 0)),
            pl.BlockSpec((None, 2, tile, tile), lambda b, h, i: (h, 0, 0, 0)),
            pl.BlockSpec(memory_space=pltpu.SMEM),
            pl.BlockSpec(memory_space=pltpu.SMEM),
            pl.BlockSpec((None, 4, HEAD_W), lambda b, h, i: (layer, 0, 0)),
            pl.BlockSpec((None, DF_HW, 1), lambda b, h, i: (layer, h, 0)),
        ],
        out_specs=pl.BlockSpec((tile, DF_HW), lambda b, h, i: (b * nq + i, h)),
        out_shape=jax.ShapeDtypeStruct((m, DF_W), BF16),
        scratch_shapes=[pltpu.VMEM((2, 1, tile), F32), pltpu.VMEM((2, DF_HW + ONES_ROWS, tile), F32)],
        compiler_params=_cparams(("parallel", "parallel", "arbitrary")),
        name="diff_attn",
    )(qd, kd, vdt, bias_tiles, c31, lam0, lam_vecs, norm_w_t)


def _mix_out_kernel(x_ref, hm_ref, om_ref, yd_ref, oh_ref, gh_ref, mnw_ref, hnw_ref, w_ref, nw_ref, o_ref):
    r2 = _head_of(lax.broadcasted_iota(jnp.int32, (M_W, M_W), 0))
    c2 = _head_of(lax.broadcasted_iota(jnp.int32, (M_W, M_W), 1))
    ones_bd = (r2 == c2).astype(BF16)
    inv = 1.0 / HEAD_W

    hm = hm_ref[...]
    xc = hm - _dot01_rhs(hm, ones_bd, 2) * inv
    var = _dot01_rhs(xc * xc, ones_bd, 2) * inv
    ym = xc * lax.rsqrt(var + EPS) * mnw_ref[...] * _sigmoid(om_ref[...])

    oh = oh_ref[...]
    ms = _dot01_rhs(oh * oh, ones_bd, 2) * inv
    gh = gh_ref[...]
    yh = oh * lax.rsqrt(ms + EPS) * hnw_ref[...] * (gh * _sigmoid(gh))

    w = w_ref[...]
    acc = jnp.dot(ym.astype(BF16), w[0:M_W], preferred_element_type=F32)
    acc = acc + jnp.dot(yd_ref[...].astype(BF16), w[M_W:M_W + DF_W], preferred_element_type=F32)
    acc = acc + jnp.dot(yh.astype(BF16), w[M_W + DF_W:], preferred_element_type=F32)
    o_ref[...] = x_ref[...] + _rms(acc, nw_ref[...])


def _mix_out(x, hm, om, yd, oh, gh, mnw, hnw, w_out, nw, layer, *, tm):
    m, d = x.shape
    tok = lambda width: pl.BlockSpec((tm, width), lambda i: (i, 0))
    lay = lambda *shape: pl.BlockSpec((None,) + shape, lambda i: (layer,) + (0,) * len(shape))
    return pl.pallas_call(
        _mix_out_kernel,
        grid=(m // tm,),
        in_specs=[tok(d), tok(M_W), tok(M_W), tok(DF_W), tok(M_W), tok(M_W),
                  lay(1, M_W), lay(1, M_W), lay(d, d),
                  pl.BlockSpec((None, None, 1, d), lambda i: (layer, 3, 0, 0))],
        out_specs=tok(d),
        out_shape=jax.ShapeDtypeStruct((m, d), F32),
        compiler_params=_cparams(("parallel",)),
        name="mix_out",
    )(x, hm, om, yd, oh, gh, mnw, hnw, w_out, nw)


def _tiles(batch, seq):
    return dict(
        ffn_tm=min(512, seq), proj_tm=min(512, seq), mlstm_chunk=min(256, seq), hgrn_tile=min(256, seq),
        attn_tile=min(512, seq), out_tm=min(512, seq), hgrn_nb=4 if batch % 4 == 0 else 1,
        mlstm_nb=2 if batch % 2 == 0 else 1)


def _ffn_tf(dff):
    return dff


def kernel(x, norm_w, ffn1_wi, ffn1_wo, ffn2_wi, ffn2_wo, w_in, w_out, mlstm_conv_w, mlstm_conv_b, mlstm_igate_b,
           mlstm_fgate_b, mlstm_norm_w, diff_lambda, diff_norm_w, rel_bias, hgrn_lb_logits, hgrn_norm_w):
    batch, seq, d = x.shape
    depth = norm_w.shape[0]
    tl = _tiles(batch, seq)
    tf = _ffn_tf(ffn1_wo.shape[1])
    act_dtype = BF16

    off = np.concatenate([[0], np.cumsum(SPLIT_SIZES)])
    seg = lambda i: np.arange(off[i], off[i + 1])
    order = np.concatenate([seg(0), seg(1), seg(2), seg(5), seg(6), seg(8), seg(9), seg(10), seg(11),
                            seg(3), seg(4)])
    w_main = jnp.take(w_in, jnp.asarray(order), axis=2)
    w_proj = jnp.pad(w_main, ((0, 0), (0, 0), (0, GATE_PAD - 2 * HEADS))).astype(BF16)
    w_vd_t = jnp.swapaxes(w_in[:, :, off[7]:off[8]], 1, 2).astype(BF16)
    w_gate_t = jnp.swapaxes(w_in[:, :, off[3]:off[5]], 1, 2).astype(BF16)
    gate_b = jnp.concatenate([mlstm_igate_b, mlstm_fgate_b], axis=1).astype(F32)
    gate_b_col = jnp.pad(gate_b, ((0, 0), (0, GATE_PAD - 2 * HEADS)))[:, None, :]
    gate_b_row = gate_b[:, :, None]
    dff = ffn1_wo.shape[1]
    nf = dff // tf
    chunk_wi = lambda w: jnp.swapaxes(w.astype(BF16).reshape(depth, d, 2 * nf, tf), 1, 2)
    chunk_wo = lambda w: w.astype(BF16).reshape(depth, nf, tf, d)
    wi1, wo1 = chunk_wi(ffn1_wi), chunk_wo(ffn1_wo)
    wi2, wo2 = chunk_wi(ffn2_wi), chunk_wo(ffn2_wo)
    w_o = w_out.astype(BF16)
    nw = norm_w.astype(F32)[:, :, None, :]
    conv_w = mlstm_conv_w.astype(F32)
    conv_b = mlstm_conv_b.astype(F32)[:, None, :]
    mnw = mlstm_norm_w.astype(F32)[:, None, :]
    hnw = hgrn_norm_w.astype(F32)[:, None, :]
    dnw_t = diff_norm_w.astype(F32)[:, :, None]
    lam_vecs = diff_lambda.astype(F32)

    lb_all = _hgrn_lower_bounds(hgrn_lb_logits)[:, None, :]
    bias_tiles = _rel_bias_tiles(rel_bias, tl["attn_tile"])
    c31 = rel_bias[REL_BUCKETS - 1].astype(F32)

    xf = x.reshape(batch * seq, d)
    for l in range(depth):
        lam0 = jnp.full((1,), 0.8 - 0.6 * math.exp(-0.3 * l), F32)
        xf = _ffn(xf, nw, l, 0, 1, wi1, wo1, tm=tl["ffn_tm"])
        (qm, km, vm, om, gc, gr, qd, kd, vdt, qh, lfh, kh, vh, gh) = _proj(
            xf, nw, l, w_proj, w_vd_t, w_gate_t, conv_w, conv_b, gate_b_col, gate_b_row, lb_all,
            batch=batch, tm=tl["proj_tm"], act_dtype=act_dtype)
        hm = _mlstm(qm, km, vm, gc, gr, batch=batch, chunk=tl["mlstm_chunk"], nb=tl["mlstm_nb"])
        yd = _diff_attn(qd, kd, vdt, bias_tiles, c31, lam0, lam_vecs, dnw_t, l, batch=batch, tile=tl["attn_tile"])
        oh = _hgrn(qh, kh, vh, lfh, batch=batch, tile=tl["hgrn_tile"], nb=tl["hgrn_nb"])
        xf = _mix_out(xf, hm, om, yd, oh, gh, mnw, hnw, w_o, nw, l, tm=tl["out_tm"])
        xf = _ffn(xf, nw, l, 4, 5, wi2, wo2, tm=tl["ffn_tm"])
    return xf.reshape(batch, seq, d)
```

```python
import functools
import math

import numpy as np
import jax
import jax.numpy as jnp
from jax import lax
from jax.experimental import pallas as pl
from jax.experimental.pallas import tpu as pltpu

F32 = jnp.float32
BF16 = jnp.bfloat16
EPS = 1e-6
NEG = -1e30
LOG2E = 1.4426950408889634

HEADS = 4
HEAD_W = 64
M_W = HEADS * HEAD_W
DF_HW = 128
DF_W = HEADS * DF_HW
CONV_W = 4
REL_BUCKETS = 32
REL_MAX_EXACT = 16
REL_MAX_DIST = 128
GATE_PAD = 128
SPLIT_SIZES = (2 * M_W, M_W, M_W, HEADS, HEADS, DF_W, DF_W, DF_W, M_W, M_W, M_W, M_W)
VMEM_LIMIT = 56 * 1024 * 1024


def _cparams(sem):
    return pltpu.CompilerParams(dimension_semantics=sem, vmem_limit_bytes=VMEM_LIMIT)


def _sigmoid(x):
    return 1.0 / (1.0 + jnp.exp(-x))


def _log_sigmoid(x):
    return jnp.minimum(x, 0.0) - jnp.log1p(jnp.exp(-jnp.abs(x)))


def _rms(x, w):
    return x * lax.rsqrt(jnp.mean(x * x, axis=-1, keepdims=True) + EPS) * w


def _split_bf16(x, parts):
    out = []
    r = x
    for _ in range(parts):
        p = r.astype(BF16)
        out.append(p)
        r = r - p.astype(F32)
    return out


def _dot01_rhs(x, m01, parts):
    acc = None
    for p in _split_bf16(x, parts):
        t = jnp.dot(p, m01, preferred_element_type=F32)
        acc = t if acc is None else acc + t
    return acc


def _dot01_lhs(m01, x, parts):
    acc = None
    for p in _split_bf16(x, parts):
        t = jnp.dot(m01, p, preferred_element_type=F32)
        acc = t if acc is None else acc + t
    return acc


def _head_of(idx):
    return lax.shift_right_logical(idx, 6)


def _lb_kernel(lg_ref, o_ref):
    lg = lg_ref[...]
    e = jnp.exp(lg - jnp.max(lg, axis=0, keepdims=True))
    sm = e / jnp.sum(e, axis=0, keepdims=True)
    depth = lg.shape[0]
    rows = []
    run = sm[0:1]
    first = run
    for i in range(depth):
        if i > 0:
            run = run + sm[i:i + 1]
        rows.append(jnp.maximum(run - first, 0.0))
    o_ref[...] = jnp.concatenate(rows, axis=0)


def _hgrn_lower_bounds(logits):
    return pl.pallas_call(
        _lb_kernel, out_shape=jax.ShapeDtypeStruct(logits.shape, F32), name="hgrn_lb")(logits.astype(F32))


def _bias_kernel(tab_ref, o_ref, *, tile):
    h = pl.program_id(0)
    typ = pl.program_id(1)
    r = lax.broadcasted_iota(jnp.int32, (tile, tile), 0)
    c = lax.broadcasted_iota(jnp.int32, (tile, tile), 1)
    rel = c - r + typ * tile
    n = jnp.maximum(rel, 0)
    nf = jnp.maximum(n, 1).astype(F32)
    large = REL_MAX_EXACT + (jnp.log(nf / REL_MAX_EXACT) / math.log(REL_MAX_DIST / REL_MAX_EXACT)
                             * (REL_BUCKETS - REL_MAX_EXACT)).astype(jnp.int32)
    large = jnp.minimum(large, REL_BUCKETS - 1)
    bucket = jnp.where(n < REL_MAX_EXACT, n, large)
    bias = jnp.zeros((tile, tile), F32)
    for b in range(REL_BUCKETS):
        bias = jnp.where(bucket == b, tab_ref[b, h], bias)
    o_ref[...] = jnp.where(rel >= 0, bias * LOG2E, NEG)


def _rel_bias_tiles(rel_bias, tile):
    assert tile >= REL_MAX_DIST
    return pl.pallas_call(
        functools.partial(_bias_kernel, tile=tile),
        grid=(HEADS, 3),
        in_specs=[pl.BlockSpec(memory_space=pltpu.SMEM)],
        out_specs=pl.BlockSpec((None, None, tile, tile), lambda h, t: (h, t, 0, 0)),
        out_shape=jax.ShapeDtypeStruct((HEADS, 3, tile, tile), F32),
        name="rel_bias_tiles",
    )(rel_bias.astype(F32))


def _ffn_kernel(x_ref, nwi_ref, wi_ref, wo_ref, nwo_ref, o_ref, xn_ref, acc_ref):
    j = pl.program_id(1)

    @pl.when(j == 0)
    def _():
        xn_ref[...] = _rms(x_ref[...], nwi_ref[...]).astype(BF16)
        acc_ref[...] = jnp.zeros_like(acc_ref)

    nf = wo_ref.shape[0]
    xn = xn_ref[...]
    g = jnp.dot(xn, wi_ref[j], preferred_element_type=F32)
    u = jnp.dot(xn, wi_ref[nf + j], preferred_element_type=F32)
    a = (g * _sigmoid(g) * u).astype(BF16)
    acc_ref[...] += jnp.dot(a, wo_ref[j], preferred_element_type=F32)

    @pl.when(j == nf - 1)
    def _():
        o_ref[...] = x_ref[...] + 0.5 * _rms(acc_ref[...], nwo_ref[...])


def _ffn(x, nw, layer, row_in, row_out, wi, wo, *, tm):
    m, d = x.shape
    nf, tf = wo.shape[1], wo.shape[2]
    resident = pl.Buffered(1)
    return pl.pallas_call(
        _ffn_kernel,
        grid=(m // tm, nf),
        in_specs=[
            pl.BlockSpec((tm, d), lambda i, j: (i, 0)),
            pl.BlockSpec((None, None, 1, d), lambda i, j: (layer, row_in, 0, 0)),
            pl.BlockSpec((None, 2 * nf, d, tf), lambda i, j: (layer, 0, 0, 0), pipeline_mode=resident),
            pl.BlockSpec((None, nf, tf, d), lambda i, j: (layer, 0, 0, 0), pipeline_mode=resident),
            pl.BlockSpec((None, None, 1, d), lambda i, j: (layer, row_out, 0, 0)),
        ],
        out_specs=pl.BlockSpec((tm, d), lambda i, j: (i, 0)),
        out_shape=jax.ShapeDtypeStruct((m, d), F32),
        scratch_shapes=[pltpu.VMEM((tm, d), BF16), pltpu.VMEM((tm, d), F32)],
        compiler_params=_cparams(("parallel", "arbitrary")),
        name="ffn",
    )(x, nw, wi, wo, nw)


def _proj_kernel(x_ref, nw_ref, w_ref, wvt_ref, wgt_ref, cw_ref, cb_ref, gbc_ref, gbr_ref, lb_ref,
                 qm_ref, km_ref, vm_ref, om_ref, gc_ref, gr_ref, qd_ref, kd_ref, vdt_ref,
                 qh_ref, lfh_ref, kh_ref, vh_ref, gh_ref, cbuf_ref):
    t = pl.program_id(1)
    tm = x_ref.shape[0]
    xn = _rms(x_ref[...], nw_ref[...]).astype(BF16)

    def u(lo, hi):
        return jnp.dot(xn, w_ref[:, lo:hi], preferred_element_type=F32)

    @pl.when(t == 0)
    def _():
        cbuf_ref[0:8, :] = jnp.zeros((8, 2 * M_W), F32)

    qk = u(0, 2 * M_W)
    cbuf_ref[8:8 + tm, :] = qk
    cw = cw_ref[...]
    y = cb_ref[...] + cw[CONV_W - 1:CONV_W] * qk
    for d in range(1, CONV_W):
        y = y + cw[CONV_W - 1 - d:CONV_W - d] * cbuf_ref[8 - d:8 - d + tm, :]
    cbuf_ref[0:8, :] = cbuf_ref[tm:tm + 8, :]
    y = y * _sigmoid(y)
    qm_ref[...] = y[:, 0:M_W].astype(qm_ref.dtype)
    km_ref[...] = (y[:, M_W:2 * M_W] * (HEAD_W ** -0.5)).astype(km_ref.dtype)
    vm_ref[...] = u(512, 768).astype(vm_ref.dtype)
    om_ref[...] = u(768, 1024)

    qd_ref[...] = (u(1024, 1536) * (HEAD_W ** -0.5 * LOG2E)).astype(qd_ref.dtype)
    kd_ref[...] = u(1536, 2048).astype(kd_ref.dtype)
    vdt_ref[...] = lax.dot_general(wvt_ref[...], xn, (((1,), (1,)), ((), ())),
                                   preferred_element_type=F32).astype(vdt_ref.dtype)

    qh = u(2048, 2304)
    qh_ref[...] = (qh * _sigmoid(qh)).astype(qh_ref.dtype)
    fp = u(2304, 2560)
    lb = lb_ref[...]
    a = jnp.log(lb)
    bb = jnp.log1p(-lb) + _log_sigmoid(fp)
    lfh_ref[...] = jnp.maximum(a, bb) + jnp.log1p(jnp.exp(-jnp.abs(a - bb)))
    kh_ref[...] = ((1.0 - lb) * _sigmoid(-fp)).astype(kh_ref.dtype)
    vh_ref[...] = u(2560, 2816).astype(vh_ref.dtype)
    gh_ref[...] = u(2816, 3072)

    zc = u(3072, 3072 + GATE_PAD) + gbc_ref[...]
    lane = lax.broadcasted_iota(jnp.int32, (1, GATE_PAD), 1)
    gc_ref[...] = jnp.where(lane < HEADS, zc, _log_sigmoid(zc))
    zr = lax.dot_general(wgt_ref[...], xn, (((1,), (1,)), ((), ())), preferred_element_type=F32) + gbr_ref[...]
    row = lax.broadcasted_iota(jnp.int32, (2 * HEADS, 1), 0)
    gr_ref[...] = jnp.where(row < HEADS, zr, _log_sigmoid(zr))


def _proj(x, nw, layer, w, wvt, wgt, cw, cb, gbc, gbr, lb, *, batch, tm, act_dtype):
    m, d = x.shape
    seq = m // batch
    nt = seq // tm
    npc = w.shape[-1]
    tok = lambda width: pl.BlockSpec((tm, width), lambda b, t: (b * nt + t, 0))
    lay = lambda *shape: pl.BlockSpec((None,) + shape, lambda b, t: (layer,) + (0,) * len(shape),
                                      pipeline_mode=pl.Buffered(1))
    sds = lambda width, dt: jax.ShapeDtypeStruct((m, width), dt)
    out_shape = [sds(M_W, act_dtype), sds(M_W, act_dtype), sds(M_W, act_dtype), sds(M_W, F32),
                 sds(GATE_PAD, F32), jax.ShapeDtypeStruct((batch, 2 * HEADS, seq), F32),
                 sds(DF_W, act_dtype), sds(DF_W, act_dtype),
                 jax.ShapeDtypeStruct((batch, nt, DF_W, tm), act_dtype),
                 sds(M_W, act_dtype), sds(M_W, F32), sds(M_W, act_dtype), sds(M_W, act_dtype), sds(M_W, F32)]
    out_specs = [tok(M_W), tok(M_W), tok(M_W), tok(M_W), tok(GATE_PAD),
                 pl.BlockSpec((None, 2 * HEADS, tm), lambda b, t: (b, 0, t)),
                 tok(DF_W), tok(DF_W),
                 pl.BlockSpec((None, None, DF_W, tm), lambda b, t: (b, t, 0, 0)),
                 tok(M_W), tok(M_W), tok(M_W), tok(M_W), tok(M_W)]
    return pl.pallas_call(
        _proj_kernel,
        grid=(batch, nt),
        in_specs=[tok(d),
                  pl.BlockSpec((None, None, 1, d), lambda b, t: (layer, 2, 0, 0)),
                  lay(d, npc), lay(DF_W, d), lay(2 * HEADS, d), lay(CONV_W, 2 * M_W), lay(1, 2 * M_W),
                  lay(1, GATE_PAD), lay(2 * HEADS, 1), lay(1, M_W)],
        out_specs=out_specs,
        out_shape=out_shape,
        scratch_shapes=[pltpu.VMEM((tm + 8, 2 * M_W), F32)],
        compiler_params=_cparams(("parallel", "arbitrary")),
        name="mixer_proj",
    )(x, nw, w, wvt, wgt, cw, cb, gbc, gbr, lb)


def _mlstm_kernel(q_ref, k_ref, v_ref, gc_ref, gr_ref, h_ref, c_ref, n_ref, m_ref):
    @pl.when(pl.program_id(1) == 0)
    def _():
        c_ref[...] = jnp.zeros_like(c_ref)
        n_ref[...] = jnp.zeros_like(n_ref)
        m_ref[...] = jnp.zeros_like(m_ref)

    for sq in range(q_ref.shape[0]):
        _mlstm_chunk(*(r.at[sq] for r in (q_ref, k_ref, v_ref, gc_ref, gr_ref, h_ref, c_ref, n_ref, m_ref)))


def _mlstm_chunk(q_ref, k_ref, v_ref, gc_ref, gr_ref, h_ref, c_ref, n_ref, m_ref):
    L = q_ref.shape[0]
    q = q_ref[...].astype(F32)
    k = k_ref[...].astype(F32)
    vb = v_ref[...].astype(BF16)
    kb = k.astype(BF16)
    gc = gc_ref[...]
    gr = gr_ref[...]
    row = lax.broadcasted_iota(jnp.int32, (L, L), 0)
    col = lax.broadcasted_iota(jnp.int32, (L, L), 1)
    causal = row >= col
    bcol = _dot01_rhs_t(causal, gc)
    brow = _dot01_rhs(gr, (row <= col).astype(BF16), 3)
    lane_head = _head_of(lax.broadcasted_iota(jnp.int32, (1, M_W), 1))
    cbd = c_ref[...]
    nrow = n_ref[...]
    qc = jnp.dot(q.astype(BF16), cbd.astype(BF16), preferred_element_type=F32)
    hout = jnp.zeros((L, M_W), F32)
    wa_all = jnp.zeros((L, M_W), F32)
    dec_all = jnp.zeros((1, M_W), F32)
    for h in range(HEADS):
        hm = lane_head == h
        bt = bcol[:, HEADS + h:HEADS + h + 1]
        li_c = gc[:, h:h + 1]
        rs = gr[h:h + 1, :] - brow[HEADS + h:HEADS + h + 1, :]
        mprev = m_ref[h:h + 1, 0:1]
        dmat = jnp.where(causal, bt + rs, -jnp.inf)
        m_inter = bt + mprev
        mt = jnp.maximum(jnp.max(dmat, axis=1, keepdims=True), m_inter)
        w = jnp.exp(dmat - mt)
        qh = jnp.where(hm, q, 0.0).astype(BF16)
        sc = lax.dot_general(qh, kb, (((1,), (1,)), ((), ())), preferred_element_type=F32) * w
        g = jnp.exp(m_inter - mt)
        pv = jnp.dot(sc.astype(BF16), vb, preferred_element_type=F32)
        qn = jnp.sum(jnp.where(hm, q * nrow, 0.0), axis=1, keepdims=True)
        den = jnp.sum(sc, axis=1, keepdims=True) + g * qn
        scale = 1.0 / jnp.maximum(jnp.abs(den), jnp.exp(-mt))
        hout = jnp.where(hm, (pv + g * qc) * scale, hout)
        bl = bt[L - 1:L, :]
        a = bl - bt + li_c
        mnew = jnp.maximum(bl + mprev, jnp.max(a, axis=0, keepdims=True))
        wa_all = jnp.where(hm, jnp.exp(a - mnew), wa_all)
        dec_all = jnp.where(hm, jnp.exp(bl + mprev - mnew), dec_all)
        m_ref[h:h + 1, :] = jnp.broadcast_to(mnew, (1, m_ref.shape[1]))
    h_ref[...] = hout
    kw = k * wa_all
    cnew = lax.dot_general(kw.astype(BF16), vb, (((0,), (0,)), ((), ())), preferred_element_type=F32)
    r2 = _head_of(lax.broadcasted_iota(jnp.int32, (M_W, M_W), 0))
    c2 = _head_of(lax.broadcasted_iota(jnp.int32, (M_W, M_W), 1))
    c_ref[...] = dec_all * cbd + jnp.where(r2 == c2, cnew, 0.0)
    n_ref[...] = dec_all * nrow + jnp.sum(kw, axis=0, keepdims=True)


def _dot01_rhs_t(mask, x):
    return _dot01_lhs(mask.astype(BF16), x, 3)


def _mlstm(qm, km, vm, gc, gr, *, batch, chunk, nb):
    m = qm.shape[0]
    seq = m // batch
    view = lambda a: a.reshape(batch, seq, a.shape[-1])
    tok = lambda width: pl.BlockSpec((nb, chunk, width), lambda b, c: (b, c, 0))
    out = pl.pallas_call(
        _mlstm_kernel,
        grid=(batch // nb, seq // chunk),
        in_specs=[tok(M_W), tok(M_W), tok(M_W), tok(GATE_PAD),
                  pl.BlockSpec((nb, 2 * HEADS, chunk), lambda b, c: (b, 0, c))],
        out_specs=tok(M_W),
        out_shape=jax.ShapeDtypeStruct((batch, seq, M_W), F32),
        scratch_shapes=[pltpu.VMEM((nb, M_W, M_W), F32), pltpu.VMEM((nb, 1, M_W), F32),
                        pltpu.VMEM((nb, 8, 128), F32)],
        compiler_params=_cparams(("parallel", "arbitrary")),
        name="mlstm",
    )(view(qm), view(km), view(vm), view(gc), gr)
    return out.reshape(m, M_W)


HG_BLK = 16


def _hgrn_kernel(q_ref, k_ref, v_ref, lf_ref, o_ref, st_ref):
    @pl.when(pl.program_id(1) == 0)
    def _():
        st_ref[...] = jnp.zeros_like(st_ref)

    nb = q_ref.shape[0]
    nblk = q_ref.shape[1] // HG_BLK
    r16 = lax.broadcasted_iota(jnp.int32, (HG_BLK, HG_BLK), 0)
    c16 = lax.broadcasted_iota(jnp.int32, (HG_BLK, HG_BLK), 1)
    tri16 = (r16 >= c16).astype(BF16)
    rowid = lax.broadcasted_iota(jnp.int32, (HG_BLK, M_W), 0)
    r2 = _head_of(lax.broadcasted_iota(jnp.int32, (M_W, M_W), 0))
    c2 = _head_of(lax.broadcasted_iota(jnp.int32, (M_W, M_W), 1))
    same_head = r2 == c2
    ones_bd = same_head.astype(BF16)

    def block(i, sq):
        r0 = pl.multiple_of(i * HG_BLK, HG_BLK)
        q = q_ref[sq, pl.ds(r0, HG_BLK), :].astype(F32)
        k = k_ref[sq, pl.ds(r0, HG_BLK), :].astype(F32)
        v = v_ref[sq, pl.ds(r0, HG_BLK), :].astype(F32)
        lf = lf_ref[sq, pl.ds(r0, HG_BLK), :]
        b = _dot01_lhs(tri16, lf, 3)
        bl = b[HG_BLK - 1:HG_BLK, :]
        st = st_ref[sq]
        o = lax.dot_general((q * jnp.exp(b)).astype(BF16), st.astype(BF16), (((1,), (1,)), ((), ())),
                            preferred_element_type=F32)
        ps = []
        for s in range(HG_BLK):
            dd = jnp.where(rowid >= s, b - b[s:s + 1, :], -jnp.inf)
            ps.append(q * k[s:s + 1, :] * jnp.exp(dd))
        p = jnp.concatenate(ps, axis=0).astype(BF16)
        abig = jnp.dot(p, ones_bd, preferred_element_type=F32)
        for s in range(HG_BLK):
            o = o + abig[s * HG_BLK:(s + 1) * HG_BLK, :] * v[s:s + 1, :]
        o_ref[sq, pl.ds(r0, HG_BLK), :] = o
        ke = (k * jnp.exp(bl - b)).astype(BF16)
        upd = lax.dot_general(v.astype(BF16), ke, (((0,), (0,)), ((), ())), preferred_element_type=F32)
        st_ref[sq] = st * jnp.exp(bl) + jnp.where(same_head, upd, 0.0)

    def body(i, carry):
        for sq in range(nb):
            block(i, sq)
        return carry

    lax.fori_loop(0, nblk, body, 0)


def _hgrn(qh, kh, vh, lfh, *, batch, tile, nb):
    m = qh.shape[0]
    seq = m // batch
    view = lambda a: a.reshape(batch, seq, M_W)
    tok = pl.BlockSpec((nb, tile, M_W), lambda b, t: (b, t, 0))
    out = pl.pallas_call(
        _hgrn_kernel,
        grid=(batch // nb, seq // tile),
        in_specs=[tok, tok, tok, tok],
        out_specs=tok,
        out_shape=jax.ShapeDtypeStruct((batch, seq, M_W), F32),
        scratch_shapes=[pltpu.VMEM((nb, M_W, M_W), F32)],
        compiler_params=_cparams(("parallel", "arbitrary")),
        name="hgrn2",
    )(view(qh), view(kh), view(vh), view(lfh))
    return out.reshape(m, M_W)


ONES_ROWS = 16


def _attn_kernel(q_ref, k_ref, vt_ref, bias_ref, lam0_ref, lv_ref, nwt_ref, o_ref, m_ref, acc_ref, sa_ref, sb_ref,
                 *, tile):
    qi = pl.program_id(2)
    vchunk = vt_ref.shape[2]
    nvc = tile // vchunk
    m_ref[...] = jnp.full_like(m_ref, NEG)
    acc_ref[...] = jnp.zeros_like(acc_ref)
    q = q_ref[...]
    first = lax.broadcasted_iota(jnp.int32, (1, DF_HW), 1) < HEAD_W
    zero = jnp.zeros_like(q)
    qs = (jnp.where(first, q, zero), jnp.where(first, zero, q))
    ones = jnp.ones((ONES_ROWS, vchunk), BF16)

    def scores(ki, s_ref):
        kb = k_ref[pl.ds(pl.multiple_of(ki * tile, tile), tile), :]
        for j in range(2):
            s_ref[j] = lax.dot_general(kb, qs[j], (((1,), (1,)), ((), ())), preferred_element_type=F32)

    def softmax_pv(ki, s_ref):
        bias = bias_ref[jnp.minimum(qi - ki, 2)]
        vts = [jnp.concatenate([vt_ref[ki * nvc + c], ones], axis=0) for c in range(nvc)]
        for j in range(2):
            st = s_ref[j] + bias
            m_old = m_ref[j]
            m_new = jnp.maximum(m_old, jnp.max(st, axis=0, keepdims=True))
            alpha = jnp.exp2(m_old - m_new)
            pt = jnp.exp2((st - m_new).astype(BF16))
            pv = None
            for c in range(nvc):
                t = jnp.dot(vts[c], pt[c * vchunk:(c + 1) * vchunk, :], preferred_element_type=F32)
                pv = t if pv is None else pv + t
            acc_ref[j] = alpha * acc_ref[j] + pv
            m_ref[j] = m_new

    ntiles = qi + 1
    scores(0, sa_ref)

    def pair(p, carry):
        ka = 2 * p
        scores(ka + 1, sb_ref)
        softmax_pv(ka, sa_ref)
        scores(jnp.minimum(ka + 2, qi), sa_ref)
        softmax_pv(ka + 1, sb_ref)
        return carry

    lax.fori_loop(0, ntiles // 2, pair, 0)

    @pl.when(ntiles % 2 == 1)
    def _():
        softmax_pv(qi, sa_ref)

    lv = lv_ref[...]
    lam0 = lam0_ref[0]
    lam = (jnp.exp(jnp.sum(lv[0:1] * lv[1:2], axis=1, keepdims=True))
           - jnp.exp(jnp.sum(lv[2:3] * lv[3:4], axis=1, keepdims=True)) + lam0)
    a1 = acc_ref[0]
    a2 = acc_ref[1]
    od = a1[0:DF_HW] / a1[DF_HW:DF_HW + 1] - lam * (a2[0:DF_HW] / a2[DF_HW:DF_HW + 1])
    ms = jnp.mean(od * od, axis=0, keepdims=True)
    yt = od * lax.rsqrt(ms + EPS) * nwt_ref[...] * (1.0 - lam0)
    o_ref[...] = yt.T.astype(o_ref.dtype)


def _diff_attn(qd, kd, vdt, bias_tiles, lam0, lam_vecs, norm_w_t, layer, *, batch, tile):
    m = qd.shape[0]
    seq = m // batch
    nq = seq // tile
    nchunk, vchunk = vdt.shape[1], vdt.shape[3]
    return pl.pallas_call(
        functools.partial(_attn_kernel, tile=tile),
        grid=(batch, HEADS, nq),
        in_specs=[
            pl.BlockSpec((tile, DF_HW), lambda b, h, i: (b * nq + i, h)),
            pl.BlockSpec((seq, DF_HW), lambda b, h, i: (b, h)),
            pl.BlockSpec((None, nchunk, DF_HW, vchunk), lambda b, h, i: (b, 0, h, 0)),
            pl.BlockSpec((None, 3, tile, tile), lambda b, h, i: (h, 0, 0, 0)),
            pl.BlockSpec(memory_space=pltpu.SMEM),
            pl.BlockSpec((None, 4, HEAD_W), lambda b, h, i: (layer, 0, 0)),
            pl.BlockSpec((None, DF_HW, 1), lambda b, h, i: (layer, h, 0)),
        ],
        out_specs=pl.BlockSpec((tile, DF_HW), lambda b, h, i: (b * nq + i, h)),
        out_shape=jax.ShapeDtypeStruct((m, DF_W), BF16),
        scratch_shapes=[pltpu.VMEM((2, 1, tile), F32), pltpu.VMEM((2, DF_HW + ONES_ROWS, tile), F32),
                        pltpu.VMEM((2, tile, tile), F32), pltpu.VMEM((2, tile, tile), F32)],
        compiler_params=_cparams(("parallel", "parallel", "arbitrary")),
        name="diff_attn",
    )(qd, kd, vdt, bias_tiles, lam0, lam_vecs, norm_w_t)


def _mix_out_kernel(x_ref, hm_ref, om_ref, yd_ref, oh_ref, gh_ref, mnw_ref, hnw_ref, w_ref, nw_ref, o_ref):
    r2 = _head_of(lax.broadcasted_iota(jnp.int32, (M_W, M_W), 0))
    c2 = _head_of(lax.broadcasted_iota(jnp.int32, (M_W, M_W), 1))
    ones_bd = (r2 == c2).astype(BF16)
    inv = 1.0 / HEAD_W

    hm = hm_ref[...]
    xc = hm - _dot01_rhs(hm, ones_bd, 2) * inv
    var = _dot01_rhs(xc * xc, ones_bd, 2) * inv
    ym = xc * lax.rsqrt(var + EPS) * mnw_ref[...] * _sigmoid(om_ref[...])

    oh = oh_ref[...]
    ms = _dot01_rhs(oh * oh, ones_bd, 2) * inv
    gh = gh_ref[...]
    yh = oh * lax.rsqrt(ms + EPS) * hnw_ref[...] * (gh * _sigmoid(gh))

    w = w_ref[...]
    acc = jnp.dot(ym.astype(BF16), w[0:M_W], preferred_element_type=F32)
    acc = acc + jnp.dot(yd_ref[...].astype(BF16), w[M_W:M_W + DF_W], preferred_element_type=F32)
    acc = acc + jnp.dot(yh.astype(BF16), w[M_W + DF_W:], preferred_element_type=F32)
    o_ref[...] = x_ref[...] + _rms(acc, nw_ref[...])


def _mix_out(x, hm, om, yd, oh, gh, mnw, hnw, w_out, nw, layer, *, tm):
    m, d = x.shape
    tok = lambda width: pl.BlockSpec((tm, width), lambda i: (i, 0))
    lay = lambda *shape: pl.BlockSpec((None,) + shape, lambda i: (layer,) + (0,) * len(shape))
    return pl.pallas_call(
        _mix_out_kernel,
        grid=(m // tm,),
        in_specs=[tok(d), tok(M_W), tok(M_W), tok(DF_W), tok(M_W), tok(M_W),
                  lay(1, M_W), lay(1, M_W), lay(d, d),
                  pl.BlockSpec((None, None, 1, d), lambda i: (layer, 3, 0, 0))],
        out_specs=tok(d),
        out_shape=jax.ShapeDtypeStruct((m, d), F32),
        compiler_params=_cparams(("parallel",)),
        name="mix_out",
    )(x, hm, om, yd, oh, gh, mnw, hnw, w_out, nw)


def _tiles(batch, seq):
    return dict(
        ffn_tm=min(512, seq), proj_tm=min(512, seq), mlstm_chunk=min(256, seq), hgrn_tile=min(256, seq),
        attn_tile=min(512, seq), out_tm=min(512, seq), hgrn_nb=4 if batch % 4 == 0 else 1,
        mlstm_nb=2 if batch % 2 == 0 else 1)


def _ffn_tf(dff):
    return dff


def kernel(x, norm_w, ffn1_wi, ffn1_wo, ffn2_wi, ffn2_wo, w_in, w_out, mlstm_conv_w, mlstm_conv_b, mlstm_igate_b,
           mlstm_fgate_b, mlstm_norm_w, diff_lambda, diff_norm_w, rel_bias, hgrn_lb_logits, hgrn_norm_w):
    batch, seq, d = x.shape
    depth = norm_w.shape[0]
    tl = _tiles(batch, seq)
    tf = _ffn_tf(ffn1_wo.shape[1])
    act_dtype = BF16

    off = np.concatenate([[0], np.cumsum(SPLIT_SIZES)])
    seg = lambda i: np.arange(off[i], off[i + 1])
    order = np.concatenate([seg(0), seg(1), seg(2), seg(5), seg(6), seg(8), seg(9), seg(10), seg(11),
                            seg(3), seg(4)])
    w_main = jnp.take(w_in, jnp.asarray(order), axis=2)
    w_proj = jnp.pad(w_main, ((0, 0), (0, 0), (0, GATE_PAD - 2 * HEADS))).astype(BF16)
    w_vd_t = jnp.swapaxes(w_in[:, :, off[7]:off[8]], 1, 2).astype(BF16)
    w_gate_t = jnp.swapaxes(w_in[:, :, off[3]:off[5]], 1, 2).astype(BF16)
    gate_b = jnp.concatenate([mlstm_igate_b, mlstm_fgate_b], axis=1).astype(F32)
    gate_b_col = jnp.pad(gate_b, ((0, 0), (0, GATE_PAD - 2 * HEADS)))[:, None, :]
    gate_b_row = gate_b[:, :, None]
    dff = ffn1_wo.shape[1]
    nf = dff // tf
    chunk_wi = lambda w: jnp.swapaxes(w.astype(BF16).reshape(depth, d, 2 * nf, tf), 1, 2)
    chunk_wo = lambda w: w.astype(BF16).reshape(depth, nf, tf, d)
    wi1, wo1 = chunk_wi(ffn1_wi), chunk_wo(ffn1_wo)
    wi2, wo2 = chunk_wi(ffn2_wi), chunk_wo(ffn2_wo)
    w_o = w_out.astype(BF16)
    nw = norm_w.astype(F32)[:, :, None, :]
    conv_w = mlstm_conv_w.astype(F32)
    conv_b = mlstm_conv_b.astype(F32)[:, None, :]
    mnw = mlstm_norm_w.astype(F32)[:, None, :]
    hnw = hgrn_norm_w.astype(F32)[:, None, :]
    dnw_t = diff_norm_w.astype(F32)[:, :, None]
    lam_vecs = diff_lambda.astype(F32)

    lb_all = _hgrn_lower_bounds(hgrn_lb_logits)[:, None, :]
    bias_tiles = _rel_bias_tiles(rel_bias, tl["attn_tile"])

    xf = x.reshape(batch * seq, d)
    for l in range(depth):
        lam0 = jnp.full((1,), 0.8 - 0.6 * math.exp(-0.3 * l), F32)
        xf = _ffn(xf, nw, l, 0, 1, wi1, wo1, tm=tl["ffn_tm"])
        (qm, km, vm, om, gc, gr, qd, kd, vdt, qh, lfh, kh, vh, gh) = _proj(
            xf, nw, l, w_proj, w_vd_t, w_gate_t, conv_w, conv_b, gate_b_col, gate_b_row, lb_all,
            batch=batch, tm=tl["proj_tm"], act_dtype=act_dtype)
        hm = _mlstm(qm, km, vm, gc, gr, batch=batch, chunk=tl["mlstm_chunk"], nb=tl["mlstm_nb"])
        yd = _diff_attn(qd, kd, vdt, bias_tiles, lam0, lam_vecs, dnw_t, l, batch=batch, tile=tl["attn_tile"])
        oh = _hgrn(qh, kh, vh, lfh, batch=batch, tile=tl["hgrn_tile"], nb=tl["hgrn_nb"])
        xf = _mix_out(xf, hm, om, yd, oh, gh, mnw, hnw, w_o, nw, l, tm=tl["out_tm"])
        xf = _ffn(xf, nw, l, 4, 5, wi2, wo2, tm=tl["ffn_tm"])
    return xf.reshape(batch, seq, d)
```

```python
import functools
import math

import numpy as np
import jax
import jax.numpy as jnp
from jax import lax
from jax.experimental import pallas as pl
from jax.experimental.pallas import tpu as pltpu

F32 = jnp.float32
BF16 = jnp.bfloat16
EPS = 1e-6
NEG = -1e30
LOG2E = 1.4426950408889634

HEADS = 4
HEAD_W = 64
M_W = HEADS * HEAD_W
DF_HW = 128
DF_W = HEADS * DF_HW
CONV_W = 4
REL_BUCKETS = 32
REL_MAX_EXACT = 16
REL_MAX_DIST = 128
GATE_PAD = 128
SPLIT_SIZES = (2 * M_W, M_W, M_W, HEADS, HEADS, DF_W, DF_W, DF_W, M_W, M_W, M_W, M_W)
VMEM_LIMIT = 56 * 1024 * 1024


def _cparams(sem):
    return pltpu.CompilerParams(dimension_semantics=sem, vmem_limit_bytes=VMEM_LIMIT)


def _sigmoid(x):
    return 1.0 / (1.0 + jnp.exp(-x))


def _log_sigmoid(x):
    return jnp.minimum(x, 0.0) - jnp.log1p(jnp.exp(-jnp.abs(x)))


def _rms(x, w):
    return x * lax.rsqrt(jnp.mean(x * x, axis=-1, keepdims=True) + EPS) * w


def _split_bf16(x, parts):
    out = []
    r = x
    for _ in range(parts):
        p = r.astype(BF16)
        out.append(p)
        r = r - p.astype(F32)
    return out


def _dot01_rhs(x, m01, parts):
    acc = None
    for p in _split_bf16(x, parts):
        t = jnp.dot(p, m01, preferred_element_type=F32)
        acc = t if acc is None else acc + t
    return acc


def _dot01_lhs(m01, x, parts):
    acc = None
    for p in _split_bf16(x, parts):
        t = jnp.dot(m01, p, preferred_element_type=F32)
        acc = t if acc is None else acc + t
    return acc


def _head_of(idx):
    return lax.shift_right_logical(idx, 6)


def _lb_kernel(lg_ref, o_ref):
    lg = lg_ref[...]
    e = jnp.exp(lg - jnp.max(lg, axis=0, keepdims=True))
    sm = e / jnp.sum(e, axis=0, keepdims=True)
    depth = lg.shape[0]
    rows = []
    run = sm[0:1]
    first = run
    for i in range(depth):
        if i > 0:
            run = run + sm[i:i + 1]
        rows.append(jnp.maximum(run - first, 0.0))
    o_ref[...] = jnp.concatenate(rows, axis=0)


def _hgrn_lower_bounds(logits):
    return pl.pallas_call(
        _lb_kernel, out_shape=jax.ShapeDtypeStruct(logits.shape, F32), name="hgrn_lb")(logits.astype(F32))


def _bias_kernel(tab_ref, o_ref, *, tile):
    h = pl.program_id(0)
    typ = pl.program_id(1)
    r = lax.broadcasted_iota(jnp.int32, (tile, tile), 0)
    c = lax.broadcasted_iota(jnp.int32, (tile, tile), 1)
    rel = c - r + typ * tile
    n = jnp.maximum(rel, 0)
    nf = jnp.maximum(n, 1).astype(F32)
    large = REL_MAX_EXACT + (jnp.log(nf / REL_MAX_EXACT) / math.log(REL_MAX_DIST / REL_MAX_EXACT)
                             * (REL_BUCKETS - REL_MAX_EXACT)).astype(jnp.int32)
    large = jnp.minimum(large, REL_BUCKETS - 1)
    bucket = jnp.where(n < REL_MAX_EXACT, n, large)
    bias = jnp.zeros((tile, tile), F32)
    for b in range(REL_BUCKETS):
        bias = jnp.where(bucket == b, tab_ref[b, h], bias)
    o_ref[...] = jnp.where(rel >= 0, bias * LOG2E, NEG)


def _rel_bias_tiles(rel_bias, tile):
    assert tile >= REL_MAX_DIST
    return pl.pallas_call(
        functools.partial(_bias_kernel, tile=tile),
        grid=(HEADS, 3),
        in_specs=[pl.BlockSpec(memory_space=pltpu.SMEM)],
        out_specs=pl.BlockSpec((None, None, tile, tile), lambda h, t: (h, t, 0, 0)),
        out_shape=jax.ShapeDtypeStruct((HEADS, 3, tile, tile), F32),
        name="rel_bias_tiles",
    )(rel_bias.astype(F32))


def _ffn_kernel(x_ref, nwi_ref, wi_ref, wo_ref, nwo_ref, o_ref):
    dff = wo_ref.shape[0]
    x = x_ref[...]
    xn = _rms(x, nwi_ref[...]).astype(BF16)
    g = jnp.dot(xn, wi_ref[:, 0:dff], preferred_element_type=F32)
    u = jnp.dot(xn, wi_ref[:, dff:2 * dff], preferred_element_type=F32)
    a = (g * _sigmoid(g) * u).astype(BF16)
    h = jnp.dot(a, wo_ref[...], preferred_element_type=F32)
    o_ref[...] = x + 0.5 * _rms(h, nwo_ref[...])


def _ffn(x, nw, layer, row_in, row_out, wi, wo, *, tm):
    m, d = x.shape
    dff = wo.shape[1]
    resident = pl.Buffered(1)
    return pl.pallas_call(
        _ffn_kernel,
        grid=(m // tm,),
        in_specs=[
            pl.BlockSpec((tm, d), lambda i: (i, 0)),
            pl.BlockSpec((None, None, 1, d), lambda i: (layer, row_in, 0, 0)),
            pl.BlockSpec((None, d, 2 * dff), lambda i: (layer, 0, 0), pipeline_mode=resident),
            pl.BlockSpec((None, dff, d), lambda i: (layer, 0, 0), pipeline_mode=resident),
            pl.BlockSpec((None, None, 1, d), lambda i: (layer, row_out, 0, 0)),
        ],
        out_specs=pl.BlockSpec((tm, d), lambda i: (i, 0)),
        out_shape=jax.ShapeDtypeStruct((m, d), F32),
        compiler_params=_cparams(("parallel",)),
        name="ffn",
    )(x, nw, wi, wo, nw)


def _proj_kernel(x_ref, nw_ref, w_ref, wvt_ref, wgt_ref, cw_ref, cb_ref, gbc_ref, gbr_ref, lb_ref,
                 qm_ref, km_ref, vm_ref, om_ref, gc_ref, gr_ref, qd_ref, kd_ref, vdt_ref,
                 qh_ref, lfh_ref, kh_ref, vh_ref, gh_ref, cbuf_ref):
    t = pl.program_id(1)
    tm = x_ref.shape[0]
    xn = _rms(x_ref[...], nw_ref[...]).astype(BF16)

    def u(lo, hi):
        return jnp.dot(xn, w_ref[:, lo:hi], preferred_element_type=F32)

    @pl.when(t == 0)
    def _():
        cbuf_ref[0:8, :] = jnp.zeros((8, 2 * M_W), F32)

    qk = u(0, 2 * M_W)
    cbuf_ref[8:8 + tm, :] = qk
    cw = cw_ref[...]
    y = cb_ref[...] + cw[CONV_W - 1:CONV_W] * qk
    for d in range(1, CONV_W):
        y = y + cw[CONV_W - 1 - d:CONV_W - d] * cbuf_ref[8 - d:8 - d + tm, :]
    cbuf_ref[0:8, :] = cbuf_ref[tm:tm + 8, :]
    y = y * _sigmoid(y)
    qm_ref[...] = y[:, 0:M_W].astype(qm_ref.dtype)
    km_ref[...] = (y[:, M_W:2 * M_W] * (HEAD_W ** -0.5)).astype(km_ref.dtype)
    vm_ref[...] = u(512, 768).astype(vm_ref.dtype)
    om_ref[...] = u(768, 1024)

    qd_ref[...] = (u(1024, 1536) * (HEAD_W ** -0.5 * LOG2E)).astype(qd_ref.dtype)
    kd_ref[...] = u(1536, 2048).astype(kd_ref.dtype)
    vdt_ref[...] = lax.dot_general(wvt_ref[...], xn, (((1,), (1,)), ((), ())),
                                   preferred_element_type=F32).astype(vdt_ref.dtype)

    qh = u(2048, 2304)
    qh_ref[...] = (qh * _sigmoid(qh)).astype(qh_ref.dtype)
    fp = u(2304, 2560)
    lb = lb_ref[...]
    a = jnp.log(lb)
    bb = jnp.log1p(-lb) + _log_sigmoid(fp)
    lfh_ref[...] = jnp.maximum(a, bb) + jnp.log1p(jnp.exp(-jnp.abs(a - bb)))
    kh_ref[...] = ((1.0 - lb) * _sigmoid(-fp)).astype(kh_ref.dtype)
    vh_ref[...] = u(2560, 2816).astype(vh_ref.dtype)
    gh_ref[...] = u(2816, 3072)

    zc = u(3072, 3072 + GATE_PAD) + gbc_ref[...]
    lane = lax.broadcasted_iota(jnp.int32, (1, GATE_PAD), 1)
    gc_ref[...] = jnp.where(lane < HEADS, zc, _log_sigmoid(zc))
    zr = lax.dot_general(wgt_ref[...], xn, (((1,), (1,)), ((), ())), preferred_element_type=F32) + gbr_ref[...]
    row = lax.broadcasted_iota(jnp.int32, (2 * HEADS, 1), 0)
    gr_ref[...] = jnp.where(row < HEADS, zr, _log_sigmoid(zr))


def _proj(x, nw, layer, w, wvt, wgt, cw, cb, gbc, gbr, lb, *, batch, tm, act_dtype):
    m, d = x.shape
    seq = m // batch
    nt = seq // tm
    npc = w.shape[-1]
    tok = lambda width: pl.BlockSpec((tm, width), lambda b, t: (b * nt + t, 0))
    lay = lambda *shape: pl.BlockSpec((None,) + shape, lambda b, t: (layer,) + (0,) * len(shape),
                                      pipeline_mode=pl.Buffered(1))
    sds = lambda width, dt: jax.ShapeDtypeStruct((m, width), dt)
    out_shape = [sds(M_W, act_dtype), sds(M_W, act_dtype), sds(M_W, act_dtype), sds(M_W, F32),
                 sds(GATE_PAD, F32), jax.ShapeDtypeStruct((batch, 2 * HEADS, seq), F32),
                 sds(DF_W, act_dtype), sds(DF_W, act_dtype),
                 jax.ShapeDtypeStruct((batch, nt, DF_W, tm), act_dtype),
                 sds(M_W, act_dtype), sds(M_W, F32), sds(M_W, act_dtype), sds(M_W, act_dtype), sds(M_W, F32)]
    out_specs = [tok(M_W), tok(M_W), tok(M_W), tok(M_W), tok(GATE_PAD),
                 pl.BlockSpec((None, 2 * HEADS, tm), lambda b, t: (b, 0, t)),
                 tok(DF_W), tok(DF_W),
                 pl.BlockSpec((None, None, DF_W, tm), lambda b, t: (b, t, 0, 0)),
                 tok(M_W), tok(M_W), tok(M_W), tok(M_W), tok(M_W)]
    return pl.pallas_call(
        _proj_kernel,
        grid=(batch, nt),
        in_specs=[tok(d),
                  pl.BlockSpec((None, None, 1, d), lambda b, t: (layer, 2, 0, 0)),
                  lay(d, npc), lay(DF_W, d), lay(2 * HEADS, d), lay(CONV_W, 2 * M_W), lay(1, 2 * M_W),
                  lay(1, GATE_PAD), lay(2 * HEADS, 1), lay(1, M_W)],
        out_specs=out_specs,
        out_shape=out_shape,
        scratch_shapes=[pltpu.VMEM((tm + 8, 2 * M_W), F32)],
        compiler_params=_cparams(("parallel", "arbitrary")),
        name="mixer_proj",
    )(x, nw, w, wvt, wgt, cw, cb, gbc, gbr, lb)


def _mlstm_kernel(q_ref, k_ref, v_ref, gc_ref, gr_ref, h_ref, c_ref, n_ref, m_ref):
    @pl.when(pl.program_id(1) == 0)
    def _():
        c_ref[...] = jnp.zeros_like(c_ref)
        n_ref[...] = jnp.zeros_like(n_ref)
        m_ref[...] = jnp.zeros_like(m_ref)

    for sq in range(q_ref.shape[0]):
        _mlstm_chunk(*(r.at[sq] for r in (q_ref, k_ref, v_ref, gc_ref, gr_ref, h_ref, c_ref, n_ref, m_ref)))


def _mlstm_chunk(q_ref, k_ref, v_ref, gc_ref, gr_ref, h_ref, c_ref, n_ref, m_ref):
    L = q_ref.shape[0]
    q = q_ref[...].astype(F32)
    k = k_ref[...].astype(F32)
    vb = v_ref[...].astype(BF16)
    kb = k.astype(BF16)
    gc = gc_ref[...]
    gr = gr_ref[...]
    row = lax.broadcasted_iota(jnp.int32, (L, L), 0)
    col = lax.broadcasted_iota(jnp.int32, (L, L), 1)
    causal = row >= col
    bcol = _dot01_rhs_t(causal, gc)
    brow = _dot01_rhs(gr, (row <= col).astype(BF16), 3)
    lane_head = _head_of(lax.broadcasted_iota(jnp.int32, (1, M_W), 1))
    cbd = c_ref[...]
    nrow = n_ref[...]
    qc = jnp.dot(q.astype(BF16), cbd.astype(BF16), preferred_element_type=F32)
    hout = jnp.zeros((L, M_W), F32)
    wa_all = jnp.zeros((L, M_W), F32)
    dec_all = jnp.zeros((1, M_W), F32)
    for h in range(HEADS):
        hm = lane_head == h
        bt = bcol[:, HEADS + h:HEADS + h + 1]
        li_c = gc[:, h:h + 1]
        rs = gr[h:h + 1, :] - brow[HEADS + h:HEADS + h + 1, :]
        mprev = m_ref[h:h + 1, 0:1]
        dmat = jnp.where(causal, bt + rs, -jnp.inf)
        m_inter = bt + mprev
        mt = jnp.maximum(jnp.max(dmat, axis=1, keepdims=True), m_inter)
        w = jnp.exp(dmat - mt)
        qh = jnp.where(hm, q, 0.0).astype(BF16)
        sc = lax.dot_general(qh, kb, (((1,), (1,)), ((), ())), preferred_element_type=F32) * w
        g = jnp.exp(m_inter - mt)
        pv = jnp.dot(sc.astype(BF16), vb, preferred_element_type=F32)
        qn = jnp.sum(jnp.where(hm, q * nrow, 0.0), axis=1, keepdims=True)
        den = jnp.sum(sc, axis=1, keepdims=True) + g * qn
        scale = 1.0 / jnp.maximum(jnp.abs(den), jnp.exp(-mt))
        hout = jnp.where(hm, (pv + g * qc) * scale, hout)
        bl = bt[L - 1:L, :]
        a = bl - bt + li_c
        mnew = jnp.maximum(bl + mprev, jnp.max(a, axis=0, keepdims=True))
        wa_all = jnp.where(hm, jnp.exp(a - mnew), wa_all)
        dec_all = jnp.where(hm, jnp.exp(bl + mprev - mnew), dec_all)
        m_ref[h:h + 1, :] = jnp.broadcast_to(mnew, (1, m_ref.shape[1]))
    h_ref[...] = hout
    kw = k * wa_all
    cnew = lax.dot_general(kw.astype(BF16), vb, (((0,), (0,)), ((), ())), preferred_element_type=F32)
    r2 = _head_of(lax.broadcasted_iota(jnp.int32, (M_W, M_W), 0))
    c2 = _head_of(lax.broadcasted_iota(jnp.int32, (M_W, M_W), 1))
    c_ref[...] = dec_all * cbd + jnp.where(r2 == c2, cnew, 0.0)
    n_ref[...] = dec_all * nrow + jnp.sum(kw, axis=0, keepdims=True)


def _dot01_rhs_t(mask, x):
    return _dot01_lhs(mask.astype(BF16), x, 3)


def _mlstm(qm, km, vm, gc, gr, *, batch, chunk, nb):
    m = qm.shape[0]
    seq = m // batch
    view = lambda a: a.reshape(batch, seq, a.shape[-1])
    tok = lambda width: pl.BlockSpec((nb, chunk, width), lambda b, c: (b, c, 0))
    out = pl.pallas_call(
        _mlstm_kernel,
        grid=(batch // nb, seq // chunk),
        in_specs=[tok(M_W), tok(M_W), tok(M_W), tok(GATE_PAD),
                  pl.BlockSpec((nb, 2 * HEADS, chunk), lambda b, c: (b, 0, c))],
        out_specs=tok(M_W),
        out_shape=jax.ShapeDtypeStruct((batch, seq, M_W), F32),
        scratch_shapes=[pltpu.VMEM((nb, M_W, M_W), F32), pltpu.VMEM((nb, 1, M_W), F32),
                        pltpu.VMEM((nb, 8, 128), F32)],
        compiler_params=_cparams(("parallel", "arbitrary")),
        name="mlstm",
    )(view(qm), view(km), view(vm), view(gc), gr)
    return out.reshape(m, M_W)


HG_BLK = 16


def _hgrn_kernel(q_ref, k_ref, v_ref, lf_ref, o_ref, st_ref):
    @pl.when(pl.program_id(1) == 0)
    def _():
        st_ref[...] = jnp.zeros_like(st_ref)

    nb = q_ref.shape[0]
    nblk = q_ref.shape[1] // HG_BLK
    r16 = lax.broadcasted_iota(jnp.int32, (HG_BLK, HG_BLK), 0)
    c16 = lax.broadcasted_iota(jnp.int32, (HG_BLK, HG_BLK), 1)
    tri16 = (r16 >= c16).astype(BF16)
    rowid = lax.broadcasted_iota(jnp.int32, (HG_BLK, M_W), 0)
    r2 = _head_of(lax.broadcasted_iota(jnp.int32, (M_W, M_W), 0))
    c2 = _head_of(lax.broadcasted_iota(jnp.int32, (M_W, M_W), 1))
    same_head = r2 == c2
    ones_bd = same_head.astype(BF16)

    def block(i, sq):
        r0 = pl.multiple_of(i * HG_BLK, HG_BLK)
        q = q_ref[sq, pl.ds(r0, HG_BLK), :].astype(F32)
        k = k_ref[sq, pl.ds(r0, HG_BLK), :].astype(F32)
        v = v_ref[sq, pl.ds(r0, HG_BLK), :].astype(F32)
        lf = lf_ref[sq, pl.ds(r0, HG_BLK), :]
        b = _dot01_lhs(tri16, lf, 3)
        bl = b[HG_BLK - 1:HG_BLK, :]
        st = st_ref[sq]
        o = lax.dot_general((q * jnp.exp(b)).astype(BF16), st.astype(BF16), (((1,), (1,)), ((), ())),
                            preferred_element_type=F32)
        ps = []
        for s in range(HG_BLK):
            dd = jnp.where(rowid >= s, b - b[s:s + 1, :], -jnp.inf)
            ps.append(q * k[s:s + 1, :] * jnp.exp(dd))
        p = jnp.concatenate(ps, axis=0).astype(BF16)
        abig = jnp.dot(p, ones_bd, preferred_element_type=F32)
        for s in range(HG_BLK):
            o = o + abig[s * HG_BLK:(s + 1) * HG_BLK, :] * v[s:s + 1, :]
        o_ref[sq, pl.ds(r0, HG_BLK), :] = o
        ke = (k * jnp.exp(bl - b)).astype(BF16)
        upd = lax.dot_general(v.astype(BF16), ke, (((0,), (0,)), ((), ())), preferred_element_type=F32)
        st_ref[sq] = st * jnp.exp(bl) + jnp.where(same_head, upd, 0.0)

    def body(i, carry):
        for sq in range(nb):
            block(i, sq)
        return carry

    lax.fori_loop(0, nblk, body, 0)


def _hgrn(qh, kh, vh, lfh, *, batch, tile, nb):
    m = qh.shape[0]
    seq = m // batch
    view = lambda a: a.reshape(batch, seq, M_W)
    tok = pl.BlockSpec((nb, tile, M_W), lambda b, t: (b, t, 0))
    out = pl.pallas_call(
        _hgrn_kernel,
        grid=(batch // nb, seq // tile),
        in_specs=[tok, tok, tok, tok],
        out_specs=tok,
        out_shape=jax.ShapeDtypeStruct((batch, seq, M_W), F32),
        scratch_shapes=[pltpu.VMEM((nb, M_W, M_W), F32)],
        compiler_params=_cparams(("parallel", "arbitrary")),
        name="hgrn2",
    )(view(qh), view(kh), view(vh), view(lfh))
    return out.reshape(m, M_W)


ONES_ROWS = 16


def _attn_kernel(q_ref, k_ref, vt_ref, bias_ref, lam0_ref, lv_ref, nwt_ref, o_ref, m_ref, acc_ref, sa_ref, sb_ref,
                 mca_ref, mcb_ref, *, tile):
    qi = pl.program_id(2)
    vchunk = vt_ref.shape[2]
    nvc = tile // vchunk
    m_ref[...] = jnp.full_like(m_ref, NEG)
    acc_ref[...] = jnp.zeros_like(acc_ref)
    q = q_ref[...]
    first = lax.broadcasted_iota(jnp.int32, (1, DF_HW), 1) < HEAD_W
    zero = jnp.zeros_like(q)
    qs = (jnp.where(first, q, zero), jnp.where(first, zero, q))
    ones = jnp.ones((ONES_ROWS, vchunk), BF16)

    def scores(ki, s_ref, mc_ref):
        kb = k_ref[pl.ds(pl.multiple_of(ki * tile, tile), tile), :]
        bias = bias_ref[jnp.minimum(qi - ki, 2)]
        for j in range(2):
            st = bias + lax.dot_general(kb, qs[j], (((1,), (1,)), ((), ())), preferred_element_type=F32)
            s_ref[j] = st
            mc_ref[j] = jnp.max(st, axis=0, keepdims=True)

    def softmax_pv(ki, s_ref, mc_ref):
        vts = [jnp.concatenate([vt_ref[ki * nvc + c], ones], axis=0) for c in range(nvc)]
        for j in range(2):
            m_old = m_ref[j]
            m_new = jnp.maximum(m_old, mc_ref[j])
            alpha = jnp.exp2(m_old - m_new)
            pt = jnp.exp2((s_ref[j] - m_new).astype(BF16))
            pv = None
            for c in range(nvc):
                t = jnp.dot(vts[c], pt[c * vchunk:(c + 1) * vchunk, :], preferred_element_type=F32)
                pv = t if pv is None else pv + t
            acc_ref[j] = alpha * acc_ref[j] + pv
            m_ref[j] = m_new

    ntiles = qi + 1
    scores(0, sa_ref, mca_ref)

    def pair(p, carry):
        ka = 2 * p
        scores(ka + 1, sb_ref, mcb_ref)
        softmax_pv(ka, sa_ref, mca_ref)
        scores(jnp.minimum(ka + 2, qi), sa_ref, mca_ref)
        softmax_pv(ka + 1, sb_ref, mcb_ref)
        return carry

    lax.fori_loop(0, ntiles // 2, pair, 0)

    @pl.when(ntiles % 2 == 1)
    def _():
        softmax_pv(qi, sa_ref, mca_ref)

    lv = lv_ref[...]
    lam0 = lam0_ref[0]
    lam = (jnp.exp(jnp.sum(lv[0:1] * lv[1:2], axis=1, keepdims=True))
           - jnp.exp(jnp.sum(lv[2:3] * lv[3:4], axis=1, keepdims=True)) + lam0)
    a1 = acc_ref[0]
    a2 = acc_ref[1]
    od = a1[0:DF_HW] / a1[DF_HW:DF_HW + 1] - lam * (a2[0:DF_HW] / a2[DF_HW:DF_HW + 1])
    ms = jnp.mean(od * od, axis=0, keepdims=True)
    yt = od * lax.rsqrt(ms + EPS) * nwt_ref[...] * (1.0 - lam0)
    o_ref[...] = yt.T.astype(o_ref.dtype)


def _diff_attn(qd, kd, vdt, bias_tiles, lam0, lam_vecs, norm_w_t, layer, *, batch, tile):
    m = qd.shape[0]
    seq = m // batch
    nq = seq // tile
    nchunk, vchunk = vdt.shape[1], vdt.shape[3]
    return pl.pallas_call(
        functools.partial(_attn_kernel, tile=tile),
        grid=(batch, HEADS, nq),
        in_specs=[
            pl.BlockSpec((tile, DF_HW), lambda b, h, i: (b * nq + i, h)),
            pl.BlockSpec((seq, DF_HW), lambda b, h, i: (b, h)),
            pl.BlockSpec((None, nchunk, DF_HW, vchunk), lambda b, h, i: (b, 0, h, 0)),
            pl.BlockSpec((None, 3, tile, tile), lambda b, h, i: (h, 0, 0, 0)),
            pl.BlockSpec(memory_space=pltpu.SMEM),
            pl.BlockSpec((None, 4, HEAD_W), lambda b, h, i: (layer, 0, 0)),
            pl.BlockSpec((None, DF_HW, 1), lambda b, h, i: (layer, h, 0)),
        ],
        out_specs=pl.BlockSpec((tile, DF_HW), lambda b, h, i: (b * nq + i, h)),
        out_shape=jax.ShapeDtypeStruct((m, DF_W), BF16),
        scratch_shapes=[pltpu.VMEM((2, 1, tile), F32), pltpu.VMEM((2, DF_HW + ONES_ROWS, tile), F32),
                        pltpu.VMEM((2, tile, tile), F32), pltpu.VMEM((2, tile, tile), F32),
                        pltpu.VMEM((2, 1, tile), F32), pltpu.VMEM((2, 1, tile), F32)],
        compiler_params=_cparams(("parallel", "parallel", "arbitrary")),
        name="diff_attn",
    )(qd, kd, vdt, bias_tiles, lam0, lam_vecs, norm_w_t)


def _mix_out_kernel(x_ref, hm_ref, om_ref, yd_ref, oh_ref, gh_ref, mnw_ref, hnw_ref, w_ref, nw_ref, o_ref):
    r2 = _head_of(lax.broadcasted_iota(jnp.int32, (M_W, M_W), 0))
    c2 = _head_of(lax.broadcasted_iota(jnp.int32, (M_W, M_W), 1))
    ones_bd = (r2 == c2).astype(BF16)
    inv = 1.0 / HEAD_W

    hm = hm_ref[...]
    xc = hm - _dot01_rhs(hm, ones_bd, 2) * inv
    var = _dot01_rhs(xc * xc, ones_bd, 2) * inv
    ym = xc * lax.rsqrt(var + EPS) * mnw_ref[...] * _sigmoid(om_ref[...])

    oh = oh_ref[...]
    ms = _dot01_rhs(oh * oh, ones_bd, 2) * inv
    gh = gh_ref[...]
    yh = oh * lax.rsqrt(ms + EPS) * hnw_ref[...] * (gh * _sigmoid(gh))

    w = w_ref[...]
    acc = jnp.dot(ym.astype(BF16), w[0:M_W], preferred_element_type=F32)
    acc = acc + jnp.dot(yd_ref[...].astype(BF16), w[M_W:M_W + DF_W], preferred_element_type=F32)
    acc = acc + jnp.dot(yh.astype(BF16), w[M_W + DF_W:], preferred_element_type=F32)
    o_ref[...] = x_ref[...] + _rms(acc, nw_ref[...])


def _mix_out(x, hm, om, yd, oh, gh, mnw, hnw, w_out, nw, layer, *, tm):
    m, d = x.shape
    tok = lambda width: pl.BlockSpec((tm, width), lambda i: (i, 0))
    lay = lambda *shape: pl.BlockSpec((None,) + shape, lambda i: (layer,) + (0,) * len(shape))
    return pl.pallas_call(
        _mix_out_kernel,
        grid=(m // tm,),
        in_specs=[tok(d), tok(M_W), tok(M_W), tok(DF_W), tok(M_W), tok(M_W),
                  lay(1, M_W), lay(1, M_W), lay(d, d),
                  pl.BlockSpec((None, None, 1, d), lambda i: (layer, 3, 0, 0))],
        out_specs=tok(d),
        out_shape=jax.ShapeDtypeStruct((m, d), F32),
        compiler_params=_cparams(("parallel",)),
        name="mix_out",
    )(x, hm, om, yd, oh, gh, mnw, hnw, w_out, nw)


def _tiles(batch, seq):
    return dict(
        ffn_tm=min(512, seq), proj_tm=min(512, seq), mlstm_chunk=min(256, seq), hgrn_tile=min(256, seq),
        attn_tile=min(512, seq), out_tm=min(512, seq), hgrn_nb=8 if batch % 8 == 0 else 1,
        mlstm_nb=2 if batch % 2 == 0 else 1)


def kernel(x, norm_w, ffn1_wi, ffn1_wo, ffn2_wi, ffn2_wo, w_in, w_out, mlstm_conv_w, mlstm_conv_b, mlstm_igate_b,
           mlstm_fgate_b, mlstm_norm_w, diff_lambda, diff_norm_w, rel_bias, hgrn_lb_logits, hgrn_norm_w):
    batch, seq, d = x.shape
    depth = norm_w.shape[0]
    tl = _tiles(batch, seq)
    act_dtype = BF16

    off = [0] + [int(v) for v in np.cumsum(SPLIT_SIZES)]
    seg = lambda i: w_in[:, :, off[i]:off[i + 1]].astype(BF16)
    gate_pad = jnp.zeros(w_in.shape[:2] + (GATE_PAD - 2 * HEADS,), BF16)
    w_proj = jnp.concatenate([seg(0), seg(1), seg(2), seg(5), seg(6), seg(8), seg(9), seg(10), seg(11),
                              seg(3), seg(4), gate_pad], axis=2)
    w_vd_t = jnp.swapaxes(w_in[:, :, off[7]:off[8]], 1, 2).astype(BF16)
    w_gate_t = jnp.swapaxes(w_in[:, :, off[3]:off[5]], 1, 2).astype(BF16)
    gate_b = jnp.concatenate([mlstm_igate_b, mlstm_fgate_b], axis=1).astype(F32)
    gate_b_col = jnp.pad(gate_b, ((0, 0), (0, GATE_PAD - 2 * HEADS)))[:, None, :]
    gate_b_row = gate_b[:, :, None]
    wi1, wo1 = ffn1_wi.astype(BF16), ffn1_wo.astype(BF16)
    wi2, wo2 = ffn2_wi.astype(BF16), ffn2_wo.astype(BF16)
    w_o = w_out.astype(BF16)
    nw = norm_w.astype(F32)[:, :, None, :]
    conv_w = mlstm_conv_w.astype(F32)
    conv_b = mlstm_conv_b.astype(F32)[:, None, :]
    mnw = mlstm_norm_w.astype(F32)[:, None, :]
    hnw = hgrn_norm_w.astype(F32)[:, None, :]
    dnw_t = diff_norm_w.astype(F32)[:, :, None]
    lam_vecs = diff_lambda.astype(F32)

    lb_all = _hgrn_lower_bounds(hgrn_lb_logits)[:, None, :]
    bias_tiles = _rel_bias_tiles(rel_bias, tl["attn_tile"])

    xf = x.reshape(batch * seq, d)
    for l in range(depth):
        lam0 = jnp.full((1,), 0.8 - 0.6 * math.exp(-0.3 * l), F32)
        xf = _ffn(xf, nw, l, 0, 1, wi1, wo1, tm=tl["ffn_tm"])
        (qm, km, vm, om, gc, gr, qd, kd, vdt, qh, lfh, kh, vh, gh) = _proj(
            xf, nw, l, w_proj, w_vd_t, w_gate_t, conv_w, conv_b, gate_b_col, gate_b_row, lb_all,
            batch=batch, tm=tl["proj_tm"], act_dtype=act_dtype)
        hm = _mlstm(qm, km, vm, gc, gr, batch=batch, chunk=tl["mlstm_chunk"], nb=tl["mlstm_nb"])
        yd = _diff_attn(qd, kd, vdt, bias_tiles, lam0, lam_vecs, dnw_t, l, batch=batch, tile=tl["attn_tile"])
        oh = _hgrn(qh, kh, vh, lfh, batch=batch, tile=tl["hgrn_tile"], nb=tl["hgrn_nb"])
        xf = _mix_out(xf, hm, om, yd, oh, gh, mnw, hnw, w_o, nw, l, tm=tl["out_tm"])
        xf = _ffn(xf, nw, l, 4, 5, wi2, wo2, tm=tl["ffn_tm"])
    return xf.reshape(batch, seq, d)
```

```python
import functools
import math

import numpy as np
import jax
import jax.numpy as jnp
from jax import lax
from jax.experimental import pallas as pl
from jax.experimental.pallas import tpu as pltpu

F32 = jnp.float32
BF16 = jnp.bfloat16
EPS = 1e-6
NEG = -1e30
LOG2E = 1.4426950408889634

HEADS = 4
HEAD_W = 64
M_W = HEADS * HEAD_W
DF_HW = 128
DF_W = HEADS * DF_HW
CONV_W = 4
REL_BUCKETS = 32
REL_MAX_EXACT = 16
REL_MAX_DIST = 128
GATE_PAD = 128
SPLIT_SIZES = (2 * M_W, M_W, M_W, HEADS, HEADS, DF_W, DF_W, DF_W, M_W, M_W, M_W, M_W)
VMEM_LIMIT = 56 * 1024 * 1024


def _cparams(sem):
    return pltpu.CompilerParams(dimension_semantics=sem, vmem_limit_bytes=VMEM_LIMIT)


def _sigmoid(x):
    return 1.0 / (1.0 + jnp.exp(-x))


def _log_sigmoid(x):
    return jnp.minimum(x, 0.0) - jnp.log1p(jnp.exp(-jnp.abs(x)))


def _rms(x, w):
    return x * lax.rsqrt(jnp.mean(x * x, axis=-1, keepdims=True) + EPS) * w


def _split_bf16(x, parts):
    out = []
    r = x
    for _ in range(parts):
        p = r.astype(BF16)
        out.append(p)
        r = r - p.astype(F32)
    return out


def _dot01_rhs(x, m01, parts):
    acc = None
    for p in _split_bf16(x, parts):
        t = jnp.dot(p, m01, preferred_element_type=F32)
        acc = t if acc is None else acc + t
    return acc


def _dot01_lhs(m01, x, parts):
    acc = None
    for p in _split_bf16(x, parts):
        t = jnp.dot(m01, p, preferred_element_type=F32)
        acc = t if acc is None else acc + t
    return acc


def _head_of(idx):
    return lax.shift_right_logical(idx, 6)


def _lb_kernel(lg_ref, o_ref):
    lg = lg_ref[...]
    e = jnp.exp(lg - jnp.max(lg, axis=0, keepdims=True))
    sm = e / jnp.sum(e, axis=0, keepdims=True)
    depth = lg.shape[0]
    rows = []
    run = sm[0:1]
    first = run
    for i in range(depth):
        if i > 0:
            run = run + sm[i:i + 1]
        rows.append(jnp.maximum(run - first, 0.0))
    o_ref[...] = jnp.concatenate(rows, axis=0)


def _hgrn_lower_bounds(logits):
    return pl.pallas_call(
        _lb_kernel, out_shape=jax.ShapeDtypeStruct(logits.shape, F32), name="hgrn_lb")(logits.astype(F32))


def _bias_kernel(tab_ref, o_ref, *, tile):
    h = pl.program_id(0)
    typ = pl.program_id(1)
    r = lax.broadcasted_iota(jnp.int32, (tile, tile), 0)
    c = lax.broadcasted_iota(jnp.int32, (tile, tile), 1)
    rel = c - r + typ * tile
    n = jnp.maximum(rel, 0)
    nf = jnp.maximum(n, 1).astype(F32)
    large = REL_MAX_EXACT + (jnp.log(nf / REL_MAX_EXACT) / math.log(REL_MAX_DIST / REL_MAX_EXACT)
                             * (REL_BUCKETS - REL_MAX_EXACT)).astype(jnp.int32)
    large = jnp.minimum(large, REL_BUCKETS - 1)
    bucket = jnp.where(n < REL_MAX_EXACT, n, large)
    bias = jnp.zeros((tile, tile), F32)
    for b in range(REL_BUCKETS):
        bias = jnp.where(bucket == b, tab_ref[b, h], bias)
    o_ref[...] = jnp.where(rel >= 0, bias * LOG2E, NEG)


def _rel_bias_tiles(rel_bias, tile):
    assert tile >= REL_MAX_DIST
    return pl.pallas_call(
        functools.partial(_bias_kernel, tile=tile),
        grid=(HEADS, 3),
        in_specs=[pl.BlockSpec(memory_space=pltpu.SMEM)],
        out_specs=pl.BlockSpec((None, None, tile, tile), lambda h, t: (h, t, 0, 0)),
        out_shape=jax.ShapeDtypeStruct((HEADS, 3, tile, tile), F32),
        name="rel_bias_tiles",
    )(rel_bias.astype(F32))


def _ffn_kernel(x_ref, nwi_ref, wi_ref, wo_ref, nwo_ref, o_ref):
    dff = wo_ref.shape[0]
    x = x_ref[...]
    xn = _rms(x, nwi_ref[...]).astype(BF16)
    g = jnp.dot(xn, wi_ref[:, 0:dff], preferred_element_type=F32)
    u = jnp.dot(xn, wi_ref[:, dff:2 * dff], preferred_element_type=F32)
    a = (g * _sigmoid(g) * u).astype(BF16)
    h = jnp.dot(a, wo_ref[...], preferred_element_type=F32)
    o_ref[...] = x + 0.5 * _rms(h, nwo_ref[...])


def _ffn(x, nw, layer, row_in, row_out, wi, wo, *, tm):
    m, d = x.shape
    dff = wo.shape[1]
    resident = pl.Buffered(1)
    return pl.pallas_call(
        _ffn_kernel,
        grid=(m // tm,),
        in_specs=[
            pl.BlockSpec((tm, d), lambda i: (i, 0)),
            pl.BlockSpec((None, None, 1, d), lambda i: (layer, row_in, 0, 0)),
            pl.BlockSpec((None, d, 2 * dff), lambda i: (layer, 0, 0), pipeline_mode=resident),
            pl.BlockSpec((None, dff, d), lambda i: (layer, 0, 0), pipeline_mode=resident),
            pl.BlockSpec((None, None, 1, d), lambda i: (layer, row_out, 0, 0)),
        ],
        out_specs=pl.BlockSpec((tm, d), lambda i: (i, 0)),
        out_shape=jax.ShapeDtypeStruct((m, d), F32),
        compiler_params=_cparams(("parallel",)),
        name="ffn",
    )(x, nw, wi, wo, nw)


def _proj_kernel(x_ref, nw_ref, w_ref, wvt_ref, wgt_ref, cw_ref, cb_ref, gbc_ref, gbr_ref, lb_ref,
                 qm_ref, km_ref, vm_ref, om_ref, gc_ref, gr_ref, qd_ref, kd_ref, vdt_ref,
                 qh_ref, lfh_ref, kh_ref, vh_ref, gh_ref, cbuf_ref):
    t = pl.program_id(1)
    tm = x_ref.shape[0]
    xn = _rms(x_ref[...], nw_ref[...]).astype(BF16)

    def u(lo, hi):
        return jnp.dot(xn, w_ref[:, lo:hi], preferred_element_type=F32)

    @pl.when(t == 0)
    def _():
        cbuf_ref[0:8, :] = jnp.zeros((8, 2 * M_W), F32)

    qk = u(0, 2 * M_W)
    cbuf_ref[8:8 + tm, :] = qk
    cw = cw_ref[...]
    y = cb_ref[...] + cw[CONV_W - 1:CONV_W] * qk
    for d in range(1, CONV_W):
        y = y + cw[CONV_W - 1 - d:CONV_W - d] * cbuf_ref[8 - d:8 - d + tm, :]
    cbuf_ref[0:8, :] = cbuf_ref[tm:tm + 8, :]
    y = y * _sigmoid(y)
    qm_ref[...] = y[:, 0:M_W].astype(qm_ref.dtype)
    km_ref[...] = (y[:, M_W:2 * M_W] * (HEAD_W ** -0.5)).astype(km_ref.dtype)
    vm_ref[...] = u(512, 768).astype(vm_ref.dtype)
    om_ref[...] = u(768, 1024)

    qd_ref[...] = (u(1024, 1536) * (HEAD_W ** -0.5 * LOG2E)).astype(qd_ref.dtype)
    kd_ref[...] = u(1536, 2048).astype(kd_ref.dtype)
    vdt_ref[...] = lax.dot_general(wvt_ref[...], xn, (((1,), (1,)), ((), ())),
                                   preferred_element_type=F32).astype(vdt_ref.dtype)

    qh = u(2048, 2304)
    qh_ref[...] = (qh * _sigmoid(qh)).astype(qh_ref.dtype)
    fp = u(2304, 2560)
    lb = lb_ref[...]
    a = jnp.log(lb)
    bb = jnp.log1p(-lb) + _log_sigmoid(fp)
    lfh_ref[...] = jnp.maximum(a, bb) + jnp.log1p(jnp.exp(-jnp.abs(a - bb)))
    kh_ref[...] = ((1.0 - lb) * _sigmoid(-fp)).astype(kh_ref.dtype)
    vh_ref[...] = u(2560, 2816).astype(vh_ref.dtype)
    gh_ref[...] = u(2816, 3072)

    zc = u(3072, 3072 + GATE_PAD) + gbc_ref[...]
    lane = lax.broadcasted_iota(jnp.int32, (1, GATE_PAD), 1)
    gc_ref[...] = jnp.where(lane < HEADS, zc, _log_sigmoid(zc))
    zr = lax.dot_general(wgt_ref[...], xn, (((1,), (1,)), ((), ())), preferred_element_type=F32) + gbr_ref[...]
    row = lax.broadcasted_iota(jnp.int32, (2 * HEADS, 1), 0)
    gr_ref[...] = jnp.where(row < HEADS, zr, _log_sigmoid(zr))


def _proj(x, nw, layer, w, wvt, wgt, cw, cb, gbc, gbr, lb, *, batch, tm, act_dtype):
    m, d = x.shape
    seq = m // batch
    nt = seq // tm
    npc = w.shape[-1]
    tok = lambda width: pl.BlockSpec((tm, width), lambda b, t: (b * nt + t, 0))
    lay = lambda *shape: pl.BlockSpec((None,) + shape, lambda b, t: (layer,) + (0,) * len(shape),
                                      pipeline_mode=pl.Buffered(1))
    sds = lambda width, dt: jax.ShapeDtypeStruct((m, width), dt)
    out_shape = [sds(M_W, act_dtype), sds(M_W, act_dtype), sds(M_W, act_dtype), sds(M_W, F32),
                 sds(GATE_PAD, F32), jax.ShapeDtypeStruct((batch, 2 * HEADS, seq), F32),
                 sds(DF_W, act_dtype), sds(DF_W, act_dtype),
                 jax.ShapeDtypeStruct((batch, nt, DF_W, tm), act_dtype),
                 sds(M_W, act_dtype), sds(M_W, F32), sds(M_W, act_dtype), sds(M_W, act_dtype), sds(M_W, F32)]
    out_specs = [tok(M_W), tok(M_W), tok(M_W), tok(M_W), tok(GATE_PAD),
                 pl.BlockSpec((None, 2 * HEADS, tm), lambda b, t: (b, 0, t)),
                 tok(DF_W), tok(DF_W),
                 pl.BlockSpec((None, None, DF_W, tm), lambda b, t: (b, t, 0, 0)),
                 tok(M_W), tok(M_W), tok(M_W), tok(M_W), tok(M_W)]
    return pl.pallas_call(
        _proj_kernel,
        grid=(batch, nt),
        in_specs=[tok(d),
                  pl.BlockSpec((None, None, 1, d), lambda b, t: (layer, 2, 0, 0)),
                  lay(d, npc), lay(DF_W, d), lay(2 * HEADS, d), lay(CONV_W, 2 * M_W), lay(1, 2 * M_W),
                  lay(1, GATE_PAD), lay(2 * HEADS, 1), lay(1, M_W)],
        out_specs=out_specs,
        out_shape=out_shape,
        scratch_shapes=[pltpu.VMEM((tm + 8, 2 * M_W), F32)],
        compiler_params=_cparams(("parallel", "arbitrary")),
        name="mixer_proj",
    )(x, nw, w, wvt, wgt, cw, cb, gbc, gbr, lb)


def _mlstm_kernel(q_ref, k_ref, v_ref, gc_ref, gr_ref, h_ref, c_ref, n_ref, m_ref):
    @pl.when(pl.program_id(1) == 0)
    def _():
        c_ref[...] = jnp.zeros_like(c_ref)
        n_ref[...] = jnp.zeros_like(n_ref)
        m_ref[...] = jnp.zeros_like(m_ref)

    for sq in range(q_ref.shape[0]):
        _mlstm_chunk(*(r.at[sq] for r in (q_ref, k_ref, v_ref, gc_ref, gr_ref, h_ref, c_ref, n_ref, m_ref)))


def _mlstm_chunk(q_ref, k_ref, v_ref, gc_ref, gr_ref, h_ref, c_ref, n_ref, m_ref):
    L = q_ref.shape[0]
    q = q_ref[...].astype(F32)
    k = k_ref[...].astype(F32)
    vb = v_ref[...].astype(BF16)
    kb = k.astype(BF16)
    gc = gc_ref[...]
    gr = gr_ref[...]
    row = lax.broadcasted_iota(jnp.int32, (L, L), 0)
    col = lax.broadcasted_iota(jnp.int32, (L, L), 1)
    causal = row >= col
    bcol = _dot01_rhs_t(causal, gc)
    brow = _dot01_rhs(gr, (row <= col).astype(BF16), 3)
    lane_head = _head_of(lax.broadcasted_iota(jnp.int32, (1, M_W), 1))
    cbd = c_ref[...]
    nrow = n_ref[...]
    qc = jnp.dot(q.astype(BF16), cbd.astype(BF16), preferred_element_type=F32)
    hout = jnp.zeros((L, M_W), F32)
    wa_all = jnp.zeros((L, M_W), F32)
    dec_all = jnp.zeros((1, M_W), F32)
    for h in range(HEADS):
        hm = lane_head == h
        bt = bcol[:, HEADS + h:HEADS + h + 1]
        li_c = gc[:, h:h + 1]
        rs = gr[h:h + 1, :] - brow[HEADS + h:HEADS + h + 1, :]
        mprev = m_ref[h:h + 1, 0:1]
        dmat = jnp.where(causal, bt + rs, -jnp.inf)
        m_inter = bt + mprev
        mt = jnp.maximum(jnp.max(dmat, axis=1, keepdims=True), m_inter)
        w = jnp.exp(dmat - mt)
        qh = jnp.where(hm, q, 0.0).astype(BF16)
        sc = lax.dot_general(qh, kb, (((1,), (1,)), ((), ())), preferred_element_type=F32) * w
        g = jnp.exp(m_inter - mt)
        pv = jnp.dot(sc.astype(BF16), vb, preferred_element_type=F32)
        qn = jnp.sum(jnp.where(hm, q * nrow, 0.0), axis=1, keepdims=True)
        den = jnp.sum(sc, axis=1, keepdims=True) + g * qn
        scale = 1.0 / jnp.maximum(jnp.abs(den), jnp.exp(-mt))
        hout = jnp.where(hm, (pv + g * qc) * scale, hout)
        bl = bt[L - 1:L, :]
        a = bl - bt + li_c
        mnew = jnp.maximum(bl + mprev, jnp.max(a, axis=0, keepdims=True))
        wa_all = jnp.where(hm, jnp.exp(a - mnew), wa_all)
        dec_all = jnp.where(hm, jnp.exp(bl + mprev - mnew), dec_all)
        m_ref[h:h + 1, :] = jnp.broadcast_to(mnew, (1, m_ref.shape[1]))
    h_ref[...] = hout
    kw = k * wa_all
    cnew = lax.dot_general(kw.astype(BF16), vb, (((0,), (0,)), ((), ())), preferred_element_type=F32)
    r2 = _head_of(lax.broadcasted_iota(jnp.int32, (M_W, M_W), 0))
    c2 = _head_of(lax.broadcasted_iota(jnp.int32, (M_W, M_W), 1))
    c_ref[...] = dec_all * cbd + jnp.where(r2 == c2, cnew, 0.0)
    n_ref[...] = dec_all * nrow + jnp.sum(kw, axis=0, keepdims=True)


def _dot01_rhs_t(mask, x):
    return _dot01_lhs(mask.astype(BF16), x, 3)


def _mlstm(qm, km, vm, gc, gr, *, batch, chunk, nb):
    m = qm.shape[0]
    seq = m // batch
    view = lambda a: a.reshape(batch, seq, a.shape[-1])
    tok = lambda width: pl.BlockSpec((nb, chunk, width), lambda b, c: (b, c, 0))
    out = pl.pallas_call(
        _mlstm_kernel,
        grid=(batch // nb, seq // chunk),
        in_specs=[tok(M_W), tok(M_W), tok(M_W), tok(GATE_PAD),
                  pl.BlockSpec((nb, 2 * HEADS, chunk), lambda b, c: (b, 0, c))],
        out_specs=tok(M_W),
        out_shape=jax.ShapeDtypeStruct((batch, seq, M_W), F32),
        scratch_shapes=[pltpu.VMEM((nb, M_W, M_W), F32), pltpu.VMEM((nb, 1, M_W), F32),
                        pltpu.VMEM((nb, 8, 128), F32)],
        compiler_params=_cparams(("parallel", "arbitrary")),
        name="mlstm",
    )(view(qm), view(km), view(vm), view(gc), gr)
    return out.reshape(m, M_W)


HG_BLK = 16


def _hgrn_kernel(q_ref, k_ref, v_ref, lf_ref, o_ref, st_ref):
    @pl.when(pl.program_id(1) == 0)
    def _():
        st_ref[...] = jnp.zeros_like(st_ref)

    nb = q_ref.shape[0]
    nblk = q_ref.shape[1] // HG_BLK
    r16 = lax.broadcasted_iota(jnp.int32, (HG_BLK, HG_BLK), 0)
    c16 = lax.broadcasted_iota(jnp.int32, (HG_BLK, HG_BLK), 1)
    tri16 = (r16 >= c16).astype(BF16)
    rowid = lax.broadcasted_iota(jnp.int32, (HG_BLK, M_W), 0)
    r2 = _head_of(lax.broadcasted_iota(jnp.int32, (M_W, M_W), 0))
    c2 = _head_of(lax.broadcasted_iota(jnp.int32, (M_W, M_W), 1))
    same_head = r2 == c2
    ones_bd = same_head.astype(BF16)

    def block(i, sq):
        r0 = pl.multiple_of(i * HG_BLK, HG_BLK)
        q = q_ref[sq, pl.ds(r0, HG_BLK), :].astype(F32)
        k = k_ref[sq, pl.ds(r0, HG_BLK), :].astype(F32)
        v = v_ref[sq, pl.ds(r0, HG_BLK), :].astype(F32)
        lf = lf_ref[sq, pl.ds(r0, HG_BLK), :]
        b = _dot01_lhs(tri16, lf, 3)
        bl = b[HG_BLK - 1:HG_BLK, :]
        st = st_ref[sq]
        o = lax.dot_general((q * jnp.exp(b)).astype(BF16), st.astype(BF16), (((1,), (1,)), ((), ())),
                            preferred_element_type=F32)
        ps = []
        for s in range(HG_BLK):
            dd = jnp.where(rowid >= s, b - b[s:s + 1, :], -jnp.inf)
            ps.append(q * k[s:s + 1, :] * jnp.exp(dd))
        p = jnp.concatenate(ps, axis=0).astype(BF16)
        abig = jnp.dot(p, ones_bd, preferred_element_type=F32)
        for s in range(HG_BLK):
            o = o + abig[s * HG_BLK:(s + 1) * HG_BLK, :] * v[s:s + 1, :]
        o_ref[sq, pl.ds(r0, HG_BLK), :] = o
        ke = (k * jnp.exp(bl - b)).astype(BF16)
        upd = lax.dot_general(v.astype(BF16), ke, (((0,), (0,)), ((), ())), preferred_element_type=F32)
        st_ref[sq] = st * jnp.exp(bl) + jnp.where(same_head, upd, 0.0)

    def body(i, carry):
        for sq in range(nb):
            block(i, sq)
        return carry

    lax.fori_loop(0, nblk, body, 0)


def _hgrn(qh, kh, vh, lfh, *, batch, tile, nb):
    m = qh.shape[0]
    seq = m // batch
    view = lambda a: a.reshape(batch, seq, M_W)
    tok = pl.BlockSpec((nb, tile, M_W), lambda b, t: (b, t, 0))
    out = pl.pallas_call(
        _hgrn_kernel,
        grid=(batch // nb, seq // tile),
        in_specs=[tok, tok, tok, tok],
        out_specs=tok,
        out_shape=jax.ShapeDtypeStruct((batch, seq, M_W), F32),
        scratch_shapes=[pltpu.VMEM((nb, M_W, M_W), F32)],
        compiler_params=_cparams(("parallel", "arbitrary")),
        name="hgrn2",
    )(view(qh), view(kh), view(vh), view(lfh))
    return out.reshape(m, M_W)


ONES_ROWS = 16


def _attn_kernel(q_ref, k_ref, vt_ref, bias_ref, lam0_ref, lv_ref, nwt_ref, o_ref, m_ref, acc_ref, sa_ref, sb_ref,
                 mca_ref, mcb_ref, *, tile):
    qi = pl.program_id(2)
    vchunk = vt_ref.shape[2]
    nvc = tile // vchunk
    m_ref[...] = jnp.full_like(m_ref, NEG)
    acc_ref[...] = jnp.zeros_like(acc_ref)
    q = q_ref[...]
    first = lax.broadcasted_iota(jnp.int32, (1, DF_HW), 1) < HEAD_W
    zero = jnp.zeros_like(q)
    qs = (jnp.where(first, q, zero), jnp.where(first, zero, q))
    ones = jnp.ones((ONES_ROWS, vchunk), BF16)

    def scores(ki, s_ref, mc_ref):
        kb = k_ref[pl.ds(pl.multiple_of(ki * tile, tile), tile), :]
        bias = bias_ref[jnp.minimum(qi - ki, 2)]
        for j in range(2):
            st = bias + lax.dot_general(kb, qs[j], (((1,), (1,)), ((), ())), preferred_element_type=F32)
            s_ref[j] = st
            mc_ref[j] = jnp.max(st, axis=0, keepdims=True)

    def softmax_pv(ki, s_ref, mc_ref):
        vts = [jnp.concatenate([vt_ref[ki * nvc + c], ones], axis=0) for c in range(nvc)]
        for j in range(2):
            m_old = m_ref[j]
            m_new = jnp.maximum(m_old, mc_ref[j])
            alpha = jnp.exp2(m_old - m_new)
            pt = jnp.exp2((s_ref[j] - m_new).astype(BF16))
            pv = None
            for c in range(nvc):
                t = jnp.dot(vts[c], pt[c * vchunk:(c + 1) * vchunk, :], preferred_element_type=F32)
                pv = t if pv is None else pv + t
            acc_ref[j] = alpha * acc_ref[j] + pv
            m_ref[j] = m_new

    ntiles = qi + 1
    scores(0, sa_ref, mca_ref)

    def pair(p, carry):
        ka = 2 * p
        scores(ka + 1, sb_ref, mcb_ref)
        softmax_pv(ka, sa_ref, mca_ref)
        scores(ka + 2, sa_ref, mca_ref)
        softmax_pv(ka + 1, sb_ref, mcb_ref)
        return carry

    lax.fori_loop(0, (ntiles - 1) // 2, pair, 0)

    @pl.when(ntiles % 2 == 0)
    def _():
        scores(qi, sb_ref, mcb_ref)
        softmax_pv(qi - 1, sa_ref, mca_ref)
        softmax_pv(qi, sb_ref, mcb_ref)

    @pl.when(ntiles % 2 == 1)
    def _():
        softmax_pv(qi, sa_ref, mca_ref)

    lv = lv_ref[...]
    lam0 = lam0_ref[0]
    lam = (jnp.exp(jnp.sum(lv[0:1] * lv[1:2], axis=1, keepdims=True))
           - jnp.exp(jnp.sum(lv[2:3] * lv[3:4], axis=1, keepdims=True)) + lam0)
    a1 = acc_ref[0]
    a2 = acc_ref[1]
    od = a1[0:DF_HW] / a1[DF_HW:DF_HW + 1] - lam * (a2[0:DF_HW] / a2[DF_HW:DF_HW + 1])
    ms = jnp.mean(od * od, axis=0, keepdims=True)
    yt = od * lax.rsqrt(ms + EPS) * nwt_ref[...] * (1.0 - lam0)
    o_ref[...] = yt.T.astype(o_ref.dtype)


def _diff_attn(qd, kd, vdt, bias_tiles, lam0, lam_vecs, norm_w_t, layer, *, batch, tile):
    m = qd.shape[0]
    seq = m // batch
    nq = seq // tile
    nchunk, vchunk = vdt.shape[1], vdt.shape[3]
    return pl.pallas_call(
        functools.partial(_attn_kernel, tile=tile),
        grid=(batch, HEADS, nq),
        in_specs=[
            pl.BlockSpec((tile, DF_HW), lambda b, h, i: (b * nq + i, h)),
            pl.BlockSpec((seq, DF_HW), lambda b, h, i: (b, h)),
            pl.BlockSpec((None, nchunk, DF_HW, vchunk), lambda b, h, i: (b, 0, h, 0)),
            pl.BlockSpec((None, 3, tile, tile), lambda b, h, i: (h, 0, 0, 0)),
            pl.BlockSpec(memory_space=pltpu.SMEM),
            pl.BlockSpec((None, 4, HEAD_W), lambda b, h, i: (layer, 0, 0)),
            pl.BlockSpec((None, DF_HW, 1), lambda b, h, i: (layer, h, 0)),
        ],
        out_specs=pl.BlockSpec((tile, DF_HW), lambda b, h, i: (b * nq + i, h)),
        out_shape=jax.ShapeDtypeStruct((m, DF_W), BF16),
        scratch_shapes=[pltpu.VMEM((2, 1, tile), F32), pltpu.VMEM((2, DF_HW + ONES_ROWS, tile), F32),
                        pltpu.VMEM((2, tile, tile), F32), pltpu.VMEM((2, tile, tile), F32),
                        pltpu.VMEM((2, 1, tile), F32), pltpu.VMEM((2, 1, tile), F32)],
        compiler_params=_cparams(("parallel", "parallel", "arbitrary")),
        name="diff_attn",
    )(qd, kd, vdt, bias_tiles, lam0, lam_vecs, norm_w_t)


def _mix_out_kernel(x_ref, hm_ref, om_ref, yd_ref, oh_ref, gh_ref, mnw_ref, hnw_ref, w_ref, nw_ref, o_ref):
    r2 = _head_of(lax.broadcasted_iota(jnp.int32, (M_W, M_W), 0))
    c2 = _head_of(lax.broadcasted_iota(jnp.int32, (M_W, M_W), 1))
    ones_bd = (r2 == c2).astype(BF16)
    inv = 1.0 / HEAD_W

    hm = hm_ref[...]
    xc = hm - _dot01_rhs(hm, ones_bd, 2) * inv
    var = _dot01_rhs(xc * xc, ones_bd, 2) * inv
    ym = xc * lax.rsqrt(var + EPS) * mnw_ref[...] * _sigmoid(om_ref[...])

    oh = oh_ref[...]
    ms = _dot01_rhs(oh * oh, ones_bd, 2) * inv
    gh = gh_ref[...]
    yh = oh * lax.rsqrt(ms + EPS) * hnw_ref[...] * (gh * _sigmoid(gh))

    w = w_ref[...]
    acc = jnp.dot(ym.astype(BF16), w[0:M_W], preferred_element_type=F32)
    acc = acc + jnp.dot(yd_ref[...].astype(BF16), w[M_W:M_W + DF_W], preferred_element_type=F32)
    acc = acc + jnp.dot(yh.astype(BF16), w[M_W + DF_W:], preferred_element_type=F32)
    o_ref[...] = x_ref[...] + _rms(acc, nw_ref[...])


def _mix_out(x, hm, om, yd, oh, gh, mnw, hnw, w_out, nw, layer, *, tm):
    m, d = x.shape
    tok = lambda width: pl.BlockSpec((tm, width), lambda i: (i, 0))
    lay = lambda *shape: pl.BlockSpec((None,) + shape, lambda i: (layer,) + (0,) * len(shape))
    return pl.pallas_call(
        _mix_out_kernel,
        grid=(m // tm,),
        in_specs=[tok(d), tok(M_W), tok(M_W), tok(DF_W), tok(M_W), tok(M_W),
                  lay(1, M_W), lay(1, M_W), lay(d, d),
                  pl.BlockSpec((None, None, 1, d), lambda i: (layer, 3, 0, 0))],
        out_specs=tok(d),
        out_shape=jax.ShapeDtypeStruct((m, d), F32),
        compiler_params=_cparams(("parallel",)),
        name="mix_out",
    )(x, hm, om, yd, oh, gh, mnw, hnw, w_out, nw)


def _tiles(batch, seq):
    return dict(
        ffn_tm=min(512, seq), proj_tm=min(512, seq), mlstm_chunk=min(256, seq), hgrn_tile=min(256, seq),
        attn_tile=min(512, seq), out_tm=min(512, seq), hgrn_nb=8 if batch % 8 == 0 else 1,
        mlstm_nb=1)


def kernel(x, norm_w, ffn1_wi, ffn1_wo, ffn2_wi, ffn2_wo, w_in, w_out, mlstm_conv_w, mlstm_conv_b, mlstm_igate_b,
           mlstm_fgate_b, mlstm_norm_w, diff_lambda, diff_norm_w, rel_bias, hgrn_lb_logits, hgrn_norm_w):
    batch, seq, d = x.shape
    depth = norm_w.shape[0]
    tl = _tiles(batch, seq)
    act_dtype = BF16

    off = [0] + [int(v) for v in np.cumsum(SPLIT_SIZES)]
    seg = lambda i: w_in[:, :, off[i]:off[i + 1]].astype(BF16)
    gate_pad = jnp.zeros(w_in.shape[:2] + (GATE_PAD - 2 * HEADS,), BF16)
    w_proj = jnp.concatenate([seg(0), seg(1), seg(2), seg(5), seg(6), seg(8), seg(9), seg(10), seg(11),
                              seg(3), seg(4), gate_pad], axis=2)
    w_vd_t = jnp.swapaxes(w_in[:, :, off[7]:off[8]], 1, 2).astype(BF16)
    w_gate_t = jnp.swapaxes(w_in[:, :, off[3]:off[5]], 1, 2).astype(BF16)
    gate_b = jnp.concatenate([mlstm_igate_b, mlstm_fgate_b], axis=1).astype(F32)
    gate_b_col = jnp.pad(gate_b, ((0, 0), (0, GATE_PAD - 2 * HEADS)))[:, None, :]
    gate_b_row = gate_b[:, :, None]
    wi1, wo1 = ffn1_wi.astype(BF16), ffn1_wo.astype(BF16)
    wi2, wo2 = ffn2_wi.astype(BF16), ffn2_wo.astype(BF16)
    w_o = w_out.astype(BF16)
    nw = norm_w.astype(F32)[:, :, None, :]
    conv_w = mlstm_conv_w.astype(F32)
    conv_b = mlstm_conv_b.astype(F32)[:, None, :]
    mnw = mlstm_norm_w.astype(F32)[:, None, :]
    hnw = hgrn_norm_w.astype(F32)[:, None, :]
    dnw_t = diff_norm_w.astype(F32)[:, :, None]
    lam_vecs = diff_lambda.astype(F32)

    lb_all = _hgrn_lower_bounds(hgrn_lb_logits)[:, None, :]
    bias_tiles = _rel_bias_tiles(rel_bias, tl["attn_tile"])

    xf = x.reshape(batch * seq, d)
    for l in range(depth):
        lam0 = jnp.full((1,), 0.8 - 0.6 * math.exp(-0.3 * l), F32)
        xf = _ffn(xf, nw, l, 0, 1, wi1, wo1, tm=tl["ffn_tm"])
        (qm, km, vm, om, gc, gr, qd, kd, vdt, qh, lfh, kh, vh, gh) = _proj(
            xf, nw, l, w_proj, w_vd_t, w_gate_t, conv_w, conv_b, gate_b_col, gate_b_row, lb_all,
            batch=batch, tm=tl["proj_tm"], act_dtype=act_dtype)
        hm = _mlstm(qm, km, vm, gc, gr, batch=batch, chunk=tl["mlstm_chunk"], nb=tl["mlstm_nb"])
        yd = _diff_attn(qd, kd, vdt, bias_tiles, lam0, lam_vecs, dnw_t, l, batch=batch, tile=tl["attn_tile"])
        oh = _hgrn(qh, kh, vh, lfh, batch=batch, tile=tl["hgrn_tile"], nb=tl["hgrn_nb"])
        xf = _mix_out(xf, hm, om, yd, oh, gh, mnw, hnw, w_o, nw, l, tm=tl["out_tm"])
        xf = _ffn(xf, nw, l, 4, 5, wi2, wo2, tm=tl["ffn_tm"])
    return xf.reshape(batch, seq, d)
```

```python
import functools
import math

import numpy as np
import jax
import jax.numpy as jnp
from jax import lax
from jax.experimental import pallas as pl
from jax.experimental.pallas import tpu as pltpu

F32 = jnp.float32
BF16 = jnp.bfloat16
EPS = 1e-6
NEG = -1e30
LOG2E = 1.4426950408889634

HEADS = 4
HEAD_W = 64
M_W = HEADS * HEAD_W
DF_HW = 128
DF_W = HEADS * DF_HW
CONV_W = 4
REL_BUCKETS = 32
REL_MAX_EXACT = 16
REL_MAX_DIST = 128
GATE_PAD = 128
SPLIT_SIZES = (2 * M_W, M_W, M_W, HEADS, HEADS, DF_W, DF_W, DF_W, M_W, M_W, M_W, M_W)
VMEM_LIMIT = 56 * 1024 * 1024


def _cparams(sem):
    return pltpu.CompilerParams(dimension_semantics=sem, vmem_limit_bytes=VMEM_LIMIT)


def _sigmoid(x):
    return 1.0 / (1.0 + jnp.exp(-x))


def _log_sigmoid(x):
    return jnp.minimum(x, 0.0) - jnp.log1p(jnp.exp(-jnp.abs(x)))


def _rms(x, w):
    return x * lax.rsqrt(jnp.mean(x * x, axis=-1, keepdims=True) + EPS) * w


def _split_bf16(x, parts):
    out = []
    r = x
    for _ in range(parts):
        p = r.astype(BF16)
        out.append(p)
        r = r - p.astype(F32)
    return out


def _dot01_rhs(x, m01, parts):
    acc = None
    for p in _split_bf16(x, parts):
        t = jnp.dot(p, m01, preferred_element_type=F32)
        acc = t if acc is None else acc + t
    return acc


def _dot01_lhs(m01, x, parts):
    acc = None
    for p in _split_bf16(x, parts):
        t = jnp.dot(m01, p, preferred_element_type=F32)
        acc = t if acc is None else acc + t
    return acc


def _head_of(idx):
    return lax.shift_right_logical(idx, 6)


def _lb_kernel(lg_ref, o_ref):
    lg = lg_ref[...]
    e = jnp.exp(lg - jnp.max(lg, axis=0, keepdims=True))
    sm = e / jnp.sum(e, axis=0, keepdims=True)
    depth = lg.shape[0]
    rows = []
    run = sm[0:1]
    first = run
    for i in range(depth):
        if i > 0:
            run = run + sm[i:i + 1]
        rows.append(jnp.maximum(run - first, 0.0))
    o_ref[...] = jnp.concatenate(rows, axis=0)


def _hgrn_lower_bounds(logits):
    return pl.pallas_call(
        _lb_kernel, out_shape=jax.ShapeDtypeStruct(logits.shape, F32), name="hgrn_lb")(logits.astype(F32))


def _bias_kernel(tab_ref, o_ref, *, tile):
    h = pl.program_id(0)
    typ = pl.program_id(1)
    r = lax.broadcasted_iota(jnp.int32, (tile, tile), 0)
    c = lax.broadcasted_iota(jnp.int32, (tile, tile), 1)
    rel = c - r + typ * tile
    n = jnp.maximum(rel, 0)
    nf = jnp.maximum(n, 1).astype(F32)
    large = REL_MAX_EXACT + (jnp.log(nf / REL_MAX_EXACT) / math.log(REL_MAX_DIST / REL_MAX_EXACT)
                             * (REL_BUCKETS - REL_MAX_EXACT)).astype(jnp.int32)
    large = jnp.minimum(large, REL_BUCKETS - 1)
    bucket = jnp.where(n < REL_MAX_EXACT, n, large)
    bias = jnp.zeros((tile, tile), F32)
    for b in range(REL_BUCKETS):
        bias = jnp.where(bucket == b, tab_ref[b, h], bias)
    o_ref[...] = jnp.where(rel >= 0, bias * LOG2E, NEG)


def _rel_bias_tiles(rel_bias, tile):
    assert tile >= REL_MAX_DIST
    return pl.pallas_call(
        functools.partial(_bias_kernel, tile=tile),
        grid=(HEADS, 3),
        in_specs=[pl.BlockSpec(memory_space=pltpu.SMEM)],
        out_specs=pl.BlockSpec((None, None, tile, tile), lambda h, t: (h, t, 0, 0)),
        out_shape=jax.ShapeDtypeStruct((HEADS, 3, tile, tile), F32),
        name="rel_bias_tiles",
    )(rel_bias.astype(F32))


def _ffn_kernel(x_ref, nwi_ref, wi_ref, wo_ref, nwo_ref, o_ref):
    dff = wo_ref.shape[0]
    x = x_ref[...]
    xn = _rms(x, nwi_ref[...]).astype(BF16)
    g = jnp.dot(xn, wi_ref[:, 0:dff], preferred_element_type=F32)
    u = jnp.dot(xn, wi_ref[:, dff:2 * dff], preferred_element_type=F32)
    a = (g * _sigmoid(g) * u).astype(BF16)
    h = jnp.dot(a, wo_ref[...], preferred_element_type=F32)
    o_ref[...] = x + 0.5 * _rms(h, nwo_ref[...])


def _ffn(x, nw, layer, row_in, row_out, wi, wo, *, tm):
    m, d = x.shape
    dff = wo.shape[1]
    resident = pl.Buffered(1)
    return pl.pallas_call(
        _ffn_kernel,
        grid=(m // tm,),
        in_specs=[
            pl.BlockSpec((tm, d), lambda i: (i, 0)),
            pl.BlockSpec((None, None, 1, d), lambda i: (layer, row_in, 0, 0)),
            pl.BlockSpec((None, d, 2 * dff), lambda i: (layer, 0, 0), pipeline_mode=resident),
            pl.BlockSpec((None, dff, d), lambda i: (layer, 0, 0), pipeline_mode=resident),
            pl.BlockSpec((None, None, 1, d), lambda i: (layer, row_out, 0, 0)),
        ],
        out_specs=pl.BlockSpec((tm, d), lambda i: (i, 0)),
        out_shape=jax.ShapeDtypeStruct((m, d), F32),
        compiler_params=_cparams(("parallel",)),
        name="ffn",
    )(x, nw, wi, wo, nw)


def _proj_kernel(x_ref, nw_ref, w_ref, wvt_ref, wvmt_ref, wgt_ref, cw_ref, cb_ref, gbc_ref, gbr_ref, lb_ref,
                 qm_ref, km_ref, vmt_ref, om_ref, gc_ref, gr_ref, qd_ref, kd_ref, vdt_ref,
                 qh_ref, lfh_ref, kh_ref, vh_ref, gh_ref, cbuf_ref):
    t = pl.program_id(1)
    tm = x_ref.shape[0]
    xn = _rms(x_ref[...], nw_ref[...]).astype(BF16)

    def u(lo, hi):
        return jnp.dot(xn, w_ref[:, lo:hi], preferred_element_type=F32)

    @pl.when(t == 0)
    def _():
        cbuf_ref[0:8, :] = jnp.zeros((8, 2 * M_W), F32)

    qk = u(0, 2 * M_W)
    cbuf_ref[8:8 + tm, :] = qk
    cw = cw_ref[...]
    y = cb_ref[...] + cw[CONV_W - 1:CONV_W] * qk
    for d in range(1, CONV_W):
        y = y + cw[CONV_W - 1 - d:CONV_W - d] * cbuf_ref[8 - d:8 - d + tm, :]
    cbuf_ref[0:8, :] = cbuf_ref[tm:tm + 8, :]
    y = y * _sigmoid(y)
    qm_ref[...] = y[:, 0:M_W].astype(qm_ref.dtype)
    km_ref[...] = (y[:, M_W:2 * M_W] * (HEAD_W ** -0.5)).astype(km_ref.dtype)
    vmt_ref[...] = lax.dot_general(wvmt_ref[...], xn, (((1,), (1,)), ((), ())),
                                   preferred_element_type=F32).astype(vmt_ref.dtype)
    om_ref[...] = u(768, 1024)

    qd_ref[...] = (u(1024, 1536) * (HEAD_W ** -0.5 * LOG2E)).astype(qd_ref.dtype)
    kd_ref[...] = u(1536, 2048).astype(kd_ref.dtype)
    vdt_ref[...] = lax.dot_general(wvt_ref[...], xn, (((1,), (1,)), ((), ())),
                                   preferred_element_type=F32).astype(vdt_ref.dtype)

    qh = u(2048, 2304)
    qh_ref[...] = (qh * _sigmoid(qh)).astype(qh_ref.dtype)
    fp = u(2304, 2560)
    lb = lb_ref[...]
    a = jnp.log(lb)
    bb = jnp.log1p(-lb) + _log_sigmoid(fp)
    lfh_ref[...] = jnp.maximum(a, bb) + jnp.log1p(jnp.exp(-jnp.abs(a - bb)))
    kh_ref[...] = ((1.0 - lb) * _sigmoid(-fp)).astype(kh_ref.dtype)
    vh_ref[...] = u(2560, 2816).astype(vh_ref.dtype)
    gh_ref[...] = u(2816, 3072)

    zc = u(3072, 3072 + GATE_PAD) + gbc_ref[...]
    lane = lax.broadcasted_iota(jnp.int32, (1, GATE_PAD), 1)
    gc_ref[...] = jnp.where(lane < HEADS, zc, _log_sigmoid(zc))
    zr = lax.dot_general(wgt_ref[...], xn, (((1,), (1,)), ((), ())), preferred_element_type=F32) + gbr_ref[...]
    row = lax.broadcasted_iota(jnp.int32, (2 * HEADS, 1), 0)
    gr_ref[...] = jnp.where(row < HEADS, zr, _log_sigmoid(zr))


def _proj(x, nw, layer, w, wvt, wvmt, wgt, cw, cb, gbc, gbr, lb, *, batch, tm, act_dtype):
    m, d = x.shape
    seq = m // batch
    nt = seq // tm
    npc = w.shape[-1]
    tok = lambda width: pl.BlockSpec((tm, width), lambda b, t: (b * nt + t, 0))
    lay = lambda *shape: pl.BlockSpec((None,) + shape, lambda b, t: (layer,) + (0,) * len(shape),
                                      pipeline_mode=pl.Buffered(1))
    sds = lambda width, dt: jax.ShapeDtypeStruct((m, width), dt)
    out_shape = [sds(M_W, act_dtype), sds(M_W, act_dtype),
                 jax.ShapeDtypeStruct((batch, nt, M_W, tm), act_dtype), sds(M_W, F32),
                 sds(GATE_PAD, F32), jax.ShapeDtypeStruct((batch, 2 * HEADS, seq), F32),
                 sds(DF_W, act_dtype), sds(DF_W, act_dtype),
                 jax.ShapeDtypeStruct((batch, nt, DF_W, tm), act_dtype),
                 sds(M_W, act_dtype), sds(M_W, F32), sds(M_W, act_dtype), sds(M_W, act_dtype), sds(M_W, F32)]
    out_specs = [tok(M_W), tok(M_W), pl.BlockSpec((None, None, M_W, tm), lambda b, t: (b, t, 0, 0)), tok(M_W),
                 tok(GATE_PAD),
                 pl.BlockSpec((None, 2 * HEADS, tm), lambda b, t: (b, 0, t)),
                 tok(DF_W), tok(DF_W),
                 pl.BlockSpec((None, None, DF_W, tm), lambda b, t: (b, t, 0, 0)),
                 tok(M_W), tok(M_W), tok(M_W), tok(M_W), tok(M_W)]
    return pl.pallas_call(
        _proj_kernel,
        grid=(batch, nt),
        in_specs=[tok(d),
                  pl.BlockSpec((None, None, 1, d), lambda b, t: (layer, 2, 0, 0)),
                  lay(d, npc), lay(DF_W, d), lay(M_W, d), lay(2 * HEADS, d), lay(CONV_W, 2 * M_W), lay(1, 2 * M_W),
                  lay(1, GATE_PAD), lay(2 * HEADS, 1), lay(1, M_W)],
        out_specs=out_specs,
        out_shape=out_shape,
        scratch_shapes=[pltpu.VMEM((tm + 8, 2 * M_W), F32)],
        compiler_params=_cparams(("parallel", "arbitrary")),
        name="mixer_proj",
    )(x, nw, w, wvt, wvmt, wgt, cw, cb, gbc, gbr, lb)


def _mlstm_kernel(q_ref, k_ref, v_ref, gc_ref, gr_ref, h_ref, c_ref, n_ref, m_ref):
    @pl.when(pl.program_id(1) == 0)
    def _():
        c_ref[...] = jnp.zeros_like(c_ref)
        n_ref[...] = jnp.zeros_like(n_ref)
        m_ref[...] = jnp.zeros_like(m_ref)

    for sq in range(q_ref.shape[0]):
        _mlstm_chunk(*(r.at[sq] for r in (q_ref, k_ref, v_ref, gc_ref, gr_ref, h_ref, c_ref, n_ref, m_ref)))


def _mlstm_chunk(q_ref, k_ref, vt_ref, gc_ref, gr_ref, h_ref, ct_ref, n_ref, m_ref):
    L = q_ref.shape[0]
    qb = q_ref[...]
    kb = k_ref[...]
    vt = vt_ref[...]
    gc = gc_ref[...]
    gr = gr_ref[...]
    row = lax.broadcasted_iota(jnp.int32, (L, L), 0)
    col = lax.broadcasted_iota(jnp.int32, (L, L), 1)
    allowed = row <= col
    bcol = _dot01_lhs((row >= col).astype(BF16), gc, 3)
    brow = _dot01_rhs(gr, allowed.astype(BF16), 3)
    lane_head = _head_of(lax.broadcasted_iota(jnp.int32, (1, M_W), 1))
    r8 = lax.broadcasted_iota(jnp.int32, (2 * HEADS, M_W), 0)
    own8 = _head_of(lax.broadcasted_iota(jnp.int32, (2 * HEADS, M_W), 1)) == r8
    r8l = lax.broadcasted_iota(jnp.int32, (2 * HEADS, L), 0)
    ct = ct_ref[...]
    nrow = n_ref[...]
    nt_dims = (((1,), (1,)), ((), ()))
    qct = lax.dot_general(ct.astype(BF16), qb, nt_dims, preferred_element_type=F32)
    qn8 = None
    for part in _split_bf16(jnp.where(own8, nrow, 0.0), 3):
        t = lax.dot_general(part, qb, nt_dims, preferred_element_type=F32)
        qn8 = t if qn8 is None else qn8 + t
    hts, vws = [], []
    wa8 = jnp.zeros((2 * HEADS, L), F32)
    dec_all = jnp.zeros((1, M_W), F32)
    for h in range(HEADS):
        hm = lane_head == h
        rows = slice(h * HEAD_W, (h + 1) * HEAD_W)
        b_row = brow[HEADS + h:HEADS + h + 1, :]
        keycol = gc[:, h:h + 1] - bcol[:, HEADS + h:HEADS + h + 1]
        mprev = m_ref[h:h + 1, 0:1]
        dmat = jnp.where(allowed, b_row + keycol, -jnp.inf)
        m_inter = b_row + mprev
        mt = jnp.maximum(jnp.max(dmat, axis=0, keepdims=True), m_inter)
        w = jnp.exp(dmat - mt)
        qh = jnp.where(hm, qb, jnp.zeros_like(qb))
        sc = lax.dot_general(kb, qh, nt_dims, preferred_element_type=F32) * w
        g = jnp.exp(m_inter - mt)
        pv = jnp.dot(vt[rows], sc.astype(BF16), preferred_element_type=F32)
        den = jnp.sum(sc, axis=0, keepdims=True) + g * qn8[h:h + 1, :]
        scale = 1.0 / jnp.maximum(jnp.abs(den), jnp.exp(-mt))
        hts.append((pv + g * qct[rows]) * scale)
        bl = b_row[:, L - 1:L]
        a = bl - b_row + gr[h:h + 1, :]
        mnew = jnp.maximum(bl + mprev, jnp.max(a, axis=1, keepdims=True))
        wa = jnp.exp(a - mnew)
        vws.append(vt[rows].astype(F32) * wa)
        wa8 = jnp.where(r8l == h, wa, wa8)
        dec_all = jnp.where(hm, jnp.exp(bl + mprev - mnew), dec_all)
        m_ref[h:h + 1, :] = jnp.broadcast_to(mnew, (1, m_ref.shape[1]))
    h_ref[...] = jnp.concatenate(hts, axis=0).T
    cnew = jnp.dot(jnp.concatenate(vws, axis=0).astype(BF16), kb, preferred_element_type=F32)
    r2 = _head_of(lax.broadcasted_iota(jnp.int32, (M_W, M_W), 0))
    c2 = _head_of(lax.broadcasted_iota(jnp.int32, (M_W, M_W), 1))
    ct_ref[...] = dec_all * ct + jnp.where(r2 == c2, cnew, 0.0)
    kn8 = _dot_split_lhs(wa8, kb)
    n_ref[...] = dec_all * nrow + jnp.sum(jnp.where(own8, kn8, 0.0), axis=0, keepdims=True)


def _dot_split_lhs(x, y_bf16):
    acc = None
    for p in _split_bf16(x, 3):
        t = jnp.dot(p, y_bf16, preferred_element_type=F32)
        acc = t if acc is None else acc + t
    return acc


def _mlstm(qm, km, vmt, gc, gr, *, batch, chunk, nb):
    m = qm.shape[0]
    seq = m // batch
    per = vmt.shape[3] // chunk
    view = lambda a: a.reshape(batch, seq, a.shape[-1])
    tok = lambda width: pl.BlockSpec((nb, chunk, width), lambda b, c: (b, c, 0))
    out = pl.pallas_call(
        _mlstm_kernel,
        grid=(batch // nb, seq // chunk),
        in_specs=[tok(M_W), tok(M_W),
                  pl.BlockSpec((nb, None, M_W, chunk), lambda b, c: (b, c // per, 0, c % per)),
                  tok(GATE_PAD),
                  pl.BlockSpec((nb, 2 * HEADS, chunk), lambda b, c: (b, 0, c))],
        out_specs=tok(M_W),
        out_shape=jax.ShapeDtypeStruct((batch, seq, M_W), F32),
        scratch_shapes=[pltpu.VMEM((nb, M_W, M_W), F32), pltpu.VMEM((nb, 1, M_W), F32),
                        pltpu.VMEM((nb, 8, 128), F32)],
        compiler_params=_cparams(("parallel", "arbitrary")),
        name="mlstm",
    )(view(qm), view(km), vmt, view(gc), gr)
    return out.reshape(m, M_W)


HG_BLK = 16


def _hgrn_kernel(q_ref, k_ref, v_ref, lf_ref, o_ref, st_ref):
    @pl.when(pl.program_id(1) == 0)
    def _():
        st_ref[...] = jnp.zeros_like(st_ref)

    nb = q_ref.shape[0]
    nblk = q_ref.shape[1] // HG_BLK
    r16 = lax.broadcasted_iota(jnp.int32, (HG_BLK, HG_BLK), 0)
    c16 = lax.broadcasted_iota(jnp.int32, (HG_BLK, HG_BLK), 1)
    tri16 = (r16 >= c16).astype(BF16)
    rowid = lax.broadcasted_iota(jnp.int32, (HG_BLK, M_W), 0)
    r2 = _head_of(lax.broadcasted_iota(jnp.int32, (M_W, M_W), 0))
    c2 = _head_of(lax.broadcasted_iota(jnp.int32, (M_W, M_W), 1))
    same_head = r2 == c2
    ones_bd = same_head.astype(BF16)

    def block(i, sq):
        r0 = pl.multiple_of(i * HG_BLK, HG_BLK)
        q = q_ref[sq, pl.ds(r0, HG_BLK), :].astype(F32)
        k = k_ref[sq, pl.ds(r0, HG_BLK), :].astype(F32)
        v = v_ref[sq, pl.ds(r0, HG_BLK), :].astype(F32)
        lf = lf_ref[sq, pl.ds(r0, HG_BLK), :]
        b = _dot01_lhs(tri16, lf, 3)
        bl = b[HG_BLK - 1:HG_BLK, :]
        st = st_ref[sq]
        o = lax.dot_general((q * jnp.exp(b)).astype(BF16), st.astype(BF16), (((1,), (1,)), ((), ())),
                            preferred_element_type=F32)
        ps = []
        for s in range(HG_BLK):
            dd = jnp.where(rowid >= s, b - b[s:s + 1, :], -jnp.inf)
            ps.append(q * k[s:s + 1, :] * jnp.exp(dd))
        p = jnp.concatenate(ps, axis=0).astype(BF16)
        abig = jnp.dot(p, ones_bd, preferred_element_type=F32)
        for s in range(HG_BLK):
            o = o + abig[s * HG_BLK:(s + 1) * HG_BLK, :] * v[s:s + 1, :]
        o_ref[sq, pl.ds(r0, HG_BLK), :] = o
        ke = (k * jnp.exp(bl - b)).astype(BF16)
        upd = lax.dot_general(v.astype(BF16), ke, (((0,), (0,)), ((), ())), preferred_element_type=F32)
        st_ref[sq] = st * jnp.exp(bl) + jnp.where(same_head, upd, 0.0)

    def body(i, carry):
        for sq in range(nb):
            block(i, sq)
        return carry

    lax.fori_loop(0, nblk, body, 0)


def _hgrn(qh, kh, vh, lfh, *, batch, tile, nb):
    m = qh.shape[0]
    seq = m // batch
    view = lambda a: a.reshape(batch, seq, M_W)
    tok = pl.BlockSpec((nb, tile, M_W), lambda b, t: (b, t, 0))
    out = pl.pallas_call(
        _hgrn_kernel,
        grid=(batch // nb, seq // tile),
        in_specs=[tok, tok, tok, tok],
        out_specs=tok,
        out_shape=jax.ShapeDtypeStruct((batch, seq, M_W), F32),
        scratch_shapes=[pltpu.VMEM((nb, M_W, M_W), F32)],
        compiler_params=_cparams(("parallel", "arbitrary")),
        name="hgrn2",
    )(view(qh), view(kh), view(vh), view(lfh))
    return out.reshape(m, M_W)


ONES_ROWS = 16


def _attn_kernel(q_ref, k_ref, vt_ref, bias_ref, lam0_ref, lv_ref, nwt_ref, o_ref, m_ref, acc_ref, sa_ref, sb_ref,
                 mca_ref, mcb_ref, *, tile):
    qi = pl.program_id(2)
    vchunk = vt_ref.shape[2]
    nvc = tile // vchunk
    m_ref[...] = jnp.full_like(m_ref, NEG)
    acc_ref[...] = jnp.zeros_like(acc_ref)
    q = q_ref[...]
    first = lax.broadcasted_iota(jnp.int32, (1, DF_HW), 1) < HEAD_W
    zero = jnp.zeros_like(q)
    qs = (jnp.where(first, q, zero), jnp.where(first, zero, q))
    ones = jnp.ones((ONES_ROWS, vchunk), BF16)

    def scores(ki, s_ref, mc_ref):
        kb = k_ref[pl.ds(pl.multiple_of(ki * tile, tile), tile), :]
        bias = bias_ref[jnp.minimum(qi - ki, 2)]
        for j in range(2):
            st = bias + lax.dot_general(kb, qs[j], (((1,), (1,)), ((), ())), preferred_element_type=F32)
            s_ref[j] = st
            mc_ref[j] = jnp.max(st, axis=0, keepdims=True)

    def softmax_pv(ki, s_ref, mc_ref):
        vts = [jnp.concatenate([vt_ref[ki * nvc + c], ones], axis=0) for c in range(nvc)]
        for j in range(2):
            m_old = m_ref[j]
            m_new = jnp.maximum(m_old, mc_ref[j])
            alpha = jnp.exp2(m_old - m_new)
            pt = jnp.exp2((s_ref[j] - m_new).astype(BF16))
            pv = None
            for c in range(nvc):
                t = jnp.dot(vts[c], pt[c * vchunk:(c + 1) * vchunk, :], preferred_element_type=F32)
                pv = t if pv is None else pv + t
            acc_ref[j] = alpha * acc_ref[j] + pv
            m_ref[j] = m_new

    ntiles = qi + 1
    scores(0, sa_ref, mca_ref)

    def pair(p, carry):
        ka = 2 * p
        scores(ka + 1, sb_ref, mcb_ref)
        softmax_pv(ka, sa_ref, mca_ref)
        scores(ka + 2, sa_ref, mca_ref)
        softmax_pv(ka + 1, sb_ref, mcb_ref)
        return carry

    lax.fori_loop(0, (ntiles - 1) // 2, pair, 0)

    @pl.when(ntiles % 2 == 0)
    def _():
        scores(qi, sb_ref, mcb_ref)
        softmax_pv(qi - 1, sa_ref, mca_ref)
        softmax_pv(qi, sb_ref, mcb_ref)

    @pl.when(ntiles % 2 == 1)
    def _():
        softmax_pv(qi, sa_ref, mca_ref)

    lv = lv_ref[...]
    lam0 = lam0_ref[0]
    lam = (jnp.exp(jnp.sum(lv[0:1] * lv[1:2], axis=1, keepdims=True))
           - jnp.exp(jnp.sum(lv[2:3] * lv[3:4], axis=1, keepdims=True)) + lam0)
    a1 = acc_ref[0]
    a2 = acc_ref[1]
    od = a1[0:DF_HW] / a1[DF_HW:DF_HW + 1] - lam * (a2[0:DF_HW] / a2[DF_HW:DF_HW + 1])
    ms = jnp.mean(od * od, axis=0, keepdims=True)
    yt = od * lax.rsqrt(ms + EPS) * nwt_ref[...] * (1.0 - lam0)
    o_ref[...] = yt.T.astype(o_ref.dtype)


def _diff_attn(qd, kd, vdt, bias_tiles, lam0, lam_vecs, norm_w_t, layer, *, batch, tile):
    m = qd.shape[0]
    seq = m // batch
    nq = seq // tile
    nchunk, vchunk = vdt.shape[1], vdt.shape[3]
    return pl.pallas_call(
        functools.partial(_attn_kernel, tile=tile),
        grid=(batch, HEADS, nq),
        in_specs=[
            pl.BlockSpec((tile, DF_HW), lambda b, h, i: (b * nq + i, h)),
            pl.BlockSpec((seq, DF_HW), lambda b, h, i: (b, h)),
            pl.BlockSpec((None, nchunk, DF_HW, vchunk), lambda b, h, i: (b, 0, h, 0)),
            pl.BlockSpec((None, 3, tile, tile), lambda b, h, i: (h, 0, 0, 0)),
            pl.BlockSpec(memory_space=pltpu.SMEM),
            pl.BlockSpec((None, 4, HEAD_W), lambda b, h, i: (layer, 0, 0)),
            pl.BlockSpec((None, DF_HW, 1), lambda b, h, i: (layer, h, 0)),
        ],
        out_specs=pl.BlockSpec((tile, DF_HW), lambda b, h, i: (b * nq + i, h)),
        out_shape=jax.ShapeDtypeStruct((m, DF_W), BF16),
        scratch_shapes=[pltpu.VMEM((2, 1, tile), F32), pltpu.VMEM((2, DF_HW + ONES_ROWS, tile), F32),
                        pltpu.VMEM((2, tile, tile), F32), pltpu.VMEM((2, tile, tile), F32),
                        pltpu.VMEM((2, 1, tile), F32), pltpu.VMEM((2, 1, tile), F32)],
        compiler_params=_cparams(("parallel", "parallel", "arbitrary")),
        name="diff_attn",
    )(qd, kd, vdt, bias_tiles, lam0, lam_vecs, norm_w_t)


def _mix_out_kernel(x_ref, hm_ref, om_ref, yd_ref, oh_ref, gh_ref, mnw_ref, hnw_ref, w_ref, nw_ref, o_ref):
    r2 = _head_of(lax.broadcasted_iota(jnp.int32, (M_W, M_W), 0))
    c2 = _head_of(lax.broadcasted_iota(jnp.int32, (M_W, M_W), 1))
    ones_bd = (r2 == c2).astype(BF16)
    inv = 1.0 / HEAD_W

    hm = hm_ref[...]
    xc = hm - _dot01_rhs(hm, ones_bd, 2) * inv
    var = _dot01_rhs(xc * xc, ones_bd, 2) * inv
    ym = xc * lax.rsqrt(var + EPS) * mnw_ref[...] * _sigmoid(om_ref[...])

    oh = oh_ref[...]
    ms = _dot01_rhs(oh * oh, ones_bd, 2) * inv
    gh = gh_ref[...]
    yh = oh * lax.rsqrt(ms + EPS) * hnw_ref[...] * (gh * _sigmoid(gh))

    w = w_ref[...]
    acc = jnp.dot(ym.astype(BF16), w[0:M_W], preferred_element_type=F32)
    acc = acc + jnp.dot(yd_ref[...].astype(BF16), w[M_W:M_W + DF_W], preferred_element_type=F32)
    acc = acc + jnp.dot(yh.astype(BF16), w[M_W + DF_W:], preferred_element_type=F32)
    o_ref[...] = x_ref[...] + _rms(acc, nw_ref[...])


def _mix_out(x, hm, om, yd, oh, gh, mnw, hnw, w_out, nw, layer, *, tm):
    m, d = x.shape
    tok = lambda width: pl.BlockSpec((tm, width), lambda i: (i, 0))
    lay = lambda *shape: pl.BlockSpec((None,) + shape, lambda i: (layer,) + (0,) * len(shape))
    return pl.pallas_call(
        _mix_out_kernel,
        grid=(m // tm,),
        in_specs=[tok(d), tok(M_W), tok(M_W), tok(DF_W), tok(M_W), tok(M_W),
                  lay(1, M_W), lay(1, M_W), lay(d, d),
                  pl.BlockSpec((None, None, 1, d), lambda i: (layer, 3, 0, 0))],
        out_specs=tok(d),
        out_shape=jax.ShapeDtypeStruct((m, d), F32),
        compiler_params=_cparams(("parallel",)),
        name="mix_out",
    )(x, hm, om, yd, oh, gh, mnw, hnw, w_out, nw)


def _tiles(batch, seq):
    return dict(
        ffn_tm=min(512, seq), proj_tm=min(512, seq), mlstm_chunk=min(256, seq), hgrn_tile=min(256, seq),
        attn_tile=min(512, seq), out_tm=min(512, seq), hgrn_nb=8 if batch % 8 == 0 else 1,
        mlstm_nb=2 if batch % 2 == 0 else 1)


def kernel(x, norm_w, ffn1_wi, ffn1_wo, ffn2_wi, ffn2_wo, w_in, w_out, mlstm_conv_w, mlstm_conv_b, mlstm_igate_b,
           mlstm_fgate_b, mlstm_norm_w, diff_lambda, diff_norm_w, rel_bias, hgrn_lb_logits, hgrn_norm_w):
    batch, seq, d = x.shape
    depth = norm_w.shape[0]
    tl = _tiles(batch, seq)
    act_dtype = BF16

    off = [0] + [int(v) for v in np.cumsum(SPLIT_SIZES)]
    seg = lambda i: w_in[:, :, off[i]:off[i + 1]].astype(BF16)
    gate_pad = jnp.zeros(w_in.shape[:2] + (GATE_PAD - 2 * HEADS,), BF16)
    w_proj = jnp.concatenate([seg(0), seg(1), seg(2), seg(5), seg(6), seg(8), seg(9), seg(10), seg(11),
                              seg(3), seg(4), gate_pad], axis=2)
    w_vd_t = jnp.swapaxes(w_in[:, :, off[7]:off[8]], 1, 2).astype(BF16)
    w_vm_t = jnp.swapaxes(w_in[:, :, off[1]:off[2]], 1, 2).astype(BF16)
    w_gate_t = jnp.swapaxes(w_in[:, :, off[3]:off[5]], 1, 2).astype(BF16)
    gate_b = jnp.concatenate([mlstm_igate_b, mlstm_fgate_b], axis=1).astype(F32)
    gate_b_col = jnp.pad(gate_b, ((0, 0), (0, GATE_PAD - 2 * HEADS)))[:, None, :]
    gate_b_row = gate_b[:, :, None]
    wi1, wo1 = ffn1_wi.astype(BF16), ffn1_wo.astype(BF16)
    wi2, wo2 = ffn2_wi.astype(BF16), ffn2_wo.astype(BF16)
    w_o = w_out.astype(BF16)
    nw = norm_w.astype(F32)[:, :, None, :]
    conv_w = mlstm_conv_w.astype(F32)
    conv_b = mlstm_conv_b.astype(F32)[:, None, :]
    mnw = mlstm_norm_w.astype(F32)[:, None, :]
    hnw = hgrn_norm_w.astype(F32)[:, None, :]
    dnw_t = diff_norm_w.astype(F32)[:, :, None]
    lam_vecs = diff_lambda.astype(F32)

    lb_all = _hgrn_lower_bounds(hgrn_lb_logits)[:, None, :]
    bias_tiles = _rel_bias_tiles(rel_bias, tl["attn_tile"])

    xf = x.reshape(batch * seq, d)
    for l in range(depth):
        lam0 = jnp.full((1,), 0.8 - 0.6 * math.exp(-0.3 * l), F32)
        xf = _ffn(xf, nw, l, 0, 1, wi1, wo1, tm=tl["ffn_tm"])
        (qm, km, vmt, om, gc, gr, qd, kd, vdt, qh, lfh, kh, vh, gh) = _proj(
            xf, nw, l, w_proj, w_vd_t, w_vm_t, w_gate_t, conv_w, conv_b, gate_b_col, gate_b_row, lb_all,
            batch=batch, tm=tl["proj_tm"], act_dtype=act_dtype)
        hm = _mlstm(qm, km, vmt, gc, gr, batch=batch, chunk=tl["mlstm_chunk"], nb=tl["mlstm_nb"])
        yd = _diff_attn(qd, kd, vdt, bias_tiles, lam0, lam_vecs, dnw_t, l, batch=batch, tile=tl["attn_tile"])
        oh = _hgrn(qh, kh, vh, lfh, batch=batch, tile=tl["hgrn_tile"], nb=tl["hgrn_nb"])
        xf = _mix_out(xf, hm, om, yd, oh, gh, mnw, hnw, w_o, nw, l, tm=tl["out_tm"])
        xf = _ffn(xf, nw, l, 4, 5, wi2, wo2, tm=tl["ffn_tm"])
    return xf.reshape(batch, seq, d)
```

```python
import functools
import math

import numpy as np
import jax
import jax.numpy as jnp
from jax import lax
from jax.experimental import pallas as pl
from jax.experimental.pallas import tpu as pltpu

F32 = jnp.float32
BF16 = jnp.bfloat16
EPS = 1e-6
NEG = -1e30
LOG2E = 1.4426950408889634

HEADS = 4
HEAD_W = 64
M_W = HEADS * HEAD_W
DF_HW = 128
DF_W = HEADS * DF_HW
CONV_W = 4
REL_BUCKETS = 32
REL_MAX_EXACT = 16
REL_MAX_DIST = 128
GATE_PAD = 128
SPLIT_SIZES = (2 * M_W, M_W, M_W, HEADS, HEADS, DF_W, DF_W, DF_W, M_W, M_W, M_W, M_W)
VMEM_LIMIT = 56 * 1024 * 1024


def _cparams(sem):
    return pltpu.CompilerParams(dimension_semantics=sem, vmem_limit_bytes=VMEM_LIMIT)


def _sigmoid(x):
    return 1.0 / (1.0 + jnp.exp(-x))


def _log_sigmoid(x):
    return jnp.minimum(x, 0.0) - jnp.log1p(jnp.exp(-jnp.abs(x)))


def _rms(x, w):
    return x * lax.rsqrt(jnp.mean(x * x, axis=-1, keepdims=True) + EPS) * w


def _split_bf16(x, parts):
    out = []
    r = x
    for _ in range(parts):
        p = r.astype(BF16)
        out.append(p)
        r = r - p.astype(F32)
    return out


def _dot01_rhs(x, m01, parts):
    acc = None
    for p in _split_bf16(x, parts):
        t = jnp.dot(p, m01, preferred_element_type=F32)
        acc = t if acc is None else acc + t
    return acc


def _dot01_lhs(m01, x, parts):
    acc = None
    for p in _split_bf16(x, parts):
        t = jnp.dot(m01, p, preferred_element_type=F32)
        acc = t if acc is None else acc + t
    return acc


def _head_of(idx):
    return lax.shift_right_logical(idx, 6)


def _lb_kernel(lg_ref, o_ref):
    lg = lg_ref[...]
    e = jnp.exp(lg - jnp.max(lg, axis=0, keepdims=True))
    sm = e / jnp.sum(e, axis=0, keepdims=True)
    depth = lg.shape[0]
    rows = []
    run = sm[0:1]
    first = run
    for i in range(depth):
        if i > 0:
            run = run + sm[i:i + 1]
        rows.append(jnp.maximum(run - first, 0.0))
    o_ref[...] = jnp.concatenate(rows, axis=0)


def _hgrn_lower_bounds(logits):
    return pl.pallas_call(
        _lb_kernel, out_shape=jax.ShapeDtypeStruct(logits.shape, F32), name="hgrn_lb")(logits.astype(F32))


def _bias_kernel(tab_ref, o_ref, *, tile):
    h = pl.program_id(0)
    typ = pl.program_id(1)
    r = lax.broadcasted_iota(jnp.int32, (tile, tile), 0)
    c = lax.broadcasted_iota(jnp.int32, (tile, tile), 1)
    rel = c - r + typ * tile
    n = jnp.maximum(rel, 0)
    nf = jnp.maximum(n, 1).astype(F32)
    large = REL_MAX_EXACT + (jnp.log(nf / REL_MAX_EXACT) / math.log(REL_MAX_DIST / REL_MAX_EXACT)
                             * (REL_BUCKETS - REL_MAX_EXACT)).astype(jnp.int32)
    large = jnp.minimum(large, REL_BUCKETS - 1)
    bucket = jnp.where(n < REL_MAX_EXACT, n, large)
    bias = jnp.zeros((tile, tile), F32)
    for b in range(REL_BUCKETS):
        bias = jnp.where(bucket == b, tab_ref[b, h], bias)
    o_ref[...] = jnp.where(rel >= 0, bias * LOG2E, NEG)


def _rel_bias_tiles(rel_bias, tile):
    assert tile >= REL_MAX_DIST
    return pl.pallas_call(
        functools.partial(_bias_kernel, tile=tile),
        grid=(HEADS, 3),
        in_specs=[pl.BlockSpec(memory_space=pltpu.SMEM)],
        out_specs=pl.BlockSpec((None, None, tile, tile), lambda h, t: (h, t, 0, 0)),
        out_shape=jax.ShapeDtypeStruct((HEADS, 3, tile, tile), F32),
        name="rel_bias_tiles",
    )(rel_bias.astype(F32))


def _ffn_kernel(x_ref, nwi_ref, wi_ref, wo_ref, nwo_ref, o_ref):
    dff = wo_ref.shape[0]
    x = x_ref[...]
    xn = _rms(x, nwi_ref[...]).astype(BF16)
    g = jnp.dot(xn, wi_ref[:, 0:dff], preferred_element_type=F32)
    u = jnp.dot(xn, wi_ref[:, dff:2 * dff], preferred_element_type=F32)
    a = (g * _sigmoid(g) * u).astype(BF16)
    h = jnp.dot(a, wo_ref[...], preferred_element_type=F32)
    o_ref[...] = x + 0.5 * _rms(h, nwo_ref[...])


def _ffn(x, nw, layer, row_in, row_out, wi, wo, *, tm):
    m, d = x.shape
    dff = wo.shape[1]
    resident = pl.Buffered(1)
    return pl.pallas_call(
        _ffn_kernel,
        grid=(m // tm,),
        in_specs=[
            pl.BlockSpec((tm, d), lambda i: (i, 0)),
            pl.BlockSpec((None, None, 1, d), lambda i: (layer, row_in, 0, 0)),
            pl.BlockSpec((None, d, 2 * dff), lambda i: (layer, 0, 0), pipeline_mode=resident),
            pl.BlockSpec((None, dff, d), lambda i: (layer, 0, 0), pipeline_mode=resident),
            pl.BlockSpec((None, None, 1, d), lambda i: (layer, row_out, 0, 0)),
        ],
        out_specs=pl.BlockSpec((tm, d), lambda i: (i, 0)),
        out_shape=jax.ShapeDtypeStruct((m, d), F32),
        compiler_params=_cparams(("parallel",)),
        name="ffn",
    )(x, nw, wi, wo, nw)


def _proj_kernel(x_ref, nw_ref, w_ref, wvt_ref, wvmt_ref, wgt_ref, cw_ref, cb_ref, gbc_ref, gbr_ref, lb_ref,
                 qm_ref, km_ref, vmt_ref, om_ref, gc_ref, gr_ref, qd_ref, kd_ref, vdt_ref,
                 qh_ref, lfh_ref, kh_ref, vh_ref, gh_ref, cbuf_ref):
    t = pl.program_id(1)
    tm = x_ref.shape[0]
    xn = _rms(x_ref[...], nw_ref[...]).astype(BF16)

    def u(lo, hi):
        return jnp.dot(xn, w_ref[:, lo:hi], preferred_element_type=F32)

    @pl.when(t == 0)
    def _():
        cbuf_ref[0:8, :] = jnp.zeros((8, 2 * M_W), F32)

    qk = u(0, 2 * M_W)
    cbuf_ref[8:8 + tm, :] = qk
    cw = cw_ref[...]
    y = cb_ref[...] + cw[CONV_W - 1:CONV_W] * qk
    for d in range(1, CONV_W):
        y = y + cw[CONV_W - 1 - d:CONV_W - d] * cbuf_ref[8 - d:8 - d + tm, :]
    cbuf_ref[0:8, :] = cbuf_ref[tm:tm + 8, :]
    y = y * _sigmoid(y)
    qm_ref[...] = y[:, 0:M_W].astype(qm_ref.dtype)
    km_ref[...] = (y[:, M_W:2 * M_W] * (HEAD_W ** -0.5)).astype(km_ref.dtype)
    vmt_ref[...] = lax.dot_general(wvmt_ref[...], xn, (((1,), (1,)), ((), ())),
                                   preferred_element_type=F32).astype(vmt_ref.dtype)
    om_ref[...] = u(768, 1024)

    qd_ref[...] = (u(1024, 1536) * (HEAD_W ** -0.5 * LOG2E)).astype(qd_ref.dtype)
    kd_ref[...] = u(1536, 2048).astype(kd_ref.dtype)
    vdt_ref[...] = lax.dot_general(wvt_ref[...], xn, (((1,), (1,)), ((), ())),
                                   preferred_element_type=F32).astype(vdt_ref.dtype)

    qh = u(2048, 2304)
    qh_ref[...] = (qh * _sigmoid(qh)).astype(qh_ref.dtype)
    fp = u(2304, 2560)
    lb = lb_ref[...]
    a = jnp.log(lb)
    bb = jnp.log1p(-lb) + _log_sigmoid(fp)
    lfh_ref[...] = jnp.maximum(a, bb) + jnp.log1p(jnp.exp(-jnp.abs(a - bb)))
    kh_ref[...] = ((1.0 - lb) * _sigmoid(-fp)).astype(kh_ref.dtype)
    vh_ref[...] = u(2560, 2816).astype(vh_ref.dtype)
    gh_ref[...] = u(2816, 3072)

    zc = u(3072, 3072 + GATE_PAD) + gbc_ref[...]
    lane = lax.broadcasted_iota(jnp.int32, (1, GATE_PAD), 1)
    gc_ref[...] = jnp.where(lane < HEADS, zc, _log_sigmoid(zc))
    zr = lax.dot_general(wgt_ref[...], xn, (((1,), (1,)), ((), ())), preferred_element_type=F32) + gbr_ref[...]
    row = lax.broadcasted_iota(jnp.int32, (2 * HEADS, 1), 0)
    gr_ref[...] = jnp.where(row < HEADS, zr, _log_sigmoid(zr))


def _proj(x, nw, layer, w, wvt, wvmt, wgt, cw, cb, gbc, gbr, lb, *, batch, tm, act_dtype):
    m, d = x.shape
    seq = m // batch
    nt = seq // tm
    npc = w.shape[-1]
    tok = lambda width: pl.BlockSpec((tm, width), lambda b, t: (b * nt + t, 0))
    lay = lambda *shape: pl.BlockSpec((None,) + shape, lambda b, t: (layer,) + (0,) * len(shape),
                                      pipeline_mode=pl.Buffered(1))
    sds = lambda width, dt: jax.ShapeDtypeStruct((m, width), dt)
    out_shape = [sds(M_W, act_dtype), sds(M_W, act_dtype),
                 jax.ShapeDtypeStruct((batch, nt, M_W, tm), act_dtype), sds(M_W, F32),
                 sds(GATE_PAD, F32), jax.ShapeDtypeStruct((batch, 2 * HEADS, seq), F32),
                 sds(DF_W, act_dtype), sds(DF_W, act_dtype),
                 jax.ShapeDtypeStruct((batch, nt, DF_W, tm), act_dtype),
                 sds(M_W, act_dtype), sds(M_W, F32), sds(M_W, act_dtype), sds(M_W, act_dtype), sds(M_W, F32)]
    out_specs = [tok(M_W), tok(M_W), pl.BlockSpec((None, None, M_W, tm), lambda b, t: (b, t, 0, 0)), tok(M_W),
                 tok(GATE_PAD),
                 pl.BlockSpec((None, 2 * HEADS, tm), lambda b, t: (b, 0, t)),
                 tok(DF_W), tok(DF_W),
                 pl.BlockSpec((None, None, DF_W, tm), lambda b, t: (b, t, 0, 0)),
                 tok(M_W), tok(M_W), tok(M_W), tok(M_W), tok(M_W)]
    return pl.pallas_call(
        _proj_kernel,
        grid=(batch, nt),
        in_specs=[tok(d),
                  pl.BlockSpec((None, None, 1, d), lambda b, t: (layer, 2, 0, 0)),
                  lay(d, npc), lay(DF_W, d), lay(M_W, d), lay(2 * HEADS, d), lay(CONV_W, 2 * M_W), lay(1, 2 * M_W),
                  lay(1, GATE_PAD), lay(2 * HEADS, 1), lay(1, M_W)],
        out_specs=out_specs,
        out_shape=out_shape,
        scratch_shapes=[pltpu.VMEM((tm + 8, 2 * M_W), F32)],
        compiler_params=_cparams(("parallel", "arbitrary")),
        name="mixer_proj",
    )(x, nw, w, wvt, wvmt, wgt, cw, cb, gbc, gbr, lb)


def _mlstm_kernel(q_ref, k_ref, v_ref, gc_ref, gr_ref, h_ref, c_ref, n_ref, m_ref):
    @pl.when(pl.program_id(1) == 0)
    def _():
        c_ref[...] = jnp.zeros_like(c_ref)
        n_ref[...] = jnp.zeros_like(n_ref)
        m_ref[...] = jnp.zeros_like(m_ref)

    for sq in range(q_ref.shape[0]):
        _mlstm_chunk(*(r.at[sq] for r in (q_ref, k_ref, v_ref, gc_ref, gr_ref, h_ref, c_ref, n_ref, m_ref)))


def _mlstm_chunk(q_ref, k_ref, vt_ref, gc_ref, gr_ref, h_ref, ct_ref, n_ref, m_ref):
    L = q_ref.shape[0]
    qb = q_ref[...]
    kb = k_ref[...]
    vt = vt_ref[...]
    gc = gc_ref[...]
    gr = gr_ref[...]
    row = lax.broadcasted_iota(jnp.int32, (L, L), 0)
    col = lax.broadcasted_iota(jnp.int32, (L, L), 1)
    allowed = row <= col
    bcol = _dot01_lhs((row >= col).astype(BF16), gc, 3)
    brow = _dot01_rhs(gr, allowed.astype(BF16), 3)
    lane_head = _head_of(lax.broadcasted_iota(jnp.int32, (1, M_W), 1))
    r8 = lax.broadcasted_iota(jnp.int32, (2 * HEADS, M_W), 0)
    own8 = _head_of(lax.broadcasted_iota(jnp.int32, (2 * HEADS, M_W), 1)) == r8
    r8l = lax.broadcasted_iota(jnp.int32, (2 * HEADS, L), 0)
    ct = ct_ref[...]
    nrow = n_ref[...]
    nt_dims = (((1,), (1,)), ((), ()))
    qct = lax.dot_general(ct.astype(BF16), qb, nt_dims, preferred_element_type=F32)
    qn8 = None
    for part in _split_bf16(jnp.where(own8, nrow, 0.0), 3):
        t = lax.dot_general(part, qb, nt_dims, preferred_element_type=F32)
        qn8 = t if qn8 is None else qn8 + t
    hts, vws = [], []
    wa8 = jnp.zeros((2 * HEADS, L), F32)
    dec_all = jnp.zeros((1, M_W), F32)
    for h in range(HEADS):
        hm = lane_head == h
        rows = slice(h * HEAD_W, (h + 1) * HEAD_W)
        b_row = brow[HEADS + h:HEADS + h + 1, :]
        keycol = gc[:, h:h + 1] - bcol[:, HEADS + h:HEADS + h + 1]
        mprev = m_ref[h:h + 1, 0:1]
        dmat = jnp.where(allowed, b_row + keycol, -jnp.inf)
        m_inter = b_row + mprev
        mt = jnp.maximum(jnp.max(dmat, axis=0, keepdims=True), m_inter)
        w = jnp.exp(dmat - mt)
        qh = jnp.where(hm, qb, jnp.zeros_like(qb))
        sc = lax.dot_general(kb, qh, nt_dims, preferred_element_type=F32) * w
        g = jnp.exp(m_inter - mt)
        pv = jnp.dot(vt[rows], sc.astype(BF16), preferred_element_type=F32)
        den = jnp.sum(sc, axis=0, keepdims=True) + g * qn8[h:h + 1, :]
        scale = 1.0 / jnp.maximum(jnp.abs(den), jnp.exp(-mt))
        hts.append((pv + g * qct[rows]) * scale)
        bl = b_row[:, L - 1:L]
        a = bl - b_row + gr[h:h + 1, :]
        mnew = jnp.maximum(bl + mprev, jnp.max(a, axis=1, keepdims=True))
        wa = jnp.exp(a - mnew)
        vws.append(vt[rows].astype(F32) * wa)
        wa8 = jnp.where(r8l == h, wa, wa8)
        dec_all = jnp.where(hm, jnp.exp(bl + mprev - mnew), dec_all)
        m_ref[h:h + 1, :] = jnp.broadcast_to(mnew, (1, m_ref.shape[1]))
    h_ref[...] = jnp.concatenate(hts, axis=0).T
    cnew = jnp.dot(jnp.concatenate(vws, axis=0).astype(BF16), kb, preferred_element_type=F32)
    r2 = _head_of(lax.broadcasted_iota(jnp.int32, (M_W, M_W), 0))
    c2 = _head_of(lax.broadcasted_iota(jnp.int32, (M_W, M_W), 1))
    ct_ref[...] = dec_all * ct + jnp.where(r2 == c2, cnew, 0.0)
    kn8 = _dot_split_lhs(wa8, kb)
    n_ref[...] = dec_all * nrow + jnp.sum(jnp.where(own8, kn8, 0.0), axis=0, keepdims=True)


def _dot_split_lhs(x, y_bf16):
    acc = None
    for p in _split_bf16(x, 3):
        t = jnp.dot(p, y_bf16, preferred_element_type=F32)
        acc = t if acc is None else acc + t
    return acc


def _mlstm(qm, km, vmt, gc, gr, *, batch, chunk, nb):
    m = qm.shape[0]
    seq = m // batch
    per = vmt.shape[3] // chunk
    view = lambda a: a.reshape(batch, seq, a.shape[-1])
    tok = lambda width: pl.BlockSpec((nb, chunk, width), lambda b, c: (b, c, 0))
    out = pl.pallas_call(
        _mlstm_kernel,
        grid=(batch // nb, seq // chunk),
        in_specs=[tok(M_W), tok(M_W),
                  pl.BlockSpec((nb, None, M_W, chunk), lambda b, c: (b, c // per, 0, c % per)),
                  tok(GATE_PAD),
                  pl.BlockSpec((nb, 2 * HEADS, chunk), lambda b, c: (b, 0, c))],
        out_specs=tok(M_W),
        out_shape=jax.ShapeDtypeStruct((batch, seq, M_W), F32),
        scratch_shapes=[pltpu.VMEM((nb, M_W, M_W), F32), pltpu.VMEM((nb, 1, M_W), F32),
                        pltpu.VMEM((nb, 8, 128), F32)],
        compiler_params=_cparams(("parallel", "arbitrary")),
        name="mlstm",
    )(view(qm), view(km), vmt, view(gc), gr)
    return out.reshape(m, M_W)


HG_BLK = 16


def _hgrn_kernel(q_ref, k_ref, v_ref, lf_ref, o_ref, st_ref):
    @pl.when(pl.program_id(1) == 0)
    def _():
        st_ref[...] = jnp.zeros_like(st_ref)

    nb = q_ref.shape[0]
    nblk = q_ref.shape[1] // HG_BLK
    r16 = lax.broadcasted_iota(jnp.int32, (HG_BLK, HG_BLK), 0)
    c16 = lax.broadcasted_iota(jnp.int32, (HG_BLK, HG_BLK), 1)
    tri16 = (r16 >= c16).astype(BF16)
    rowid = lax.broadcasted_iota(jnp.int32, (HG_BLK, M_W), 0)
    r2 = _head_of(lax.broadcasted_iota(jnp.int32, (M_W, M_W), 0))
    c2 = _head_of(lax.broadcasted_iota(jnp.int32, (M_W, M_W), 1))
    same_head = r2 == c2
    ones_bd = same_head.astype(BF16)

    def block(i, sq):
        r0 = pl.multiple_of(i * HG_BLK, HG_BLK)
        q = q_ref[sq, pl.ds(r0, HG_BLK), :].astype(F32)
        k = k_ref[sq, pl.ds(r0, HG_BLK), :].astype(F32)
        v = v_ref[sq, pl.ds(r0, HG_BLK), :].astype(F32)
        lf = lf_ref[sq, pl.ds(r0, HG_BLK), :]
        b = lf
        for d in (1, 2, 4, 8):
            b = b + jnp.where(rowid >= d, pltpu.roll(b, d, axis=0), 0.0)
        bl = b[HG_BLK - 1:HG_BLK, :]
        st = st_ref[sq]
        o = lax.dot_general((q * jnp.exp(b)).astype(BF16), st.astype(BF16), (((1,), (1,)), ((), ())),
                            preferred_element_type=F32)
        ps = []
        for s in range(HG_BLK):
            dd = jnp.where(rowid >= s, b - b[s:s + 1, :], -jnp.inf)
            ps.append(q * k[s:s + 1, :] * jnp.exp(dd))
        p = jnp.concatenate(ps, axis=0).astype(BF16)
        abig = jnp.dot(p, ones_bd, preferred_element_type=F32)
        for s in range(HG_BLK):
            o = o + abig[s * HG_BLK:(s + 1) * HG_BLK, :] * v[s:s + 1, :]
        o_ref[sq, pl.ds(r0, HG_BLK), :] = o
        ke = (k * jnp.exp(bl - b)).astype(BF16)
        upd = lax.dot_general(v.astype(BF16), ke, (((0,), (0,)), ((), ())), preferred_element_type=F32)
        st_ref[sq] = st * jnp.exp(bl) + jnp.where(same_head, upd, 0.0)

    def body(i, carry):
        for sq in range(nb):
            block(i, sq)
        return carry

    lax.fori_loop(0, nblk, body, 0)


def _hgrn(qh, kh, vh, lfh, *, batch, tile, nb):
    m = qh.shape[0]
    seq = m // batch
    view = lambda a: a.reshape(batch, seq, M_W)
    tok = pl.BlockSpec((nb, tile, M_W), lambda b, t: (b, t, 0))
    out = pl.pallas_call(
        _hgrn_kernel,
        grid=(batch // nb, seq // tile),
        in_specs=[tok, tok, tok, tok],
        out_specs=tok,
        out_shape=jax.ShapeDtypeStruct((batch, seq, M_W), F32),
        scratch_shapes=[pltpu.VMEM((nb, M_W, M_W), F32)],
        compiler_params=_cparams(("parallel", "arbitrary")),
        name="hgrn2",
    )(view(qh), view(kh), view(vh), view(lfh))
    return out.reshape(m, M_W)


ONES_ROWS = 16


def _attn_kernel(q_ref, k_ref, vt_ref, bias_ref, lam0_ref, lv_ref, nwt_ref, o_ref, m_ref, acc_ref, sa_ref, sb_ref,
                 mca_ref, mcb_ref, *, tile):
    qi = pl.program_id(2)
    vchunk = vt_ref.shape[2]
    nvc = tile // vchunk
    m_ref[...] = jnp.full_like(m_ref, NEG)
    acc_ref[...] = jnp.zeros_like(acc_ref)
    q = q_ref[...]
    first = lax.broadcasted_iota(jnp.int32, (1, DF_HW), 1) < HEAD_W
    zero = jnp.zeros_like(q)
    qs = (jnp.where(first, q, zero), jnp.where(first, zero, q))
    ones = jnp.ones((ONES_ROWS, vchunk), BF16)

    def scores(ki, s_ref, mc_ref):
        kb = k_ref[pl.ds(pl.multiple_of(ki * tile, tile), tile), :]
        bias = bias_ref[jnp.minimum(qi - ki, 2)]
        for j in range(2):
            st = bias + lax.dot_general(kb, qs[j], (((1,), (1,)), ((), ())), preferred_element_type=F32)
            s_ref[j] = st
            mc_ref[j] = jnp.max(st, axis=0, keepdims=True)

    def softmax_pv(ki, s_ref, mc_ref):
        vts = [jnp.concatenate([vt_ref[ki * nvc + c], ones], axis=0) for c in range(nvc)]
        for j in range(2):
            m_old = m_ref[j]
            m_new = jnp.maximum(m_old, mc_ref[j])
            alpha = jnp.exp2(m_old - m_new)
            pt = jnp.exp2((s_ref[j] - m_new).astype(BF16))
            pv = None
            for c in range(nvc):
                t = jnp.dot(vts[c], pt[c * vchunk:(c + 1) * vchunk, :], preferred_element_type=F32)
                pv = t if pv is None else pv + t
            acc_ref[j] = alpha * acc_ref[j] + pv
            m_ref[j] = m_new

    ntiles = qi + 1
    scores(0, sa_ref, mca_ref)

    def pair(p, carry):
        ka = 2 * p
        scores(ka + 1, sb_ref, mcb_ref)
        softmax_pv(ka, sa_ref, mca_ref)
        scores(ka + 2, sa_ref, mca_ref)
        softmax_pv(ka + 1, sb_ref, mcb_ref)
        return carry

    lax.fori_loop(0, (ntiles - 1) // 2, pair, 0)

    @pl.when(ntiles % 2 == 0)
    def _():
        scores(qi, sb_ref, mcb_ref)
        softmax_pv(qi - 1, sa_ref, mca_ref)
        softmax_pv(qi, sb_ref, mcb_ref)

    @pl.when(ntiles % 2 == 1)
    def _():
        softmax_pv(qi, sa_ref, mca_ref)

    lv = lv_ref[...]
    lam0 = lam0_ref[0]
    lam = (jnp.exp(jnp.sum(lv[0:1] * lv[1:2], axis=1, keepdims=True))
           - jnp.exp(jnp.sum(lv[2:3] * lv[3:4], axis=1, keepdims=True)) + lam0)
    a1 = acc_ref[0]
    a2 = acc_ref[1]
    od = a1[0:DF_HW] / a1[DF_HW:DF_HW + 1] - lam * (a2[0:DF_HW] / a2[DF_HW:DF_HW + 1])
    ms = jnp.mean(od * od, axis=0, keepdims=True)
    yt = od * lax.rsqrt(ms + EPS) * nwt_ref[...] * (1.0 - lam0)
    o_ref[...] = yt.T.astype(o_ref.dtype)


def _diff_attn(qd, kd, vdt, bias_tiles, lam0, lam_vecs, norm_w_t, layer, *, batch, tile):
    m = qd.shape[0]
    seq = m // batch
    nq = seq // tile
    nchunk, vchunk = vdt.shape[1], vdt.shape[3]
    return pl.pallas_call(
        functools.partial(_attn_kernel, tile=tile),
        grid=(HEADS, batch, nq),
        in_specs=[
            pl.BlockSpec((tile, DF_HW), lambda h, b, i: (b * nq + i, h)),
            pl.BlockSpec((seq, DF_HW), lambda h, b, i: (b, h)),
            pl.BlockSpec((None, nchunk, DF_HW, vchunk), lambda h, b, i: (b, 0, h, 0)),
            pl.BlockSpec((None, 3, tile, tile), lambda h, b, i: (h, 0, 0, 0)),
            pl.BlockSpec(memory_space=pltpu.SMEM),
            pl.BlockSpec((None, 4, HEAD_W), lambda h, b, i: (layer, 0, 0)),
            pl.BlockSpec((None, DF_HW, 1), lambda h, b, i: (layer, h, 0)),
        ],
        out_specs=pl.BlockSpec((tile, DF_HW), lambda h, b, i: (b * nq + i, h)),
        out_shape=jax.ShapeDtypeStruct((m, DF_W), BF16),
        scratch_shapes=[pltpu.VMEM((2, 1, tile), F32), pltpu.VMEM((2, DF_HW + ONES_ROWS, tile), F32),
                        pltpu.VMEM((2, tile, tile), F32), pltpu.VMEM((2, tile, tile), F32),
                        pltpu.VMEM((2, 1, tile), F32), pltpu.VMEM((2, 1, tile), F32)],
        compiler_params=_cparams(("parallel", "parallel", "arbitrary")),
        name="diff_attn",
    )(qd, kd, vdt, bias_tiles, lam0, lam_vecs, norm_w_t)


def _mix_out_kernel(x_ref, hm_ref, om_ref, yd_ref, oh_ref, gh_ref, mnw_ref, hnw_ref, w_ref, nw_ref, o_ref):
    r2 = _head_of(lax.broadcasted_iota(jnp.int32, (M_W, M_W), 0))
    c2 = _head_of(lax.broadcasted_iota(jnp.int32, (M_W, M_W), 1))
    ones_bd = (r2 == c2).astype(BF16)
    inv = 1.0 / HEAD_W

    hm = hm_ref[...]
    xc = hm - _dot01_rhs(hm, ones_bd, 2) * inv
    var = _dot01_rhs(xc * xc, ones_bd, 2) * inv
    ym = xc * lax.rsqrt(var + EPS) * mnw_ref[...] * _sigmoid(om_ref[...])

    oh = oh_ref[...]
    ms = _dot01_rhs(oh * oh, ones_bd, 2) * inv
    gh = gh_ref[...]
    yh = oh * lax.rsqrt(ms + EPS) * hnw_ref[...] * (gh * _sigmoid(gh))

    w = w_ref[...]
    acc = jnp.dot(ym.astype(BF16), w[0:M_W], preferred_element_type=F32)
    acc = acc + jnp.dot(yd_ref[...].astype(BF16), w[M_W:M_W + DF_W], preferred_element_type=F32)
    acc = acc + jnp.dot(yh.astype(BF16), w[M_W + DF_W:], preferred_element_type=F32)
    o_ref[...] = x_ref[...] + _rms(acc, nw_ref[...])


def _mix_out(x, hm, om, yd, oh, gh, mnw, hnw, w_out, nw, layer, *, tm):
    m, d = x.shape
    tok = lambda width: pl.BlockSpec((tm, width), lambda i: (i, 0))
    lay = lambda *shape: pl.BlockSpec((None,) + shape, lambda i: (layer,) + (0,) * len(shape))
    return pl.pallas_call(
        _mix_out_kernel,
        grid=(m // tm,),
        in_specs=[tok(d), tok(M_W), tok(M_W), tok(DF_W), tok(M_W), tok(M_W),
                  lay(1, M_W), lay(1, M_W), lay(d, d),
                  pl.BlockSpec((None, None, 1, d), lambda i: (layer, 3, 0, 0))],
        out_specs=tok(d),
        out_shape=jax.ShapeDtypeStruct((m, d), F32),
        compiler_params=_cparams(("parallel",)),
        name="mix_out",
    )(x, hm, om, yd, oh, gh, mnw, hnw, w_out, nw)


def _tiles(batch, seq):
    return dict(
        ffn_tm=min(512, seq), proj_tm=min(512, seq), mlstm_chunk=min(256, seq), hgrn_tile=min(256, seq),
        attn_tile=min(512, seq), out_tm=min(512, seq), hgrn_nb=8 if batch % 8 == 0 else 1,
        mlstm_nb=2 if batch % 2 == 0 else 1)


def kernel(x, norm_w, ffn1_wi, ffn1_wo, ffn2_wi, ffn2_wo, w_in, w_out, mlstm_conv_w, mlstm_conv_b, mlstm_igate_b,
           mlstm_fgate_b, mlstm_norm_w, diff_lambda, diff_norm_w, rel_bias, hgrn_lb_logits, hgrn_norm_w):
    batch, seq, d = x.shape
    depth = norm_w.shape[0]
    tl = _tiles(batch, seq)
    act_dtype = BF16

    off = [0] + [int(v) for v in np.cumsum(SPLIT_SIZES)]
    seg = lambda i: w_in[:, :, off[i]:off[i + 1]].astype(BF16)
    gate_pad = jnp.zeros(w_in.shape[:2] + (GATE_PAD - 2 * HEADS,), BF16)
    w_proj = jnp.concatenate([seg(0), seg(1), seg(2), seg(5), seg(6), seg(8), seg(9), seg(10), seg(11),
                              seg(3), seg(4), gate_pad], axis=2)
    w_vd_t = jnp.swapaxes(w_in[:, :, off[7]:off[8]], 1, 2).astype(BF16)
    w_vm_t = jnp.swapaxes(w_in[:, :, off[1]:off[2]], 1, 2).astype(BF16)
    w_gate_t = jnp.swapaxes(w_in[:, :, off[3]:off[5]], 1, 2).astype(BF16)
    gate_b = jnp.concatenate([mlstm_igate_b, mlstm_fgate_b], axis=1).astype(F32)
    gate_b_col = jnp.pad(gate_b, ((0, 0), (0, GATE_PAD - 2 * HEADS)))[:, None, :]
    gate_b_row = gate_b[:, :, None]
    wi1, wo1 = ffn1_wi.astype(BF16), ffn1_wo.astype(BF16)
    wi2, wo2 = ffn2_wi.astype(BF16), ffn2_wo.astype(BF16)
    w_o = w_out.astype(BF16)
    nw = norm_w.astype(F32)[:, :, None, :]
    conv_w = mlstm_conv_w.astype(F32)
    conv_b = mlstm_conv_b.astype(F32)[:, None, :]
    mnw = mlstm_norm_w.astype(F32)[:, None, :]
    hnw = hgrn_norm_w.astype(F32)[:, None, :]
    dnw_t = diff_norm_w.astype(F32)[:, :, None]
    lam_vecs = diff_lambda.astype(F32)

    lb_all = _hgrn_lower_bounds(hgrn_lb_logits)[:, None, :]
    bias_tiles = _rel_bias_tiles(rel_bias, tl["attn_tile"])

    xf = x.reshape(batch * seq, d)
    for l in range(depth):
        lam0 = jnp.full((1,), 0.8 - 0.6 * math.exp(-0.3 * l), F32)
        xf = _ffn(xf, nw, l, 0, 1, wi1, wo1, tm=tl["ffn_tm"])
        (qm, km, vmt, om, gc, gr, qd, kd, vdt, qh, lfh, kh, vh, gh) = _proj(
            xf, nw, l, w_proj, w_vd_t, w_vm_t, w_gate_t, conv_w, conv_b, gate_b_col, gate_b_row, lb_all,
            batch=batch, tm=tl["proj_tm"], act_dtype=act_dtype)
        hm = _mlstm(qm, km, vmt, gc, gr, batch=batch, chunk=tl["mlstm_chunk"], nb=tl["mlstm_nb"])
        yd = _diff_attn(qd, kd, vdt, bias_tiles, lam0, lam_vecs, dnw_t, l, batch=batch, tile=tl["attn_tile"])
        oh = _hgrn(qh, kh, vh, lfh, batch=batch, tile=tl["hgrn_tile"], nb=tl["hgrn_nb"])
        xf = _mix_out(xf, hm, om, yd, oh, gh, mnw, hnw, w_o, nw, l, tm=tl["out_tm"])
        xf = _ffn(xf, nw, l, 4, 5, wi2, wo2, tm=tl["ffn_tm"])
    return xf.reshape(batch, seq, d)
```

```python
import functools
import math

import numpy as np
import jax
import jax.numpy as jnp
from jax import lax
from jax.experimental import pallas as pl
from jax.experimental.pallas import tpu as pltpu

F32 = jnp.float32
BF16 = jnp.bfloat16
EPS = 1e-6
NEG = -1e30
LOG2E = 1.4426950408889634

HEADS = 4
HEAD_W = 64
M_W = HEADS * HEAD_W
DF_HW = 128
DF_W = HEADS * DF_HW
CONV_W = 4
REL_BUCKETS = 32
REL_MAX_EXACT = 16
REL_MAX_DIST = 128
GATE_PAD = 128
SPLIT_SIZES = (2 * M_W, M_W, M_W, HEADS, HEADS, DF_W, DF_W, DF_W, M_W, M_W, M_W, M_W)
VMEM_LIMIT = 56 * 1024 * 1024


def _cparams(sem):
    return pltpu.CompilerParams(dimension_semantics=sem, vmem_limit_bytes=VMEM_LIMIT)


def _sigmoid(x):
    return 1.0 / (1.0 + jnp.exp(-x))


def _log_sigmoid(x):
    return jnp.minimum(x, 0.0) - jnp.log1p(jnp.exp(-jnp.abs(x)))


def _rms(x, w):
    return x * lax.rsqrt(jnp.mean(x * x, axis=-1, keepdims=True) + EPS) * w


def _split_bf16(x, parts):
    out = []
    r = x
    for _ in range(parts):
        p = r.astype(BF16)
        out.append(p)
        r = r - p.astype(F32)
    return out


def _dot01_rhs(x, m01, parts):
    acc = None
    for p in _split_bf16(x, parts):
        t = jnp.dot(p, m01, preferred_element_type=F32)
        acc = t if acc is None else acc + t
    return acc


def _dot01_lhs(m01, x, parts):
    acc = None
    for p in _split_bf16(x, parts):
        t = jnp.dot(m01, p, preferred_element_type=F32)
        acc = t if acc is None else acc + t
    return acc


def _head_of(idx):
    return lax.shift_right_logical(idx, 6)


def _lb_kernel(lg_ref, o_ref):
    lg = lg_ref[...]
    e = jnp.exp(lg - jnp.max(lg, axis=0, keepdims=True))
    sm = e / jnp.sum(e, axis=0, keepdims=True)
    depth = lg.shape[0]
    rows = []
    run = sm[0:1]
    first = run
    for i in range(depth):
        if i > 0:
            run = run + sm[i:i + 1]
        rows.append(jnp.maximum(run - first, 0.0))
    o_ref[...] = jnp.concatenate(rows, axis=0)


def _hgrn_lower_bounds(logits):
    return pl.pallas_call(
        _lb_kernel, out_shape=jax.ShapeDtypeStruct(logits.shape, F32), name="hgrn_lb")(logits.astype(F32))


def _bias_kernel(tab_ref, o_ref, *, tile):
    h = pl.program_id(0)
    typ = pl.program_id(1)
    r = lax.broadcasted_iota(jnp.int32, (tile, tile), 0)
    c = lax.broadcasted_iota(jnp.int32, (tile, tile), 1)
    rel = c - r + typ * tile
    n = jnp.maximum(rel, 0)
    nf = jnp.maximum(n, 1).astype(F32)
    large = REL_MAX_EXACT + (jnp.log(nf / REL_MAX_EXACT) / math.log(REL_MAX_DIST / REL_MAX_EXACT)
                             * (REL_BUCKETS - REL_MAX_EXACT)).astype(jnp.int32)
    large = jnp.minimum(large, REL_BUCKETS - 1)
    bucket = jnp.where(n < REL_MAX_EXACT, n, large)
    bias = jnp.zeros((tile, tile), F32)
    for b in range(REL_BUCKETS):
        bias = jnp.where(bucket == b, tab_ref[b, h], bias)
    o_ref[...] = jnp.where(rel >= 0, bias * LOG2E, NEG)


def _rel_bias_tiles(rel_bias, tile):
    assert tile >= REL_MAX_DIST
    return pl.pallas_call(
        functools.partial(_bias_kernel, tile=tile),
        grid=(HEADS, 3),
        in_specs=[pl.BlockSpec(memory_space=pltpu.SMEM)],
        out_specs=pl.BlockSpec((None, None, tile, tile), lambda h, t: (h, t, 0, 0)),
        out_shape=jax.ShapeDtypeStruct((HEADS, 3, tile, tile), F32),
        name="rel_bias_tiles",
    )(rel_bias.astype(F32))


def _ffn_kernel(x_ref, nwi_ref, wi_ref, wo_ref, nwo_ref, o_ref):
    dff = wo_ref.shape[0]
    x = x_ref[...]
    xn = _rms(x, nwi_ref[...]).astype(BF16)
    g = jnp.dot(xn, wi_ref[:, 0:dff], preferred_element_type=F32)
    u = jnp.dot(xn, wi_ref[:, dff:2 * dff], preferred_element_type=F32)
    a = (g * _sigmoid(g) * u).astype(BF16)
    h = jnp.dot(a, wo_ref[...], preferred_element_type=F32)
    o_ref[...] = x + 0.5 * _rms(h, nwo_ref[...])


def _ffn(x, nw, layer, row_in, row_out, wi, wo, *, tm):
    m, d = x.shape
    dff = wo.shape[1]
    resident = pl.Buffered(1)
    return pl.pallas_call(
        _ffn_kernel,
        grid=(m // tm,),
        in_specs=[
            pl.BlockSpec((tm, d), lambda i: (i, 0)),
            pl.BlockSpec((None, None, 1, d), lambda i: (layer, row_in, 0, 0)),
            pl.BlockSpec((None, d, 2 * dff), lambda i: (layer, 0, 0), pipeline_mode=resident),
            pl.BlockSpec((None, dff, d), lambda i: (layer, 0, 0), pipeline_mode=resident),
            pl.BlockSpec((None, None, 1, d), lambda i: (layer, row_out, 0, 0)),
        ],
        out_specs=pl.BlockSpec((tm, d), lambda i: (i, 0)),
        out_shape=jax.ShapeDtypeStruct((m, d), F32),
        compiler_params=_cparams(("parallel",)),
        name="ffn",
    )(x, nw, wi, wo, nw)


def _proj_kernel(x_ref, nw_ref, w_ref, wvt_ref, wvmt_ref, wgt_ref, cw_ref, cb_ref, gbc_ref, gbr_ref, lb_ref,
                 qm_ref, km_ref, vmt_ref, om_ref, gc_ref, gr_ref, qd_ref, kd_ref, vdt_ref,
                 qh_ref, lfh_ref, kh_ref, vh_ref, gh_ref, cbuf_ref):
    t = pl.program_id(1)
    tm = x_ref.shape[0]
    xn = _rms(x_ref[...], nw_ref[...]).astype(BF16)

    def u(lo, hi):
        return jnp.dot(xn, w_ref[:, lo:hi], preferred_element_type=F32)

    @pl.when(t == 0)
    def _():
        cbuf_ref[0:8, :] = jnp.zeros((8, 2 * M_W), F32)

    qk = u(0, 2 * M_W)
    cbuf_ref[8:8 + tm, :] = qk
    cw = cw_ref[...]
    y = cb_ref[...] + cw[CONV_W - 1:CONV_W] * qk
    for d in range(1, CONV_W):
        y = y + cw[CONV_W - 1 - d:CONV_W - d] * cbuf_ref[8 - d:8 - d + tm, :]
    cbuf_ref[0:8, :] = cbuf_ref[tm:tm + 8, :]
    y = y * _sigmoid(y)
    qm_ref[...] = y[:, 0:M_W].astype(qm_ref.dtype)
    km_ref[...] = (y[:, M_W:2 * M_W] * (HEAD_W ** -0.5)).astype(km_ref.dtype)
    vmt_ref[...] = lax.dot_general(wvmt_ref[...], xn, (((1,), (1,)), ((), ())),
                                   preferred_element_type=F32).astype(vmt_ref.dtype)
    om_ref[...] = u(768, 1024)

    qd_ref[...] = (u(1024, 1536) * (HEAD_W ** -0.5 * LOG2E)).astype(qd_ref.dtype)
    kd_ref[...] = u(1536, 2048).astype(kd_ref.dtype)
    vdt_ref[...] = lax.dot_general(wvt_ref[...], xn, (((1,), (1,)), ((), ())),
                                   preferred_element_type=F32).astype(vdt_ref.dtype)

    qh = u(2048, 2304)
    qh_ref[...] = (qh * _sigmoid(qh)).astype(qh_ref.dtype)
    fp = u(2304, 2560)
    lb = lb_ref[...]
    a = jnp.log(lb)
    bb = jnp.log1p(-lb) + _log_sigmoid(fp)
    lfh_ref[...] = jnp.maximum(a, bb) + jnp.log1p(jnp.exp(-jnp.abs(a - bb)))
    kh_ref[...] = ((1.0 - lb) * _sigmoid(-fp)).astype(kh_ref.dtype)
    vh_ref[...] = u(2560, 2816).astype(vh_ref.dtype)
    gh_ref[...] = u(2816, 3072)

    zc = u(3072, 3072 + GATE_PAD) + gbc_ref[...]
    lane = lax.broadcasted_iota(jnp.int32, (1, GATE_PAD), 1)
    gc_ref[...] = jnp.where(lane < HEADS, zc, _log_sigmoid(zc))
    zr = lax.dot_general(wgt_ref[...], xn, (((1,), (1,)), ((), ())), preferred_element_type=F32) + gbr_ref[...]
    row = lax.broadcasted_iota(jnp.int32, (2 * HEADS, 1), 0)
    gr_ref[...] = jnp.where(row < HEADS, zr, _log_sigmoid(zr))


def _proj(x, nw, layer, w, wvt, wvmt, wgt, cw, cb, gbc, gbr, lb, *, batch, tm, act_dtype):
    m, d = x.shape
    seq = m // batch
    nt = seq // tm
    npc = w.shape[-1]
    tok = lambda width: pl.BlockSpec((tm, width), lambda b, t: (b * nt + t, 0))
    lay = lambda *shape: pl.BlockSpec((None,) + shape, lambda b, t: (layer,) + (0,) * len(shape),
                                      pipeline_mode=pl.Buffered(1))
    sds = lambda width, dt: jax.ShapeDtypeStruct((m, width), dt)
    out_shape = [sds(M_W, act_dtype), sds(M_W, act_dtype),
                 jax.ShapeDtypeStruct((batch, nt, M_W, tm), act_dtype), sds(M_W, F32),
                 sds(GATE_PAD, F32), jax.ShapeDtypeStruct((batch, 2 * HEADS, seq), F32),
                 sds(DF_W, act_dtype), sds(DF_W, act_dtype),
                 jax.ShapeDtypeStruct((batch, nt, DF_W, tm), act_dtype),
                 sds(M_W, act_dtype), sds(M_W, F32), sds(M_W, act_dtype), sds(M_W, act_dtype), sds(M_W, F32)]
    out_specs = [tok(M_W), tok(M_W), pl.BlockSpec((None, None, M_W, tm), lambda b, t: (b, t, 0, 0)), tok(M_W),
                 tok(GATE_PAD),
                 pl.BlockSpec((None, 2 * HEADS, tm), lambda b, t: (b, 0, t)),
                 tok(DF_W), tok(DF_W),
                 pl.BlockSpec((None, None, DF_W, tm), lambda b, t: (b, t, 0, 0)),
                 tok(M_W), tok(M_W), tok(M_W), tok(M_W), tok(M_W)]
    return pl.pallas_call(
        _proj_kernel,
        grid=(batch, nt),
        in_specs=[tok(d),
                  pl.BlockSpec((None, None, 1, d), lambda b, t: (layer, 2, 0, 0)),
                  lay(d, npc), lay(DF_W, d), lay(M_W, d), lay(2 * HEADS, d), lay(CONV_W, 2 * M_W), lay(1, 2 * M_W),
                  lay(1, GATE_PAD), lay(2 * HEADS, 1), lay(1, M_W)],
        out_specs=out_specs,
        out_shape=out_shape,
        scratch_shapes=[pltpu.VMEM((tm + 8, 2 * M_W), F32)],
        compiler_params=_cparams(("parallel", "arbitrary")),
        name="mixer_proj",
    )(x, nw, w, wvt, wvmt, wgt, cw, cb, gbc, gbr, lb)


def _mlstm_kernel(q_ref, k_ref, v_ref, gc_ref, gr_ref, h_ref, c_ref, n_ref, m_ref):
    @pl.when(pl.program_id(1) == 0)
    def _():
        c_ref[...] = jnp.zeros_like(c_ref)
        n_ref[...] = jnp.zeros_like(n_ref)
        m_ref[...] = jnp.zeros_like(m_ref)

    for sq in range(q_ref.shape[0]):
        _mlstm_chunk(*(r.at[sq] for r in (q_ref, k_ref, v_ref, gc_ref, gr_ref, h_ref, c_ref, n_ref, m_ref)))


def _mlstm_chunk(q_ref, k_ref, vt_ref, gc_ref, gr_ref, h_ref, ct_ref, n_ref, m_ref):
    L = q_ref.shape[0]
    qb = q_ref[...]
    kb = k_ref[...]
    vt = vt_ref[...]
    gc = gc_ref[...]
    gr = gr_ref[...]
    row = lax.broadcasted_iota(jnp.int32, (L, L), 0)
    col = lax.broadcasted_iota(jnp.int32, (L, L), 1)
    allowed = row <= col
    bcol = _dot01_lhs((row >= col).astype(BF16), gc, 3)
    brow = _dot01_rhs(gr, allowed.astype(BF16), 3)
    lane_head = _head_of(lax.broadcasted_iota(jnp.int32, (1, M_W), 1))
    r8 = lax.broadcasted_iota(jnp.int32, (2 * HEADS, M_W), 0)
    own8 = _head_of(lax.broadcasted_iota(jnp.int32, (2 * HEADS, M_W), 1)) == r8
    r8l = lax.broadcasted_iota(jnp.int32, (2 * HEADS, L), 0)
    ct = ct_ref[...]
    nrow = n_ref[...]
    nt_dims = (((1,), (1,)), ((), ()))
    qct = lax.dot_general(ct.astype(BF16), qb, nt_dims, preferred_element_type=F32)
    qn8 = None
    for part in _split_bf16(jnp.where(own8, nrow, 0.0), 3):
        t = lax.dot_general(part, qb, nt_dims, preferred_element_type=F32)
        qn8 = t if qn8 is None else qn8 + t
    hts, vws = [], []
    wa8 = jnp.zeros((2 * HEADS, L), F32)
    dec_all = jnp.zeros((1, M_W), F32)
    for h in range(HEADS):
        hm = lane_head == h
        rows = slice(h * HEAD_W, (h + 1) * HEAD_W)
        b_row = brow[HEADS + h:HEADS + h + 1, :]
        keycol = gc[:, h:h + 1] - bcol[:, HEADS + h:HEADS + h + 1]
        mprev = m_ref[h:h + 1, 0:1]
        dmat = jnp.where(allowed, b_row + keycol, -jnp.inf)
        m_inter = b_row + mprev
        mt = jnp.maximum(jnp.max(dmat, axis=0, keepdims=True), m_inter)
        w = jnp.exp(dmat - mt)
        qh = jnp.where(hm, qb, jnp.zeros_like(qb))
        sc = lax.dot_general(kb, qh, nt_dims, preferred_element_type=F32) * w
        g = jnp.exp(m_inter - mt)
        pv = jnp.dot(vt[rows], sc.astype(BF16), preferred_element_type=F32)
        den = jnp.sum(sc, axis=0, keepdims=True) + g * qn8[h:h + 1, :]
        scale = 1.0 / jnp.maximum(jnp.abs(den), jnp.exp(-mt))
        hts.append((pv + g * qct[rows]) * scale)
        bl = b_row[:, L - 1:L]
        a = bl - b_row + gr[h:h + 1, :]
        mnew = jnp.maximum(bl + mprev, jnp.max(a, axis=1, keepdims=True))
        wa = jnp.exp(a - mnew)
        vws.append(vt[rows].astype(F32) * wa)
        wa8 = jnp.where(r8l == h, wa, wa8)
        dec_all = jnp.where(hm, jnp.exp(bl + mprev - mnew), dec_all)
        m_ref[h:h + 1, :] = jnp.broadcast_to(mnew, (1, m_ref.shape[1]))
    h_ref[...] = jnp.concatenate(hts, axis=0).T
    cnew = jnp.dot(jnp.concatenate(vws, axis=0).astype(BF16), kb, preferred_element_type=F32)
    r2 = _head_of(lax.broadcasted_iota(jnp.int32, (M_W, M_W), 0))
    c2 = _head_of(lax.broadcasted_iota(jnp.int32, (M_W, M_W), 1))
    ct_ref[...] = dec_all * ct + jnp.where(r2 == c2, cnew, 0.0)
    kn8 = _dot_split_lhs(wa8, kb)
    n_ref[...] = dec_all * nrow + jnp.sum(jnp.where(own8, kn8, 0.0), axis=0, keepdims=True)


def _dot_split_lhs(x, y_bf16):
    acc = None
    for p in _split_bf16(x, 3):
        t = jnp.dot(p, y_bf16, preferred_element_type=F32)
        acc = t if acc is None else acc + t
    return acc


def _mlstm(qm, km, vmt, gc, gr, *, batch, chunk, nb):
    m = qm.shape[0]
    seq = m // batch
    per = vmt.shape[3] // chunk
    view = lambda a: a.reshape(batch, seq, a.shape[-1])
    tok = lambda width: pl.BlockSpec((nb, chunk, width), lambda b, c: (b, c, 0))
    out = pl.pallas_call(
        _mlstm_kernel,
        grid=(batch // nb, seq // chunk),
        in_specs=[tok(M_W), tok(M_W),
                  pl.BlockSpec((nb, None, M_W, chunk), lambda b, c: (b, c // per, 0, c % per)),
                  tok(GATE_PAD),
                  pl.BlockSpec((nb, 2 * HEADS, chunk), lambda b, c: (b, 0, c))],
        out_specs=tok(M_W),
        out_shape=jax.ShapeDtypeStruct((batch, seq, M_W), F32),
        scratch_shapes=[pltpu.VMEM((nb, M_W, M_W), F32), pltpu.VMEM((nb, 1, M_W), F32),
                        pltpu.VMEM((nb, 8, 128), F32)],
        compiler_params=_cparams(("parallel", "arbitrary")),
        name="mlstm",
    )(view(qm), view(km), vmt, view(gc), gr)
    return out.reshape(m, M_W)


HG_BLK = 16


def _hgrn_kernel(q_ref, k_ref, v_ref, lf_ref, o_ref, st_ref, p_ref):
    @pl.when(pl.program_id(1) == 0)
    def _():
        st_ref[...] = jnp.zeros_like(st_ref)

    nb = q_ref.shape[0]
    nblk = q_ref.shape[1] // HG_BLK
    r16 = lax.broadcasted_iota(jnp.int32, (HG_BLK, HG_BLK), 0)
    c16 = lax.broadcasted_iota(jnp.int32, (HG_BLK, HG_BLK), 1)
    tri16 = (r16 >= c16).astype(BF16)
    rowid = lax.broadcasted_iota(jnp.int32, (HG_BLK, M_W), 0)
    rowid8 = lax.broadcasted_iota(jnp.int32, (HG_BLK // 2, M_W), 0)
    r2 = _head_of(lax.broadcasted_iota(jnp.int32, (M_W, M_W), 0))
    c2 = _head_of(lax.broadcasted_iota(jnp.int32, (M_W, M_W), 1))
    same_head = r2 == c2
    ones_bd = same_head.astype(BF16)

    def block(i, sq):
        r0 = pl.multiple_of(i * HG_BLK, HG_BLK)
        q = q_ref[sq, pl.ds(r0, HG_BLK), :].astype(F32)
        k = k_ref[sq, pl.ds(r0, HG_BLK), :].astype(F32)
        v = v_ref[sq, pl.ds(r0, HG_BLK), :].astype(F32)
        lf = lf_ref[sq, pl.ds(r0, HG_BLK), :]
        b = lf
        for d in (1, 2, 4, 8):
            b = b + jnp.where(rowid >= d, pltpu.roll(b, d, axis=0), 0.0)
        bl = b[HG_BLK - 1:HG_BLK, :]
        st = st_ref[sq]
        o = lax.dot_general((q * jnp.exp(b)).astype(BF16), st.astype(BF16), (((1,), (1,)), ((), ())),
                            preferred_element_type=F32)
        hb = HG_BLK // 2
        pbuf = p_ref.at[sq]
        for half in range(2):
            qq, bq = q[half * hb:(half + 1) * hb], b[half * hb:(half + 1) * hb]
            for s in range(half * hb, (half + 1) * hb):
                dd = jnp.where(rowid8 >= s - half * hb, bq - b[s:s + 1, :], -jnp.inf)
                pbuf[s * hb:(s + 1) * hb, :] = qq * k[s:s + 1, :] * jnp.exp(dd)
        bmid = b[hb - 1:hb, :]
        qe1 = q[hb:] * jnp.exp(b[hb:] - bmid)
        ke0 = k[0:hb] * jnp.exp(bmid - b[0:hb])
        for s in range(hb):
            pbuf[(HG_BLK + s) * hb:(HG_BLK + s + 1) * hb, :] = qe1 * ke0[s:s + 1, :]
        abig = jnp.dot(pbuf[...].astype(BF16), ones_bd, preferred_element_type=F32)
        grp = lambda g: abig[g * hb:(g + 1) * hb, :]
        o0 = grp(0) * v[0:1, :]
        o1 = grp(hb) * v[hb:hb + 1, :] + grp(HG_BLK) * v[0:1, :]
        for s in range(1, hb):
            o0 = o0 + grp(s) * v[s:s + 1, :]
            o1 = o1 + grp(hb + s) * v[hb + s:hb + s + 1, :] + grp(HG_BLK + s) * v[s:s + 1, :]
        o_ref[sq, pl.ds(r0, HG_BLK), :] = o + jnp.concatenate([o0, o1], axis=0)
        ke = (k * jnp.exp(bl - b)).astype(BF16)
        upd = lax.dot_general(v.astype(BF16), ke, (((0,), (0,)), ((), ())), preferred_element_type=F32)
        st_ref[sq] = st * jnp.exp(bl) + jnp.where(same_head, upd, 0.0)

    def body(i, carry):
        for sq in range(nb):
            block(i, sq)
        return carry

    lax.fori_loop(0, nblk, body, 0)


def _hgrn(qh, kh, vh, lfh, *, batch, tile, nb):
    m = qh.shape[0]
    seq = m // batch
    view = lambda a: a.reshape(batch, seq, M_W)
    tok = pl.BlockSpec((nb, tile, M_W), lambda b, t: (b, t, 0))
    out = pl.pallas_call(
        _hgrn_kernel,
        grid=(batch // nb, seq // tile),
        in_specs=[tok, tok, tok, tok],
        out_specs=tok,
        out_shape=jax.ShapeDtypeStruct((batch, seq, M_W), F32),
        scratch_shapes=[pltpu.VMEM((nb, M_W, M_W), F32), pltpu.VMEM((nb, 3 * HG_BLK * HG_BLK // 4, M_W), F32)],
        compiler_params=_cparams(("parallel", "arbitrary")),
        name="hgrn2",
    )(view(qh), view(kh), view(vh), view(lfh))
    return out.reshape(m, M_W)


ONES_ROWS = 16


def _attn_kernel(q_ref, k_ref, vt_ref, bias_ref, lam0_ref, lv_ref, nwt_ref, o_ref, m_ref, acc_ref, sa_ref, sb_ref,
                 mca_ref, mcb_ref, *, tile):
    qi = pl.program_id(2)
    vchunk = vt_ref.shape[2]
    nvc = tile // vchunk
    m_ref[...] = jnp.full_like(m_ref, NEG)
    acc_ref[...] = jnp.zeros_like(acc_ref)
    q = q_ref[...]
    first = lax.broadcasted_iota(jnp.int32, (1, DF_HW), 1) < HEAD_W
    zero = jnp.zeros_like(q)
    qs = (jnp.where(first, q, zero), jnp.where(first, zero, q))
    ones = jnp.ones((ONES_ROWS, vchunk), BF16)

    def scores(ki, s_ref, mc_ref):
        kb = k_ref[pl.ds(pl.multiple_of(ki * tile, tile), tile), :]
        bias = bias_ref[jnp.minimum(qi - ki, 2)]
        for j in range(2):
            st = bias + lax.dot_general(kb, qs[j], (((1,), (1,)), ((), ())), preferred_element_type=F32)
            s_ref[j] = st.astype(BF16)
            mc_ref[j] = jnp.max(st, axis=0, keepdims=True).astype(BF16).astype(F32)

    def softmax_pv(ki, s_ref, mc_ref):
        vts = [jnp.concatenate([vt_ref[ki * nvc + c], ones], axis=0) for c in range(nvc)]
        for j in range(2):
            m_old = m_ref[j]
            m_new = jnp.maximum(m_old, mc_ref[j])
            alpha = jnp.exp2(m_old - m_new)
            pt = jnp.exp2(s_ref[j] - m_new.astype(BF16))
            pv = None
            for c in range(nvc):
                t = jnp.dot(vts[c], pt[c * vchunk:(c + 1) * vchunk, :], preferred_element_type=F32)
                pv = t if pv is None else pv + t
            acc_ref[j] = alpha * acc_ref[j] + pv
            m_ref[j] = m_new

    ntiles = qi + 1
    scores(0, sa_ref, mca_ref)

    def pair(p, carry):
        ka = 2 * p
        scores(ka + 1, sb_ref, mcb_ref)
        softmax_pv(ka, sa_ref, mca_ref)
        scores(ka + 2, sa_ref, mca_ref)
        softmax_pv(ka + 1, sb_ref, mcb_ref)
        return carry

    lax.fori_loop(0, (ntiles - 1) // 2, pair, 0)

    @pl.when(ntiles % 2 == 0)
    def _():
        scores(qi, sb_ref, mcb_ref)
        softmax_pv(qi - 1, sa_ref, mca_ref)
        softmax_pv(qi, sb_ref, mcb_ref)

    @pl.when(ntiles % 2 == 1)
    def _():
        softmax_pv(qi, sa_ref, mca_ref)

    lv = lv_ref[...]
    lam0 = lam0_ref[0]
    lam = (jnp.exp(jnp.sum(lv[0:1] * lv[1:2], axis=1, keepdims=True))
           - jnp.exp(jnp.sum(lv[2:3] * lv[3:4], axis=1, keepdims=True)) + lam0)
    a1 = acc_ref[0]
    a2 = acc_ref[1]
    od = a1[0:DF_HW] / a1[DF_HW:DF_HW + 1] - lam * (a2[0:DF_HW] / a2[DF_HW:DF_HW + 1])
    ms = jnp.mean(od * od, axis=0, keepdims=True)
    yt = od * lax.rsqrt(ms + EPS) * nwt_ref[...] * (1.0 - lam0)
    o_ref[...] = yt.T.astype(o_ref.dtype)


def _diff_attn(qd, kd, vdt, bias_tiles, lam0, lam_vecs, norm_w_t, layer, *, batch, tile):
    m = qd.shape[0]
    seq = m // batch
    nq = seq // tile
    nchunk, vchunk = vdt.shape[1], vdt.shape[3]
    return pl.pallas_call(
        functools.partial(_attn_kernel, tile=tile),
        grid=(HEADS, batch, nq),
        in_specs=[
            pl.BlockSpec((tile, DF_HW), lambda h, b, i: (b * nq + i, h)),
            pl.BlockSpec((seq, DF_HW), lambda h, b, i: (b, h)),
            pl.BlockSpec((None, nchunk, DF_HW, vchunk), lambda h, b, i: (b, 0, h, 0)),
            pl.BlockSpec((None, 3, tile, tile), lambda h, b, i: (h, 0, 0, 0)),
            pl.BlockSpec(memory_space=pltpu.SMEM),
            pl.BlockSpec((None, 4, HEAD_W), lambda h, b, i: (layer, 0, 0)),
            pl.BlockSpec((None, DF_HW, 1), lambda h, b, i: (layer, h, 0)),
        ],
        out_specs=pl.BlockSpec((tile, DF_HW), lambda h, b, i: (b * nq + i, h)),
        out_shape=jax.ShapeDtypeStruct((m, DF_W), BF16),
        scratch_shapes=[pltpu.VMEM((2, 1, tile), F32), pltpu.VMEM((2, DF_HW + ONES_ROWS, tile), F32),
                        pltpu.VMEM((2, tile, tile), BF16), pltpu.VMEM((2, tile, tile), BF16),
                        pltpu.VMEM((2, 1, tile), F32), pltpu.VMEM((2, 1, tile), F32)],
        compiler_params=_cparams(("parallel", "parallel", "arbitrary")),
        name="diff_attn",
    )(qd, kd, vdt, bias_tiles, lam0, lam_vecs, norm_w_t)


def _mix_out_kernel(x_ref, hm_ref, om_ref, yd_ref, oh_ref, gh_ref, mnw_ref, hnw_ref, w_ref, nw_ref, o_ref):
    r2 = _head_of(lax.broadcasted_iota(jnp.int32, (M_W, M_W), 0))
    c2 = _head_of(lax.broadcasted_iota(jnp.int32, (M_W, M_W), 1))
    ones_bd = (r2 == c2).astype(BF16)
    inv = 1.0 / HEAD_W

    hm = hm_ref[...]
    xc = hm - _dot01_rhs(hm, ones_bd, 2) * inv
    var = _dot01_rhs(xc * xc, ones_bd, 2) * inv
    ym = xc * lax.rsqrt(var + EPS) * mnw_ref[...] * _sigmoid(om_ref[...])

    oh = oh_ref[...]
    ms = _dot01_rhs(oh * oh, ones_bd, 2) * inv
    gh = gh_ref[...]
    yh = oh * lax.rsqrt(ms + EPS) * hnw_ref[...] * (gh * _sigmoid(gh))

    w = w_ref[...]
    acc = jnp.dot(ym.astype(BF16), w[0:M_W], preferred_element_type=F32)
    acc = acc + jnp.dot(yd_ref[...].astype(BF16), w[M_W:M_W + DF_W], preferred_element_type=F32)
    acc = acc + jnp.dot(yh.astype(BF16), w[M_W + DF_W:], preferred_element_type=F32)
    o_ref[...] = x_ref[...] + _rms(acc, nw_ref[...])


def _mix_out(x, hm, om, yd, oh, gh, mnw, hnw, w_out, nw, layer, *, tm):
    m, d = x.shape
    tok = lambda width: pl.BlockSpec((tm, width), lambda i: (i, 0))
    lay = lambda *shape: pl.BlockSpec((None,) + shape, lambda i: (layer,) + (0,) * len(shape))
    return pl.pallas_call(
        _mix_out_kernel,
        grid=(m // tm,),
        in_specs=[tok(d), tok(M_W), tok(M_W), tok(DF_W), tok(M_W), tok(M_W),
                  lay(1, M_W), lay(1, M_W), lay(d, d),
                  pl.BlockSpec((None, None, 1, d), lambda i: (layer, 3, 0, 0))],
        out_specs=tok(d),
        out_shape=jax.ShapeDtypeStruct((m, d), F32),
        compiler_params=_cparams(("parallel",)),
        name="mix_out",
    )(x, hm, om, yd, oh, gh, mnw, hnw, w_out, nw)


def _tiles(batch, seq):
    return dict(
        ffn_tm=min(512, seq), proj_tm=min(512, seq), mlstm_chunk=min(256, seq), hgrn_tile=min(256, seq),
        attn_tile=min(512, seq), out_tm=min(512, seq), hgrn_nb=8 if batch % 8 == 0 else 1,
        mlstm_nb=2 if batch % 2 == 0 else 1)


def kernel(x, norm_w, ffn1_wi, ffn1_wo, ffn2_wi, ffn2_wo, w_in, w_out, mlstm_conv_w, mlstm_conv_b, mlstm_igate_b,
           mlstm_fgate_b, mlstm_norm_w, diff_lambda, diff_norm_w, rel_bias, hgrn_lb_logits, hgrn_norm_w):
    batch, seq, d = x.shape
    depth = norm_w.shape[0]
    tl = _tiles(batch, seq)
    act_dtype = BF16

    off = [0] + [int(v) for v in np.cumsum(SPLIT_SIZES)]
    seg = lambda i: w_in[:, :, off[i]:off[i + 1]].astype(BF16)
    gate_pad = jnp.zeros(w_in.shape[:2] + (GATE_PAD - 2 * HEADS,), BF16)
    w_proj = jnp.concatenate([seg(0), seg(1), seg(2), seg(5), seg(6), seg(8), seg(9), seg(10), seg(11),
                              seg(3), seg(4), gate_pad], axis=2)
    w_vd_t = jnp.swapaxes(w_in[:, :, off[7]:off[8]], 1, 2).astype(BF16)
    w_vm_t = jnp.swapaxes(w_in[:, :, off[1]:off[2]], 1, 2).astype(BF16)
    w_gate_t = jnp.swapaxes(w_in[:, :, off[3]:off[5]], 1, 2).astype(BF16)
    gate_b = jnp.concatenate([mlstm_igate_b, mlstm_fgate_b], axis=1).astype(F32)
    gate_b_col = jnp.pad(gate_b, ((0, 0), (0, GATE_PAD - 2 * HEADS)))[:, None, :]
    gate_b_row = gate_b[:, :, None]
    wi1, wo1 = ffn1_wi.astype(BF16), ffn1_wo.astype(BF16)
    wi2, wo2 = ffn2_wi.astype(BF16), ffn2_wo.astype(BF16)
    w_o = w_out.astype(BF16)
    nw = norm_w.astype(F32)[:, :, None, :]
    conv_w = mlstm_conv_w.astype(F32)
    conv_b = mlstm_conv_b.astype(F32)[:, None, :]
    mnw = mlstm_norm_w.astype(F32)[:, None, :]
    hnw = hgrn_norm_w.astype(F32)[:, None, :]
    dnw_t = diff_norm_w.astype(F32)[:, :, None]
    lam_vecs = diff_lambda.astype(F32)

    lb_all = _hgrn_lower_bounds(hgrn_lb_logits)[:, None, :]
    bias_tiles = _rel_bias_tiles(rel_bias, tl["attn_tile"])

    xf = x.reshape(batch * seq, d)
    for l in range(depth):
        lam0 = jnp.full((1,), 0.8 - 0.6 * math.exp(-0.3 * l), F32)
        xf = _ffn(xf, nw, l, 0, 1, wi1, wo1, tm=tl["ffn_tm"])
        (qm, km, vmt, om, gc, gr, qd, kd, vdt, qh, lfh, kh, vh, gh) = _proj(
            xf, nw, l, w_proj, w_vd_t, w_vm_t, w_gate_t, conv_w, conv_b, gate_b_col, gate_b_row, lb_all,
            batch=batch, tm=tl["proj_tm"], act_dtype=act_dtype)
        hm = _mlstm(qm, km, vmt, gc, gr, batch=batch, chunk=tl["mlstm_chunk"], nb=tl["mlstm_nb"])
        yd = _diff_attn(qd, kd, vdt, bias_tiles, lam0, lam_vecs, dnw_t, l, batch=batch, tile=tl["attn_tile"])
        oh = _hgrn(qh, kh, vh, lfh, batch=batch, tile=tl["hgrn_tile"], nb=tl["hgrn_nb"])
        xf = _mix_out(xf, hm, om, yd, oh, gh, mnw, hnw, w_o, nw, l, tm=tl["out_tm"])
        xf = _ffn(xf, nw, l, 4, 5, wi2, wo2, tm=tl["ffn_tm"])
    return xf.reshape(batch, seq, d)
```

```python
import functools
import math

import numpy as np
import jax
import jax.numpy as jnp
from jax import lax
from jax.experimental import pallas as pl
from jax.experimental.pallas import tpu as pltpu

F32 = jnp.float32
BF16 = jnp.bfloat16
EPS = 1e-6
NEG = -1e30
LOG2E = 1.4426950408889634

HEADS = 4
HEAD_W = 64
M_W = HEADS * HEAD_W
DF_HW = 128
DF_W = HEADS * DF_HW
CONV_W = 4
REL_BUCKETS = 32
REL_MAX_EXACT = 16
REL_MAX_DIST = 128
GATE_PAD = 128
SPLIT_SIZES = (2 * M_W, M_W, M_W, HEADS, HEADS, DF_W, DF_W, DF_W, M_W, M_W, M_W, M_W)
VMEM_LIMIT = 56 * 1024 * 1024


def _cparams(sem):
    return pltpu.CompilerParams(dimension_semantics=sem, vmem_limit_bytes=VMEM_LIMIT)


def _sigmoid(x):
    return 1.0 / (1.0 + jnp.exp(-x))


def _log_sigmoid(x):
    return jnp.minimum(x, 0.0) - jnp.log1p(jnp.exp(-jnp.abs(x)))


def _rms(x, w):
    return x * lax.rsqrt(jnp.mean(x * x, axis=-1, keepdims=True) + EPS) * w


def _split_bf16(x, parts):
    out = []
    r = x
    for _ in range(parts):
        p = r.astype(BF16)
        out.append(p)
        r = r - p.astype(F32)
    return out


def _dot01_rhs(x, m01, parts):
    acc = None
    for p in _split_bf16(x, parts):
        t = jnp.dot(p, m01, preferred_element_type=F32)
        acc = t if acc is None else acc + t
    return acc


def _dot01_lhs(m01, x, parts):
    acc = None
    for p in _split_bf16(x, parts):
        t = jnp.dot(m01, p, preferred_element_type=F32)
        acc = t if acc is None else acc + t
    return acc


def _head_of(idx):
    return lax.shift_right_logical(idx, 6)


def _lb_kernel(lg_ref, o_ref):
    lg = lg_ref[...]
    e = jnp.exp(lg - jnp.max(lg, axis=0, keepdims=True))
    sm = e / jnp.sum(e, axis=0, keepdims=True)
    depth = lg.shape[0]
    rows = []
    run = sm[0:1]
    first = run
    for i in range(depth):
        if i > 0:
            run = run + sm[i:i + 1]
        rows.append(jnp.maximum(run - first, 0.0))
    o_ref[...] = jnp.concatenate(rows, axis=0)


def _hgrn_lower_bounds(logits):
    return pl.pallas_call(
        _lb_kernel, out_shape=jax.ShapeDtypeStruct(logits.shape, F32), name="hgrn_lb")(logits.astype(F32))


def _bias_kernel(tab_ref, o_ref, *, tile):
    h = pl.program_id(0)
    typ = pl.program_id(1)
    r = lax.broadcasted_iota(jnp.int32, (tile, tile), 0)
    c = lax.broadcasted_iota(jnp.int32, (tile, tile), 1)
    rel = c - r + typ * tile
    n = jnp.maximum(rel, 0)
    nf = jnp.maximum(n, 1).astype(F32)
    large = REL_MAX_EXACT + (jnp.log(nf / REL_MAX_EXACT) / math.log(REL_MAX_DIST / REL_MAX_EXACT)
                             * (REL_BUCKETS - REL_MAX_EXACT)).astype(jnp.int32)
    large = jnp.minimum(large, REL_BUCKETS - 1)
    bucket = jnp.where(n < REL_MAX_EXACT, n, large)
    bias = jnp.zeros((tile, tile), F32)
    for b in range(REL_BUCKETS):
        bias = jnp.where(bucket == b, tab_ref[b, h], bias)
    o_ref[...] = jnp.where(rel >= 0, bias * LOG2E, NEG)


def _rel_bias_tiles(rel_bias, tile):
    assert tile >= REL_MAX_DIST
    return pl.pallas_call(
        functools.partial(_bias_kernel, tile=tile),
        grid=(HEADS, 3),
        in_specs=[pl.BlockSpec(memory_space=pltpu.SMEM)],
        out_specs=pl.BlockSpec((None, None, tile, tile), lambda h, t: (h, t, 0, 0)),
        out_shape=jax.ShapeDtypeStruct((HEADS, 3, tile, tile), F32),
        name="rel_bias_tiles",
    )(rel_bias.astype(F32))


def _ffn_kernel(x_ref, nwi_ref, wi_ref, wo_ref, nwo_ref, o_ref):
    dff = wo_ref.shape[0]
    x = x_ref[...]
    xn = _rms(x, nwi_ref[...]).astype(BF16)
    g = jnp.dot(xn, wi_ref[:, 0:dff], preferred_element_type=F32)
    u = jnp.dot(xn, wi_ref[:, dff:2 * dff], preferred_element_type=F32)
    a = (g * _sigmoid(g) * u).astype(BF16)
    h = jnp.dot(a, wo_ref[...], preferred_element_type=F32)
    o_ref[...] = x + 0.5 * _rms(h, nwo_ref[...])


def _ffn(x, nw, layer, row_in, row_out, wi, wo, *, tm):
    m, d = x.shape
    dff = wo.shape[1]
    resident = pl.Buffered(1)
    return pl.pallas_call(
        _ffn_kernel,
        grid=(m // tm,),
        in_specs=[
            pl.BlockSpec((tm, d), lambda i: (i, 0)),
            pl.BlockSpec((None, None, 1, d), lambda i: (layer, row_in, 0, 0)),
            pl.BlockSpec((None, d, 2 * dff), lambda i: (layer, 0, 0), pipeline_mode=resident),
            pl.BlockSpec((None, dff, d), lambda i: (layer, 0, 0), pipeline_mode=resident),
            pl.BlockSpec((None, None, 1, d), lambda i: (layer, row_out, 0, 0)),
        ],
        out_specs=pl.BlockSpec((tm, d), lambda i: (i, 0)),
        out_shape=jax.ShapeDtypeStruct((m, d), F32),
        compiler_params=_cparams(("parallel",)),
        name="ffn",
    )(x, nw, wi, wo, nw)


def _proj_kernel(x_ref, nw_ref, w_ref, wt_ref, cw_ref, cb_ref, gbc_ref, gbr_ref, lb_ref,
                 qm_ref, km_ref, vmt_ref, om_ref, gc_ref, gr_ref, qd_ref, kd_ref, vdt_ref,
                 qh_ref, lfh_ref, kh_ref, vh_ref, gh_ref, cbuf_ref):
    t = pl.program_id(1)
    tm = x_ref.shape[0]
    xn = _rms(x_ref[...], nw_ref[...]).astype(BF16)

    def u(lo, hi):
        return jnp.dot(xn, w_ref[:, lo:hi], preferred_element_type=F32)

    @pl.when(t == 0)
    def _():
        cbuf_ref[0:8, :] = jnp.zeros((8, 2 * M_W), F32)

    qk = u(0, 2 * M_W)
    cbuf_ref[8:8 + tm, :] = qk
    cw = cw_ref[...]
    y = cb_ref[...] + cw[CONV_W - 1:CONV_W] * qk
    for d in range(1, CONV_W):
        y = y + cw[CONV_W - 1 - d:CONV_W - d] * cbuf_ref[8 - d:8 - d + tm, :]
    cbuf_ref[0:8, :] = cbuf_ref[tm:tm + 8, :]
    y = y * _sigmoid(y)
    qm_ref[...] = y[:, 0:M_W].astype(qm_ref.dtype)
    km_ref[...] = (y[:, M_W:2 * M_W] * (HEAD_W ** -0.5)).astype(km_ref.dtype)
    ut = lax.dot_general(wt_ref[...], xn, (((1,), (1,)), ((), ())), preferred_element_type=F32)
    vdt_ref[...] = ut[0:DF_W].astype(vdt_ref.dtype)
    vmt_ref[...] = ut[DF_W:DF_W + M_W].astype(vmt_ref.dtype)
    om_ref[...] = u(768, 1024)

    qd_ref[...] = (u(1024, 1536) * (HEAD_W ** -0.5 * LOG2E)).astype(qd_ref.dtype)
    kd_ref[...] = u(1536, 2048).astype(kd_ref.dtype)

    qh = u(2048, 2304)
    qh_ref[...] = (qh * _sigmoid(qh)).astype(qh_ref.dtype)
    fp = u(2304, 2560)
    lb = lb_ref[...]
    a = jnp.log(lb)
    bb = jnp.log1p(-lb) + _log_sigmoid(fp)
    lfh_ref[...] = jnp.maximum(a, bb) + jnp.log1p(jnp.exp(-jnp.abs(a - bb)))
    kh_ref[...] = ((1.0 - lb) * _sigmoid(-fp)).astype(kh_ref.dtype)
    vh_ref[...] = u(2560, 2816).astype(vh_ref.dtype)
    gh_ref[...] = u(2816, 3072)

    zc = u(3072, 3072 + GATE_PAD) + gbc_ref[...]
    lane = lax.broadcasted_iota(jnp.int32, (1, GATE_PAD), 1)
    gc_ref[...] = jnp.where(lane < HEADS, zc, _log_sigmoid(zc))
    zr = ut[DF_W + M_W:DF_W + M_W + 2 * HEADS] + gbr_ref[...]
    row = lax.broadcasted_iota(jnp.int32, (2 * HEADS, 1), 0)
    gr_ref[...] = jnp.where(row < HEADS, zr, _log_sigmoid(zr))


def _proj(x, nw, layer, w, wt, cw, cb, gbc, gbr, lb, *, batch, tm, act_dtype):
    m, d = x.shape
    seq = m // batch
    nt = seq // tm
    npc = w.shape[-1]
    tok = lambda width: pl.BlockSpec((tm, width), lambda b, t: (b * nt + t, 0))
    lay = lambda *shape: pl.BlockSpec((None,) + shape, lambda b, t: (layer,) + (0,) * len(shape),
                                      pipeline_mode=pl.Buffered(1))
    sds = lambda width, dt: jax.ShapeDtypeStruct((m, width), dt)
    out_shape = [sds(M_W, act_dtype), sds(M_W, act_dtype),
                 jax.ShapeDtypeStruct((batch, nt, M_W, tm), act_dtype), sds(M_W, F32),
                 sds(GATE_PAD, F32), jax.ShapeDtypeStruct((batch, 2 * HEADS, seq), F32),
                 sds(DF_W, act_dtype), sds(DF_W, act_dtype),
                 jax.ShapeDtypeStruct((batch, nt, DF_W, tm), act_dtype),
                 sds(M_W, act_dtype), sds(M_W, F32), sds(M_W, act_dtype), sds(M_W, act_dtype), sds(M_W, F32)]
    out_specs = [tok(M_W), tok(M_W), pl.BlockSpec((None, None, M_W, tm), lambda b, t: (b, t, 0, 0)), tok(M_W),
                 tok(GATE_PAD),
                 pl.BlockSpec((None, 2 * HEADS, tm), lambda b, t: (b, 0, t)),
                 tok(DF_W), tok(DF_W),
                 pl.BlockSpec((None, None, DF_W, tm), lambda b, t: (b, t, 0, 0)),
                 tok(M_W), tok(M_W), tok(M_W), tok(M_W), tok(M_W)]
    return pl.pallas_call(
        _proj_kernel,
        grid=(batch, nt),
        in_specs=[tok(d),
                  pl.BlockSpec((None, None, 1, d), lambda b, t: (layer, 2, 0, 0)),
                  lay(d, npc), lay(wt.shape[1], d), lay(CONV_W, 2 * M_W), lay(1, 2 * M_W),
                  lay(1, GATE_PAD), lay(2 * HEADS, 1), lay(1, M_W)],
        out_specs=out_specs,
        out_shape=out_shape,
        scratch_shapes=[pltpu.VMEM((tm + 8, 2 * M_W), F32)],
        compiler_params=_cparams(("parallel", "arbitrary")),
        name="mixer_proj",
    )(x, nw, w, wt, cw, cb, gbc, gbr, lb)


def _mlstm_kernel(q_ref, k_ref, v_ref, gc_ref, gr_ref, h_ref, c_ref, n_ref, m_ref):
    @pl.when(pl.program_id(1) == 0)
    def _():
        c_ref[...] = jnp.zeros_like(c_ref)
        n_ref[...] = jnp.zeros_like(n_ref)
        m_ref[...] = jnp.zeros_like(m_ref)

    for sq in range(q_ref.shape[0]):
        _mlstm_chunk(*(r.at[sq] for r in (q_ref, k_ref, v_ref, gc_ref, gr_ref, h_ref, c_ref, n_ref, m_ref)))


def _mlstm_chunk(q_ref, k_ref, vt_ref, gc_ref, gr_ref, h_ref, ct_ref, n_ref, m_ref):
    L = q_ref.shape[0]
    qb = q_ref[...]
    kb = k_ref[...]
    vt = vt_ref[...]
    gc = gc_ref[...]
    gr = gr_ref[...]
    row = lax.broadcasted_iota(jnp.int32, (L, L), 0)
    col = lax.broadcasted_iota(jnp.int32, (L, L), 1)
    allowed = row <= col
    bcol = _dot01_lhs((row >= col).astype(BF16), gc, 3)
    brow = _dot01_rhs(gr, allowed.astype(BF16), 3)
    lane_head = _head_of(lax.broadcasted_iota(jnp.int32, (1, M_W), 1))
    r8 = lax.broadcasted_iota(jnp.int32, (2 * HEADS, M_W), 0)
    own8 = _head_of(lax.broadcasted_iota(jnp.int32, (2 * HEADS, M_W), 1)) == r8
    r8l = lax.broadcasted_iota(jnp.int32, (2 * HEADS, L), 0)
    ct = ct_ref[...]
    nrow = n_ref[...]
    nt_dims = (((1,), (1,)), ((), ()))
    qct = lax.dot_general(ct.astype(BF16), qb, nt_dims, preferred_element_type=F32)
    qn8 = None
    for part in _split_bf16(jnp.where(own8, nrow, 0.0), 3):
        t = lax.dot_general(part, qb, nt_dims, preferred_element_type=F32)
        qn8 = t if qn8 is None else qn8 + t
    hts, vws = [], []
    wa8 = jnp.zeros((2 * HEADS, L), F32)
    dec_all = jnp.zeros((1, M_W), F32)
    for h in range(HEADS):
        hm = lane_head == h
        rows = slice(h * HEAD_W, (h + 1) * HEAD_W)
        b_row = brow[HEADS + h:HEADS + h + 1, :]
        keycol = gc[:, h:h + 1] - bcol[:, HEADS + h:HEADS + h + 1]
        mprev = m_ref[h:h + 1, 0:1]
        dmat = jnp.where(allowed, b_row + keycol, -jnp.inf)
        m_inter = b_row + mprev
        mt = jnp.maximum(jnp.max(dmat, axis=0, keepdims=True), m_inter)
        w = jnp.exp(dmat - mt)
        qh = jnp.where(hm, qb, jnp.zeros_like(qb))
        sc = lax.dot_general(kb, qh, nt_dims, preferred_element_type=F32) * w
        g = jnp.exp(m_inter - mt)
        pv = jnp.dot(vt[rows], sc.astype(BF16), preferred_element_type=F32)
        den = jnp.sum(sc, axis=0, keepdims=True) + g * qn8[h:h + 1, :]
        scale = 1.0 / jnp.maximum(jnp.abs(den), jnp.exp(-mt))
        hh = (pv + g * qct[rows]) * scale
        hc = hh - jnp.mean(hh, axis=0, keepdims=True)
        hts.append(hc * lax.rsqrt(jnp.mean(hc * hc, axis=0, keepdims=True) + EPS))
        bl = b_row[:, L - 1:L]
        a = bl - b_row + gr[h:h + 1, :]
        mnew = jnp.maximum(bl + mprev, jnp.max(a, axis=1, keepdims=True))
        wa = jnp.exp(a - mnew)
        vws.append(vt[rows].astype(F32) * wa)
        wa8 = jnp.where(r8l == h, wa, wa8)
        dec_all = jnp.where(hm, jnp.exp(bl + mprev - mnew), dec_all)
        m_ref[h:h + 1, :] = jnp.broadcast_to(mnew, (1, m_ref.shape[1]))
    h_ref[...] = jnp.concatenate(hts, axis=0).T
    cnew = jnp.dot(jnp.concatenate(vws, axis=0).astype(BF16), kb, preferred_element_type=F32)
    r2 = _head_of(lax.broadcasted_iota(jnp.int32, (M_W, M_W), 0))
    c2 = _head_of(lax.broadcasted_iota(jnp.int32, (M_W, M_W), 1))
    ct_ref[...] = dec_all * ct + jnp.where(r2 == c2, cnew, 0.0)
    kn8 = _dot_split_lhs(wa8, kb)
    n_ref[...] = dec_all * nrow + jnp.sum(jnp.where(own8, kn8, 0.0), axis=0, keepdims=True)


def _dot_split_lhs(x, y_bf16):
    acc = None
    for p in _split_bf16(x, 3):
        t = jnp.dot(p, y_bf16, preferred_element_type=F32)
        acc = t if acc is None else acc + t
    return acc


def _mlstm(qm, km, vmt, gc, gr, *, batch, chunk, nb):
    m = qm.shape[0]
    seq = m // batch
    per = vmt.shape[3] // chunk
    view = lambda a: a.reshape(batch, seq, a.shape[-1])
    tok = lambda width: pl.BlockSpec((nb, chunk, width), lambda b, c: (b, c, 0))
    out = pl.pallas_call(
        _mlstm_kernel,
        grid=(batch // nb, seq // chunk),
        in_specs=[tok(M_W), tok(M_W),
                  pl.BlockSpec((nb, None, M_W, chunk), lambda b, c: (b, c // per, 0, c % per)),
                  tok(GATE_PAD),
                  pl.BlockSpec((nb, 2 * HEADS, chunk), lambda b, c: (b, 0, c))],
        out_specs=tok(M_W),
        out_shape=jax.ShapeDtypeStruct((batch, seq, M_W), F32),
        scratch_shapes=[pltpu.VMEM((nb, M_W, M_W), F32), pltpu.VMEM((nb, 1, M_W), F32),
                        pltpu.VMEM((nb, 8, 128), F32)],
        compiler_params=_cparams(("parallel", "arbitrary")),
        name="mlstm",
    )(view(qm), view(km), vmt, view(gc), gr)
    return out.reshape(m, M_W)


HG_BLK = 16


def _hgrn_kernel(q_ref, k_ref, v_ref, lf_ref, o_ref, st_ref, p_ref):
    @pl.when(pl.program_id(1) == 0)
    def _():
        st_ref[...] = jnp.zeros_like(st_ref)

    nb = q_ref.shape[0]
    nblk = q_ref.shape[1] // HG_BLK
    r16 = lax.broadcasted_iota(jnp.int32, (HG_BLK, HG_BLK), 0)
    c16 = lax.broadcasted_iota(jnp.int32, (HG_BLK, HG_BLK), 1)
    tri16 = (r16 >= c16).astype(BF16)
    rowid = lax.broadcasted_iota(jnp.int32, (HG_BLK, M_W), 0)
    rowid8 = lax.broadcasted_iota(jnp.int32, (HG_BLK // 2, M_W), 0)
    r2 = _head_of(lax.broadcasted_iota(jnp.int32, (M_W, M_W), 0))
    c2 = _head_of(lax.broadcasted_iota(jnp.int32, (M_W, M_W), 1))
    same_head = r2 == c2
    ones_bd = same_head.astype(BF16)

    def block(i, sq):
        r0 = pl.multiple_of(i * HG_BLK, HG_BLK)
        q = q_ref[sq, pl.ds(r0, HG_BLK), :].astype(F32)
        k = k_ref[sq, pl.ds(r0, HG_BLK), :].astype(F32)
        v = v_ref[sq, pl.ds(r0, HG_BLK), :].astype(F32)
        lf = lf_ref[sq, pl.ds(r0, HG_BLK), :]
        b = lf
        for d in (1, 2, 4, 8):
            b = b + jnp.where(rowid >= d, pltpu.roll(b, d, axis=0), 0.0)
        bl = b[HG_BLK - 1:HG_BLK, :]
        st = st_ref[sq]
        o = lax.dot_general((q * jnp.exp(b)).astype(BF16), st.astype(BF16), (((1,), (1,)), ((), ())),
                            preferred_element_type=F32)
        hb = HG_BLK // 2
        pbuf = p_ref.at[sq]
        for half in range(2):
            qq, bq = q[half * hb:(half + 1) * hb], b[half * hb:(half + 1) * hb]
            for s in range(half * hb, (half + 1) * hb):
                dd = jnp.where(rowid8 >= s - half * hb, bq - b[s:s + 1, :], -jnp.inf)
                pbuf[s * hb:(s + 1) * hb, :] = qq * k[s:s + 1, :] * jnp.exp(dd)
        bmid = b[hb - 1:hb, :]
        qe1 = q[hb:] * jnp.exp(b[hb:] - bmid)
        ke0 = k[0:hb] * jnp.exp(bmid - b[0:hb])
        for s in range(hb):
            pbuf[(HG_BLK + s) * hb:(HG_BLK + s + 1) * hb, :] = qe1 * ke0[s:s + 1, :]
        abig = jnp.dot(pbuf[...].astype(BF16), ones_bd, preferred_element_type=F32)
        grp = lambda g: abig[g * hb:(g + 1) * hb, :]
        o0 = grp(0) * v[0:1, :]
        o1 = grp(hb) * v[hb:hb + 1, :] + grp(HG_BLK) * v[0:1, :]
        for s in range(1, hb):
            o0 = o0 + grp(s) * v[s:s + 1, :]
            o1 = o1 + grp(hb + s) * v[hb + s:hb + s + 1, :] + grp(HG_BLK + s) * v[s:s + 1, :]
        o_ref[sq, pl.ds(r0, HG_BLK), :] = o + jnp.concatenate([o0, o1], axis=0)
        ke = (k * jnp.exp(bl - b)).astype(BF16)
        upd = lax.dot_general(v.astype(BF16), ke, (((0,), (0,)), ((), ())), preferred_element_type=F32)
        st_ref[sq] = st * jnp.exp(bl) + jnp.where(same_head, upd, 0.0)

    def body(i, carry):
        for sq in range(nb):
            block(i, sq)
        return carry

    lax.fori_loop(0, nblk, body, 0)


def _hgrn(qh, kh, vh, lfh, *, batch, tile, nb):
    m = qh.shape[0]
    seq = m // batch
    view = lambda a: a.reshape(batch, seq, M_W)
    tok = pl.BlockSpec((nb, tile, M_W), lambda b, t: (b, t, 0))
    out = pl.pallas_call(
        _hgrn_kernel,
        grid=(batch // nb, seq // tile),
        in_specs=[tok, tok, tok, tok],
        out_specs=tok,
        out_shape=jax.ShapeDtypeStruct((batch, seq, M_W), F32),
        scratch_shapes=[pltpu.VMEM((nb, M_W, M_W), F32), pltpu.VMEM((nb, 3 * HG_BLK * HG_BLK // 4, M_W), F32)],
        compiler_params=_cparams(("parallel", "arbitrary")),
        name="hgrn2",
    )(view(qh), view(kh), view(vh), view(lfh))
    return out.reshape(m, M_W)


ONES_ROWS = 16


def _attn_kernel(q_ref, k_ref, vt_ref, bias_ref, lam0_ref, lv_ref, nwt_ref, o_ref, m_ref, acc_ref, sa_ref, sb_ref,
                 mca_ref, mcb_ref, *, tile):
    qi = pl.program_id(2)
    vchunk = vt_ref.shape[2]
    nvc = tile // vchunk
    m_ref[...] = jnp.full_like(m_ref, NEG)
    acc_ref[...] = jnp.zeros_like(acc_ref)
    q = q_ref[...]
    first = lax.broadcasted_iota(jnp.int32, (1, DF_HW), 1) < HEAD_W
    zero = jnp.zeros_like(q)
    qs = (jnp.where(first, q, zero), jnp.where(first, zero, q))
    ones = jnp.ones((ONES_ROWS, vchunk), BF16)

    def scores(ki, s_ref, mc_ref):
        kb = k_ref[pl.ds(pl.multiple_of(ki * tile, tile), tile), :]
        bias = bias_ref[jnp.minimum(qi - ki, 2)]
        for j in range(2):
            st = bias + lax.dot_general(kb, qs[j], (((1,), (1,)), ((), ())), preferred_element_type=F32)
            s_ref[j] = st.astype(BF16)
            mc_ref[j] = jnp.max(st, axis=0, keepdims=True).astype(BF16).astype(F32)

    def softmax_pv(ki, s_ref, mc_ref):
        vts = [jnp.concatenate([vt_ref[ki * nvc + c], ones], axis=0) for c in range(nvc)]
        for j in range(2):
            m_old = m_ref[j]
            m_new = jnp.maximum(m_old, mc_ref[j])
            alpha = jnp.exp2(m_old - m_new)
            pt = jnp.exp2(s_ref[j] - m_new.astype(BF16))
            pv = None
            for c in range(nvc):
                t = jnp.dot(vts[c], pt[c * vchunk:(c + 1) * vchunk, :], preferred_element_type=F32)
                pv = t if pv is None else pv + t
            acc_ref[j] = alpha * acc_ref[j] + pv
            m_ref[j] = m_new

    ntiles = qi + 1
    scores(0, sa_ref, mca_ref)

    def pair(p, carry):
        ka = 2 * p
        scores(ka + 1, sb_ref, mcb_ref)
        softmax_pv(ka, sa_ref, mca_ref)
        scores(ka + 2, sa_ref, mca_ref)
        softmax_pv(ka + 1, sb_ref, mcb_ref)
        return carry

    lax.fori_loop(0, (ntiles - 1) // 2, pair, 0)

    @pl.when(ntiles % 2 == 0)
    def _():
        scores(qi, sb_ref, mcb_ref)
        softmax_pv(qi - 1, sa_ref, mca_ref)
        softmax_pv(qi, sb_ref, mcb_ref)

    @pl.when(ntiles % 2 == 1)
    def _():
        softmax_pv(qi, sa_ref, mca_ref)

    lv = lv_ref[...]
    lam0 = lam0_ref[0]
    lam = (jnp.exp(jnp.sum(lv[0:1] * lv[1:2], axis=1, keepdims=True))
           - jnp.exp(jnp.sum(lv[2:3] * lv[3:4], axis=1, keepdims=True)) + lam0)
    a1 = acc_ref[0]
    a2 = acc_ref[1]
    od = a1[0:DF_HW] / a1[DF_HW:DF_HW + 1] - lam * (a2[0:DF_HW] / a2[DF_HW:DF_HW + 1])
    ms = jnp.mean(od * od, axis=0, keepdims=True)
    yt = od * lax.rsqrt(ms + EPS) * nwt_ref[...] * (1.0 - lam0)
    o_ref[...] = yt.T.astype(o_ref.dtype)


def _diff_attn(qd, kd, vdt, bias_tiles, lam0, lam_vecs, norm_w_t, layer, *, batch, tile):
    m = qd.shape[0]
    seq = m // batch
    nq = seq // tile
    nchunk, vchunk = vdt.shape[1], vdt.shape[3]
    return pl.pallas_call(
        functools.partial(_attn_kernel, tile=tile),
        grid=(HEADS, batch, nq),
        in_specs=[
            pl.BlockSpec((tile, DF_HW), lambda h, b, i: (b * nq + i, h)),
            pl.BlockSpec((seq, DF_HW), lambda h, b, i: (b, h)),
            pl.BlockSpec((None, nchunk, DF_HW, vchunk), lambda h, b, i: (b, 0, h, 0)),
            pl.BlockSpec((None, 3, tile, tile), lambda h, b, i: (h, 0, 0, 0)),
            pl.BlockSpec(memory_space=pltpu.SMEM),
            pl.BlockSpec((None, 4, HEAD_W), lambda h, b, i: (layer, 0, 0)),
            pl.BlockSpec((None, DF_HW, 1), lambda h, b, i: (layer, h, 0)),
        ],
        out_specs=pl.BlockSpec((tile, DF_HW), lambda h, b, i: (b * nq + i, h)),
        out_shape=jax.ShapeDtypeStruct((m, DF_W), BF16),
        scratch_shapes=[pltpu.VMEM((2, 1, tile), F32), pltpu.VMEM((2, DF_HW + ONES_ROWS, tile), F32),
                        pltpu.VMEM((2, tile, tile), BF16), pltpu.VMEM((2, tile, tile), BF16),
                        pltpu.VMEM((2, 1, tile), F32), pltpu.VMEM((2, 1, tile), F32)],
        compiler_params=_cparams(("parallel", "parallel", "arbitrary")),
        name="diff_attn",
    )(qd, kd, vdt, bias_tiles, lam0, lam_vecs, norm_w_t)


def _mix_out_kernel(x_ref, hm_ref, om_ref, yd_ref, oh_ref, gh_ref, mnw_ref, hnw_ref, w_ref, nw_ref, o_ref):
    r2 = _head_of(lax.broadcasted_iota(jnp.int32, (M_W, M_W), 0))
    c2 = _head_of(lax.broadcasted_iota(jnp.int32, (M_W, M_W), 1))
    ones_bd = (r2 == c2).astype(BF16)
    inv = 1.0 / HEAD_W

    ym = hm_ref[...] * mnw_ref[...] * _sigmoid(om_ref[...])

    oh = oh_ref[...]
    ms = _dot01_rhs(oh * oh, ones_bd, 2) * inv
    gh = gh_ref[...]
    yh = oh * lax.rsqrt(ms + EPS) * hnw_ref[...] * (gh * _sigmoid(gh))

    w = w_ref[...]
    acc = jnp.dot(ym.astype(BF16), w[0:M_W], preferred_element_type=F32)
    acc = acc + jnp.dot(yd_ref[...].astype(BF16), w[M_W:M_W + DF_W], preferred_element_type=F32)
    acc = acc + jnp.dot(yh.astype(BF16), w[M_W + DF_W:], preferred_element_type=F32)
    o_ref[...] = x_ref[...] + _rms(acc, nw_ref[...])


def _mix_out(x, hm, om, yd, oh, gh, mnw, hnw, w_out, nw, layer, *, tm):
    m, d = x.shape
    tok = lambda width: pl.BlockSpec((tm, width), lambda i: (i, 0))
    lay = lambda *shape: pl.BlockSpec((None,) + shape, lambda i: (layer,) + (0,) * len(shape))
    return pl.pallas_call(
        _mix_out_kernel,
        grid=(m // tm,),
        in_specs=[tok(d), tok(M_W), tok(M_W), tok(DF_W), tok(M_W), tok(M_W),
                  lay(1, M_W), lay(1, M_W), lay(d, d),
                  pl.BlockSpec((None, None, 1, d), lambda i: (layer, 3, 0, 0))],
        out_specs=tok(d),
        out_shape=jax.ShapeDtypeStruct((m, d), F32),
        compiler_params=_cparams(("parallel",)),
        name="mix_out",
    )(x, hm, om, yd, oh, gh, mnw, hnw, w_out, nw)


def _tiles(batch, seq):
    return dict(
        ffn_tm=min(512, seq), proj_tm=min(512, seq), mlstm_chunk=min(256, seq), hgrn_tile=min(256, seq),
        attn_tile=min(512, seq), out_tm=min(512, seq), hgrn_nb=8 if batch % 8 == 0 else 1,
        mlstm_nb=2 if batch % 2 == 0 else 1)


def kernel(x, norm_w, ffn1_wi, ffn1_wo, ffn2_wi, ffn2_wo, w_in, w_out, mlstm_conv_w, mlstm_conv_b, mlstm_igate_b,
           mlstm_fgate_b, mlstm_norm_w, diff_lambda, diff_norm_w, rel_bias, hgrn_lb_logits, hgrn_norm_w):
    batch, seq, d = x.shape
    depth = norm_w.shape[0]
    tl = _tiles(batch, seq)
    act_dtype = BF16

    off = [0] + [int(v) for v in np.cumsum(SPLIT_SIZES)]
    seg = lambda i: w_in[:, :, off[i]:off[i + 1]].astype(BF16)
    gate_pad = jnp.zeros(w_in.shape[:2] + (GATE_PAD - 2 * HEADS,), BF16)
    w_proj = jnp.concatenate([seg(0), seg(1), seg(2), seg(5), seg(6), seg(8), seg(9), seg(10), seg(11),
                              seg(3), seg(4), gate_pad], axis=2)
    w_t = jnp.swapaxes(jnp.concatenate([seg(7), seg(1), seg(3), seg(4),
                                        jnp.zeros(w_in.shape[:2] + (2 * HEADS,), BF16)], axis=2), 1, 2)
    gate_b = jnp.concatenate([mlstm_igate_b, mlstm_fgate_b], axis=1).astype(F32)
    gate_b_col = jnp.pad(gate_b, ((0, 0), (0, GATE_PAD - 2 * HEADS)))[:, None, :]
    gate_b_row = gate_b[:, :, None]
    wi1, wo1 = ffn1_wi.astype(BF16), ffn1_wo.astype(BF16)
    wi2, wo2 = ffn2_wi.astype(BF16), ffn2_wo.astype(BF16)
    w_o = w_out.astype(BF16)
    nw = norm_w.astype(F32)[:, :, None, :]
    conv_w = mlstm_conv_w.astype(F32)
    conv_b = mlstm_conv_b.astype(F32)[:, None, :]
    mnw = mlstm_norm_w.astype(F32)[:, None, :]
    hnw = hgrn_norm_w.astype(F32)[:, None, :]
    dnw_t = diff_norm_w.astype(F32)[:, :, None]
    lam_vecs = diff_lambda.astype(F32)

    lb_all = _hgrn_lower_bounds(hgrn_lb_logits)[:, None, :]
    bias_tiles = _rel_bias_tiles(rel_bias, tl["attn_tile"])

    xf = x.reshape(batch * seq, d)
    for l in range(depth):
        lam0 = jnp.full((1,), 0.8 - 0.6 * math.exp(-0.3 * l), F32)
        xf = _ffn(xf, nw, l, 0, 1, wi1, wo1, tm=tl["ffn_tm"])
        (qm, km, vmt, om, gc, gr, qd, kd, vdt, qh, lfh, kh, vh, gh) = _proj(
            xf, nw, l, w_proj, w_t, conv_w, conv_b, gate_b_col, gate_b_row, lb_all,
            batch=batch, tm=tl["proj_tm"], act_dtype=act_dtype)
        hm = _mlstm(qm, km, vmt, gc, gr, batch=batch, chunk=tl["mlstm_chunk"], nb=tl["mlstm_nb"])
        yd = _diff_attn(qd, kd, vdt, bias_tiles, lam0, lam_vecs, dnw_t, l, batch=batch, tile=tl["attn_tile"])
        oh = _hgrn(qh, kh, vh, lfh, batch=batch, tile=tl["hgrn_tile"], nb=tl["hgrn_nb"])
        xf = _mix_out(xf, hm, om, yd, oh, gh, mnw, hnw, w_o, nw, l, tm=tl["out_tm"])
        xf = _ffn(xf, nw, l, 4, 5, wi2, wo2, tm=tl["ffn_tm"])
    return xf.reshape(batch, seq, d)
```

```python
import functools
import math

import numpy as np
import jax
import jax.numpy as jnp
from jax import lax
from jax.experimental import pallas as pl
from jax.experimental.pallas import tpu as pltpu

F32 = jnp.float32
BF16 = jnp.bfloat16
EPS = 1e-6
NEG = -1e30
LOG2E = 1.4426950408889634

HEADS = 4
HEAD_W = 64
M_W = HEADS * HEAD_W
DF_HW = 128
DF_W = HEADS * DF_HW
CONV_W = 4
REL_BUCKETS = 32
REL_MAX_EXACT = 16
REL_MAX_DIST = 128
GATE_PAD = 128
SPLIT_SIZES = (2 * M_W, M_W, M_W, HEADS, HEADS, DF_W, DF_W, DF_W, M_W, M_W, M_W, M_W)
VMEM_LIMIT = 56 * 1024 * 1024


def _cparams(sem):
    return pltpu.CompilerParams(dimension_semantics=sem, vmem_limit_bytes=VMEM_LIMIT)


def _sigmoid(x):
    return 1.0 / (1.0 + jnp.exp(-x))


def _log_sigmoid(x):
    return jnp.minimum(x, 0.0) - jnp.log1p(jnp.exp(-jnp.abs(x)))


def _rms(x, w):
    return x * lax.rsqrt(jnp.mean(x * x, axis=-1, keepdims=True) + EPS) * w


def _split_bf16(x, parts):
    out = []
    r = x
    for _ in range(parts):
        p = r.astype(BF16)
        out.append(p)
        r = r - p.astype(F32)
    return out


def _dot01_rhs(x, m01, parts):
    acc = None
    for p in _split_bf16(x, parts):
        t = jnp.dot(p, m01, preferred_element_type=F32)
        acc = t if acc is None else acc + t
    return acc


def _dot01_lhs(m01, x, parts):
    acc = None
    for p in _split_bf16(x, parts):
        t = jnp.dot(m01, p, preferred_element_type=F32)
        acc = t if acc is None else acc + t
    return acc


def _head_of(idx):
    return lax.shift_right_logical(idx, HEAD_W.bit_length() - 1)


def _lb_kernel(lg_ref, o_ref):
    lg = lg_ref[...]
    e = jnp.exp(lg - jnp.max(lg, axis=0, keepdims=True))
    sm = e / jnp.sum(e, axis=0, keepdims=True)
    depth = lg.shape[0]
    rows = []
    run = sm[0:1]
    first = run
    for i in range(depth):
        if i > 0:
            run = run + sm[i:i + 1]
        rows.append(jnp.maximum(run - first, 0.0))
    o_ref[...] = jnp.concatenate(rows, axis=0)


def _hgrn_lower_bounds(logits):
    return pl.pallas_call(
        _lb_kernel, out_shape=jax.ShapeDtypeStruct(logits.shape, F32), name="hgrn_lb")(logits.astype(F32))


def _bias_kernel(tab_ref, o_ref, *, tile):
    h = pl.program_id(0)
    typ = pl.program_id(1)
    r = lax.broadcasted_iota(jnp.int32, (tile, tile), 0)
    c = lax.broadcasted_iota(jnp.int32, (tile, tile), 1)
    rel = c - r + typ * tile
    n = jnp.maximum(rel, 0)
    nf = jnp.maximum(n, 1).astype(F32)
    large = REL_MAX_EXACT + (jnp.log(nf / REL_MAX_EXACT) / math.log(REL_MAX_DIST / REL_MAX_EXACT)
                             * (REL_BUCKETS - REL_MAX_EXACT)).astype(jnp.int32)
    large = jnp.minimum(large, REL_BUCKETS - 1)
    bucket = jnp.where(n < REL_MAX_EXACT, n, large)
    bias = jnp.zeros((tile, tile), F32)
    for b in range(REL_BUCKETS):
        bias = jnp.where(bucket == b, tab_ref[b, h], bias)
    o_ref[...] = jnp.where(rel >= 0, bias * LOG2E, NEG)


def _rel_bias_tiles(rel_bias, tile):
    assert tile >= REL_MAX_DIST
    return pl.pallas_call(
        functools.partial(_bias_kernel, tile=tile),
        grid=(HEADS, 3),
        in_specs=[pl.BlockSpec(memory_space=pltpu.SMEM)],
        out_specs=pl.BlockSpec((None, None, tile, tile), lambda h, t: (h, t, 0, 0)),
        out_shape=jax.ShapeDtypeStruct((HEADS, 3, tile, tile), F32),
        name="rel_bias_tiles",
    )(rel_bias.astype(F32))


def _ffn_kernel(x_ref, nwi_ref, wi_ref, wo_ref, nwo_ref, o_ref):
    dff = wo_ref.shape[0]
    x = x_ref[...]
    xn = _rms(x, nwi_ref[...]).astype(BF16)
    g = jnp.dot(xn, wi_ref[:, 0:dff], preferred_element_type=F32)
    u = jnp.dot(xn, wi_ref[:, dff:2 * dff], preferred_element_type=F32)
    a = (g * _sigmoid(g) * u).astype(BF16)
    h = jnp.dot(a, wo_ref[...], preferred_element_type=F32)
    o_ref[...] = x + 0.5 * _rms(h, nwo_ref[...])


def _ffn(x, nw, layer, row_in, row_out, wi, wo, *, tm):
    m, d = x.shape
    dff = wo.shape[1]
    resident = pl.Buffered(1)
    return pl.pallas_call(
        _ffn_kernel,
        grid=(m // tm,),
        in_specs=[
            pl.BlockSpec((tm, d), lambda i: (i, 0)),
            pl.BlockSpec((None, None, 1, d), lambda i: (layer, row_in, 0, 0)),
            pl.BlockSpec((None, d, 2 * dff), lambda i: (layer, 0, 0), pipeline_mode=resident),
            pl.BlockSpec((None, dff, d), lambda i: (layer, 0, 0), pipeline_mode=resident),
            pl.BlockSpec((None, None, 1, d), lambda i: (layer, row_out, 0, 0)),
        ],
        out_specs=pl.BlockSpec((tm, d), lambda i: (i, 0)),
        out_shape=jax.ShapeDtypeStruct((m, d), F32),
        compiler_params=_cparams(("parallel",)),
        name="ffn",
    )(x, nw, wi, wo, nw)


def _proj_kernel(x_ref, nw_ref, w_ref, wt_ref, cw_ref, cb_ref, gbc_ref, gbr_ref, lb_ref,
                 qm_ref, km_ref, vmt_ref, om_ref, gc_ref, gr_ref, qd_ref, kd_ref, vdt_ref,
                 qh_ref, lfh_ref, kh_ref, vh_ref, gh_ref, cbuf_ref):
    t = pl.program_id(1)
    tm = x_ref.shape[0]
    xn = _rms(x_ref[...], nw_ref[...]).astype(BF16)

    def u(lo, hi):
        return jnp.dot(xn, w_ref[:, lo:hi], preferred_element_type=F32)

    @pl.when(t == 0)
    def _():
        cbuf_ref[0:8, :] = jnp.zeros((8, 2 * M_W), F32)

    qk = u(0, 2 * M_W)
    cbuf_ref[8:8 + tm, :] = qk
    cw = cw_ref[...]
    y = cb_ref[...] + cw[CONV_W - 1:CONV_W] * qk
    for d in range(1, CONV_W):
        y = y + cw[CONV_W - 1 - d:CONV_W - d] * cbuf_ref[8 - d:8 - d + tm, :]
    cbuf_ref[0:8, :] = cbuf_ref[tm:tm + 8, :]
    y = y * _sigmoid(y)
    qm_ref[...] = y[:, 0:M_W].astype(qm_ref.dtype)
    km_ref[...] = (y[:, M_W:2 * M_W] * (HEAD_W ** -0.5)).astype(km_ref.dtype)
    ut = lax.dot_general(wt_ref[...], xn, (((1,), (1,)), ((), ())), preferred_element_type=F32)
    vdt_ref[...] = ut[0:DF_W].astype(vdt_ref.dtype)
    vmt_ref[...] = ut[DF_W:DF_W + M_W].astype(vmt_ref.dtype)
    om_ref[...] = u(768, 1024)

    qd_ref[...] = (u(1024, 1536) * (HEAD_W ** -0.5 * LOG2E)).astype(qd_ref.dtype)
    kd_ref[...] = u(1536, 2048).astype(kd_ref.dtype)

    qh = u(2048, 2304)
    qh_ref[...] = (qh * _sigmoid(qh)).astype(qh_ref.dtype)
    fp = u(2304, 2560)
    lb = lb_ref[...]
    a = jnp.log(lb)
    bb = jnp.log1p(-lb) + _log_sigmoid(fp)
    lfh_ref[...] = jnp.maximum(a, bb) + jnp.log1p(jnp.exp(-jnp.abs(a - bb)))
    kh_ref[...] = ((1.0 - lb) * _sigmoid(-fp)).astype(kh_ref.dtype)
    vh_ref[...] = u(2560, 2816).astype(vh_ref.dtype)
    gh_ref[...] = u(2816, 3072)

    zc = u(3072, 3072 + GATE_PAD) + gbc_ref[...]
    lane = lax.broadcasted_iota(jnp.int32, (1, GATE_PAD), 1)
    gc_ref[...] = jnp.where(lane < HEADS, zc, _log_sigmoid(zc))
    zr = ut[DF_W + M_W:DF_W + M_W + 2 * HEADS] + gbr_ref[...]
    row = lax.broadcasted_iota(jnp.int32, (2 * HEADS, 1), 0)
    gr_ref[...] = jnp.where(row < HEADS, zr, _log_sigmoid(zr))


def _proj(x, nw, layer, w, wt, cw, cb, gbc, gbr, lb, *, batch, tm, act_dtype):
    m, d = x.shape
    seq = m // batch
    nt = seq // tm
    npc = w.shape[-1]
    tok = lambda width: pl.BlockSpec((tm, width), lambda b, t: (b * nt + t, 0))
    lay = lambda *shape: pl.BlockSpec((None,) + shape, lambda b, t: (layer,) + (0,) * len(shape),
                                      pipeline_mode=pl.Buffered(1))
    sds = lambda width, dt: jax.ShapeDtypeStruct((m, width), dt)
    out_shape = [sds(M_W, act_dtype), sds(M_W, act_dtype),
                 jax.ShapeDtypeStruct((batch, nt, M_W, tm), act_dtype), sds(M_W, F32),
                 sds(GATE_PAD, F32), jax.ShapeDtypeStruct((batch, 2 * HEADS, seq), F32),
                 sds(DF_W, act_dtype), sds(DF_W, act_dtype),
                 jax.ShapeDtypeStruct((batch, nt, DF_W, tm), act_dtype),
                 sds(M_W, act_dtype), sds(M_W, F32), sds(M_W, act_dtype), sds(M_W, act_dtype), sds(M_W, F32)]
    out_specs = [tok(M_W), tok(M_W), pl.BlockSpec((None, None, M_W, tm), lambda b, t: (b, t, 0, 0)), tok(M_W),
                 tok(GATE_PAD),
                 pl.BlockSpec((None, 2 * HEADS, tm), lambda b, t: (b, 0, t)),
                 tok(DF_W), tok(DF_W),
                 pl.BlockSpec((None, None, DF_W, tm), lambda b, t: (b, t, 0, 0)),
                 tok(M_W), tok(M_W), tok(M_W), tok(M_W), tok(M_W)]
    return pl.pallas_call(
        _proj_kernel,
        grid=(batch, nt),
        in_specs=[tok(d),
                  pl.BlockSpec((None, None, 1, d), lambda b, t: (layer, 2, 0, 0)),
                  lay(d, npc), lay(wt.shape[1], d), lay(CONV_W, 2 * M_W), lay(1, 2 * M_W),
                  lay(1, GATE_PAD), lay(2 * HEADS, 1), lay(1, M_W)],
        out_specs=out_specs,
        out_shape=out_shape,
        scratch_shapes=[pltpu.VMEM((tm + 8, 2 * M_W), F32)],
        compiler_params=_cparams(("parallel", "arbitrary")),
        name="mixer_proj",
    )(x, nw, w, wt, cw, cb, gbc, gbr, lb)


def _mlstm_kernel(q_ref, k_ref, v_ref, gc_ref, gr_ref, h_ref, c_ref, n_ref, m_ref):
    @pl.when(pl.program_id(1) == 0)
    def _():
        c_ref[...] = jnp.zeros_like(c_ref)
        n_ref[...] = jnp.zeros_like(n_ref)
        m_ref[...] = jnp.zeros_like(m_ref)

    for sq in range(q_ref.shape[0]):
        _mlstm_chunk(*(r.at[sq] for r in (q_ref, k_ref, v_ref, gc_ref, gr_ref, h_ref, c_ref, n_ref, m_ref)))


def _mlstm_chunk(q_ref, k_ref, vt_ref, gc_ref, gr_ref, h_ref, ct_ref, n_ref, m_ref):
    L = q_ref.shape[0]
    qb = q_ref[...]
    kb = k_ref[...]
    vt = vt_ref[...]
    gc = gc_ref[...]
    gr = gr_ref[...]
    row = lax.broadcasted_iota(jnp.int32, (L, L), 0)
    col = lax.broadcasted_iota(jnp.int32, (L, L), 1)
    allowed = row <= col
    bcol = _dot01_lhs((row >= col).astype(BF16), gc, 3)
    brow = _dot01_rhs(gr, allowed.astype(BF16), 3)
    lane_head = _head_of(lax.broadcasted_iota(jnp.int32, (1, M_W), 1))
    r8 = lax.broadcasted_iota(jnp.int32, (2 * HEADS, M_W), 0)
    own8 = _head_of(lax.broadcasted_iota(jnp.int32, (2 * HEADS, M_W), 1)) == r8
    r8l = lax.broadcasted_iota(jnp.int32, (2 * HEADS, L), 0)
    ct = ct_ref[...]
    nrow = n_ref[...]
    nt_dims = (((1,), (1,)), ((), ()))
    qct = lax.dot_general(ct.astype(BF16), qb, nt_dims, preferred_element_type=F32)
    qn8 = None
    for part in _split_bf16(jnp.where(own8, nrow, 0.0), 3):
        t = lax.dot_general(part, qb, nt_dims, preferred_element_type=F32)
        qn8 = t if qn8 is None else qn8 + t
    hts, vws = [], []
    wa8 = jnp.zeros((2 * HEADS, L), F32)
    dec_all = jnp.zeros((1, M_W), F32)
    for h in range(HEADS):
        hm = lane_head == h
        rows = slice(h * HEAD_W, (h + 1) * HEAD_W)
        b_row = brow[HEADS + h:HEADS + h + 1, :]
        keycol = gc[:, h:h + 1] - bcol[:, HEADS + h:HEADS + h + 1]
        mprev = m_ref[h:h + 1, 0:1]
        dmat = jnp.where(allowed, b_row + keycol, -jnp.inf)
        m_inter = b_row + mprev
        mt = jnp.maximum(jnp.max(dmat, axis=0, keepdims=True), m_inter)
        w = jnp.exp(dmat - mt)
        qh = jnp.where(hm, qb, jnp.zeros_like(qb))
        sc = lax.dot_general(kb, qh, nt_dims, preferred_element_type=F32) * w
        g = jnp.exp(m_inter - mt)
        pv = jnp.dot(vt[rows], sc.astype(BF16), preferred_element_type=F32)
        den = jnp.sum(sc, axis=0, keepdims=True) + g * qn8[h:h + 1, :]
        scale = 1.0 / jnp.maximum(jnp.abs(den), jnp.exp(-mt))
        hh = (pv + g * qct[rows]) * scale
        hc = hh - jnp.mean(hh, axis=0, keepdims=True)
        hts.append(hc * lax.rsqrt(jnp.mean(hc * hc, axis=0, keepdims=True) + EPS))
        bl = b_row[:, L - 1:L]
        a = bl - b_row + gr[h:h + 1, :]
        mnew = jnp.maximum(bl + mprev, jnp.max(a, axis=1, keepdims=True))
        wa = jnp.exp(a - mnew)
        vws.append(vt[rows].astype(F32) * wa)
        wa8 = jnp.where(r8l == h, wa, wa8)
        dec_all = jnp.where(hm, jnp.exp(bl + mprev - mnew), dec_all)
        m_ref[h:h + 1, :] = jnp.broadcast_to(mnew, (1, m_ref.shape[1]))
    h_ref[...] = jnp.concatenate(hts, axis=0).T
    cnew = jnp.dot(jnp.concatenate(vws, axis=0).astype(BF16), kb, preferred_element_type=F32)
    r2 = _head_of(lax.broadcasted_iota(jnp.int32, (M_W, M_W), 0))
    c2 = _head_of(lax.broadcasted_iota(jnp.int32, (M_W, M_W), 1))
    ct_ref[...] = dec_all * ct + jnp.where(r2 == c2, cnew, 0.0)
    kn8 = _dot_split_lhs(wa8, kb)
    n_ref[...] = dec_all * nrow + jnp.sum(jnp.where(own8, kn8, 0.0), axis=0, keepdims=True)


def _dot_split_lhs(x, y_bf16):
    acc = None
    for p in _split_bf16(x, 3):
        t = jnp.dot(p, y_bf16, preferred_element_type=F32)
        acc = t if acc is None else acc + t
    return acc


def _mlstm(qm, km, vmt, gc, gr, *, batch, chunk, nb):
    m = qm.shape[0]
    seq = m // batch
    per = vmt.shape[3] // chunk
    view = lambda a: a.reshape(batch, seq, a.shape[-1])
    tok = lambda width: pl.BlockSpec((nb, chunk, width), lambda b, c: (b, c, 0))
    out = pl.pallas_call(
        _mlstm_kernel,
        grid=(batch // nb, seq // chunk),
        in_specs=[tok(M_W), tok(M_W),
                  pl.BlockSpec((nb, None, M_W, chunk), lambda b, c: (b, c // per, 0, c % per)),
                  tok(GATE_PAD),
                  pl.BlockSpec((nb, 2 * HEADS, chunk), lambda b, c: (b, 0, c))],
        out_specs=tok(M_W),
        out_shape=jax.ShapeDtypeStruct((batch, seq, M_W), F32),
        scratch_shapes=[pltpu.VMEM((nb, M_W, M_W), F32), pltpu.VMEM((nb, 1, M_W), F32),
                        pltpu.VMEM((nb, 8, 128), F32)],
        compiler_params=_cparams(("parallel", "arbitrary")),
        name="mlstm",
    )(view(qm), view(km), vmt, view(gc), gr)
    return out.reshape(m, M_W)


HG_BLK = 16


def _hgrn_kernel(q_ref, k_ref, v_ref, lf_ref, o_ref, st_ref, p_ref):
    @pl.when(pl.program_id(1) == 0)
    def _():
        st_ref[...] = jnp.zeros_like(st_ref)

    nb = q_ref.shape[0]
    nblk = q_ref.shape[1] // HG_BLK
    rowid =lax.broadcasted_iota(jnp.int32, (HG_BLK, M_W), 0)
    rowid8 = lax.broadcasted_iota(jnp.int32, (HG_BLK // 2, M_W), 0)
    r2 = _head_of(lax.broadcasted_iota(jnp.int32, (M_W, M_W), 0))
    c2 = _head_of(lax.broadcasted_iota(jnp.int32, (M_W, M_W), 1))
    same_head = r2 == c2
    ones_bd = same_head.astype(BF16)

    def block(i, sq):
        r0 = pl.multiple_of(i * HG_BLK, HG_BLK)
        q = q_ref[sq, pl.ds(r0, HG_BLK), :].astype(F32)
        k = k_ref[sq, pl.ds(r0, HG_BLK), :].astype(F32)
        v = v_ref[sq, pl.ds(r0, HG_BLK), :].astype(F32)
        lf = lf_ref[sq, pl.ds(r0, HG_BLK), :]
        b = lf
        for d in (1, 2, 4, 8):
            b = b + jnp.where(rowid >= d, pltpu.roll(b, d, axis=0), 0.0)
        bl = b[HG_BLK - 1:HG_BLK, :]
        st = st_ref[sq]
        o = lax.dot_general((q * jnp.exp(b)).astype(BF16), st.astype(BF16), (((1,), (1,)), ((), ())),
                            preferred_element_type=F32)
        hb = HG_BLK // 2
        pbuf = p_ref.at[sq]
        for half in range(2):
            qq, bq = q[half * hb:(half + 1) * hb], b[half * hb:(half + 1) * hb]
            for s in range(half * hb, (half + 1) * hb):
                dd = jnp.where(rowid8 >= s - half * hb, bq - b[s:s + 1, :], -jnp.inf)
                pbuf[s * hb:(s + 1) * hb, :] = qq * k[s:s + 1, :] * jnp.exp(dd)
        bmid = b[hb - 1:hb, :]
        qe1 = q[hb:] * jnp.exp(b[hb:] - bmid)
        ke0 = k[0:hb] * jnp.exp(bmid - b[0:hb])
        for s in range(hb):
            pbuf[(HG_BLK + s) * hb:(HG_BLK + s + 1) * hb, :] = qe1 * ke0[s:s + 1, :]
        abig = jnp.dot(pbuf[...].astype(BF16), ones_bd, preferred_element_type=F32)
        grp = lambda g: abig[g * hb:(g + 1) * hb, :]
        o0 = grp(0) * v[0:1, :]
        o1 = grp(hb) * v[hb:hb + 1, :] + grp(HG_BLK) * v[0:1, :]
        for s in range(1, hb):
            o0 = o0 + grp(s) * v[s:s + 1, :]
            o1 = o1 + grp(hb + s) * v[hb + s:hb + s + 1, :] + grp(HG_BLK + s) * v[s:s + 1, :]
        o_ref[sq, pl.ds(r0, HG_BLK), :] = o + jnp.concatenate([o0, o1], axis=0)
        ke = (k * jnp.exp(bl - b)).astype(BF16)
        upd = lax.dot_general(v.astype(BF16), ke, (((0,), (0,)), ((), ())), preferred_element_type=F32)
        st_ref[sq] = st * jnp.exp(bl) + jnp.where(same_head, upd, 0.0)

    def body(i, carry):
        for sq in range(nb):
            block(i, sq)
        return carry

    lax.fori_loop(0, nblk, body, 0)


def _hgrn(qh, kh, vh, lfh, *, batch, tile, nb):
    m = qh.shape[0]
    seq = m // batch
    view = lambda a: a.reshape(batch, seq, M_W)
    tok = pl.BlockSpec((nb, tile, M_W), lambda b, t: (b, t, 0))
    out = pl.pallas_call(
        _hgrn_kernel,
        grid=(batch // nb, seq // tile),
        in_specs=[tok, tok, tok, tok],
        out_specs=tok,
        out_shape=jax.ShapeDtypeStruct((batch, seq, M_W), F32),
        scratch_shapes=[pltpu.VMEM((nb, M_W, M_W), F32), pltpu.VMEM((nb, 3 * HG_BLK * HG_BLK // 4, M_W), F32)],
        compiler_params=_cparams(("parallel", "arbitrary")),
        name="hgrn2",
    )(view(qh), view(kh), view(vh), view(lfh))
    return out.reshape(m, M_W)


ONES_ROWS = 16


def _attn_kernel(q_ref, k_ref, vt_ref, bias_ref, lam0_ref, lv_ref, nwt_ref, o_ref, m_ref, acc_ref, sa_ref, sb_ref,
                 mca_ref, mcb_ref, *, tile):
    qi = pl.program_id(2)
    vchunk = vt_ref.shape[2]
    nvc = tile // vchunk
    m_ref[...] = jnp.full_like(m_ref, NEG)
    acc_ref[...] = jnp.zeros_like(acc_ref)
    q = q_ref[...]
    first = lax.broadcasted_iota(jnp.int32, (1, DF_HW), 1) < HEAD_W
    zero = jnp.zeros_like(q)
    qs = (jnp.where(first, q, zero), jnp.where(first, zero, q))
    ones = jnp.ones((ONES_ROWS, vchunk), BF16)

    def scores(ki, s_ref, mc_ref):
        kb = k_ref[pl.ds(pl.multiple_of(ki * tile, tile), tile), :]
        bias = bias_ref[jnp.minimum(qi - ki, 2)]
        for j in range(2):
            st = bias + lax.dot_general(kb, qs[j], (((1,), (1,)), ((), ())), preferred_element_type=F32)
            s_ref[j] = st.astype(BF16)
            mc_ref[j] = jnp.max(st, axis=0, keepdims=True).astype(BF16).astype(F32)

    def softmax_pv(ki, s_ref, mc_ref):
        vts = [jnp.concatenate([vt_ref[ki * nvc + c], ones], axis=0) for c in range(nvc)]
        for j in range(2):
            m_old = m_ref[j]
            m_new = jnp.maximum(m_old, mc_ref[j])
            alpha = jnp.exp2(m_old - m_new)
            pt = jnp.exp2(s_ref[j] - m_new.astype(BF16))
            pv = None
            for c in range(nvc):
                t = jnp.dot(vts[c], pt[c * vchunk:(c + 1) * vchunk, :], preferred_element_type=F32)
                pv = t if pv is None else pv + t
            acc_ref[j] = alpha * acc_ref[j] + pv
            m_ref[j] = m_new

    ntiles = qi + 1
    scores(0, sa_ref, mca_ref)

    def pair(p, carry):
        ka = 2 * p
        scores(ka + 1, sb_ref, mcb_ref)
        softmax_pv(ka, sa_ref, mca_ref)
        scores(ka + 2, sa_ref, mca_ref)
        softmax_pv(ka + 1, sb_ref, mcb_ref)
        return carry

    lax.fori_loop(0, (ntiles - 1) // 2, pair, 0)

    @pl.when(ntiles % 2 == 0)
    def _():
        scores(qi, sb_ref, mcb_ref)
        softmax_pv(qi - 1, sa_ref, mca_ref)
        softmax_pv(qi, sb_ref, mcb_ref)

    @pl.when(ntiles % 2 == 1)
    def _():
        softmax_pv(qi, sa_ref, mca_ref)

    lv = lv_ref[...]
    lam0 = lam0_ref[0]
    lam = (jnp.exp(jnp.sum(lv[0:1] * lv[1:2], axis=1, keepdims=True))
           - jnp.exp(jnp.sum(lv[2:3] * lv[3:4], axis=1, keepdims=True)) + lam0)
    a1 = acc_ref[0]
    a2 = acc_ref[1]
    od = a1[0:DF_HW] / a1[DF_HW:DF_HW + 1] - lam * (a2[0:DF_HW] / a2[DF_HW:DF_HW + 1])
    ms = jnp.mean(od * od, axis=0, keepdims=True)
    yt = od * lax.rsqrt(ms + EPS) * nwt_ref[...] * (1.0 - lam0)
    o_ref[...] = yt.T.astype(o_ref.dtype)


def _diff_attn(qd, kd, vdt, bias_tiles, lam0, lam_vecs, norm_w_t, layer, *, batch, tile):
    m = qd.shape[0]
    seq = m // batch
    nq = seq // tile
    nchunk, vchunk = vdt.shape[1], vdt.shape[3]
    return pl.pallas_call(
        functools.partial(_attn_kernel, tile=tile),
        grid=(HEADS, batch, nq),
        in_specs=[
            pl.BlockSpec((tile, DF_HW), lambda h, b, i: (b * nq + i, h)),
            pl.BlockSpec((seq, DF_HW), lambda h, b, i: (b, h)),
            pl.BlockSpec((None, nchunk, DF_HW, vchunk), lambda h, b, i: (b, 0, h, 0)),
            pl.BlockSpec((None, 3, tile, tile), lambda h, b, i: (h, 0, 0, 0)),
            pl.BlockSpec(memory_space=pltpu.SMEM),
            pl.BlockSpec((None, 4, HEAD_W), lambda h, b, i: (layer, 0, 0)),
            pl.BlockSpec((None, DF_HW, 1), lambda h, b, i: (layer, h, 0)),
        ],
        out_specs=pl.BlockSpec((tile, DF_HW), lambda h, b, i: (b * nq + i, h)),
        out_shape=jax.ShapeDtypeStruct((m, DF_W), BF16),
        scratch_shapes=[pltpu.VMEM((2, 1, tile), F32), pltpu.VMEM((2, DF_HW + ONES_ROWS, tile), F32),
                        pltpu.VMEM((2, tile, tile), BF16), pltpu.VMEM((2, tile, tile), BF16),
                        pltpu.VMEM((2, 1, tile), F32), pltpu.VMEM((2, 1, tile), F32)],
        compiler_params=_cparams(("parallel", "parallel", "arbitrary")),
        name="diff_attn",
    )(qd, kd, vdt, bias_tiles, lam0, lam_vecs, norm_w_t)


def _mix_out_kernel(x_ref, hm_ref, om_ref, yd_ref, oh_ref, gh_ref, mnw_ref, hnw_ref, w_ref, nw_ref, o_ref):
    r2 = _head_of(lax.broadcasted_iota(jnp.int32, (M_W, M_W), 0))
    c2 = _head_of(lax.broadcasted_iota(jnp.int32, (M_W, M_W), 1))
    ones_bd = (r2 == c2).astype(BF16)
    inv = 1.0 / HEAD_W

    ym = hm_ref[...] * mnw_ref[...] * _sigmoid(om_ref[...])

    oh = oh_ref[...]
    ms = _dot01_rhs(oh * oh, ones_bd, 2) * inv
    gh = gh_ref[...]
    yh = oh * lax.rsqrt(ms + EPS) * hnw_ref[...] * (gh * _sigmoid(gh))

    w = w_ref[...]
    acc = jnp.dot(ym.astype(BF16), w[0:M_W], preferred_element_type=F32)
    acc = acc + jnp.dot(yd_ref[...].astype(BF16), w[M_W:M_W + DF_W], preferred_element_type=F32)
    acc = acc + jnp.dot(yh.astype(BF16), w[M_W + DF_W:], preferred_element_type=F32)
    o_ref[...] = x_ref[...] + _rms(acc, nw_ref[...])


def _mix_out(x, hm, om, yd, oh, gh, mnw, hnw, w_out, nw, layer, *, tm):
    m, d = x.shape
    tok = lambda width: pl.BlockSpec((tm, width), lambda i: (i, 0))
    lay = lambda *shape: pl.BlockSpec((None,) + shape, lambda i: (layer,) + (0,) * len(shape))
    return pl.pallas_call(
        _mix_out_kernel,
        grid=(m // tm,),
        in_specs=[tok(d), tok(M_W), tok(M_W), tok(DF_W), tok(M_W), tok(M_W),
                  lay(1, M_W), lay(1, M_W), lay(d, d),
                  pl.BlockSpec((None, None, 1, d), lambda i: (layer, 3, 0, 0))],
        out_specs=tok(d),
        out_shape=jax.ShapeDtypeStruct((m, d), F32),
        compiler_params=_cparams(("parallel",)),
        name="mix_out",
    )(x, hm, om, yd, oh, gh, mnw, hnw, w_out, nw)


def _tiles(batch, seq):
    return dict(
        ffn_tm=min(512, seq), proj_tm=min(512, seq), mlstm_chunk=min(256, seq), hgrn_tile=min(256, seq),
        attn_tile=min(512, seq), out_tm=min(512, seq), hgrn_nb=8 if batch % 8 == 0 else 1,
        mlstm_nb=2 if batch % 2 == 0 else 1)


def kernel(x, norm_w, ffn1_wi, ffn1_wo, ffn2_wi, ffn2_wo, w_in, w_out, mlstm_conv_w, mlstm_conv_b, mlstm_igate_b,
           mlstm_fgate_b, mlstm_norm_w, diff_lambda, diff_norm_w, rel_bias, hgrn_lb_logits, hgrn_norm_w):
    batch, seq, d = x.shape
    depth = norm_w.shape[0]
    tl = _tiles(batch, seq)
    act_dtype = BF16

    off = [0] + [int(v) for v in np.cumsum(SPLIT_SIZES)]
    seg = lambda i: w_in[:, :, off[i]:off[i + 1]].astype(BF16)
    gate_pad = jnp.zeros(w_in.shape[:2] + (GATE_PAD - 2 * HEADS,), BF16)
    w_proj = jnp.concatenate([seg(0), seg(1), seg(2), seg(5), seg(6), seg(8), seg(9), seg(10), seg(11),
                              seg(3), seg(4), gate_pad], axis=2)
    w_t = jnp.swapaxes(jnp.concatenate([seg(7), seg(1), seg(3), seg(4),
                                        jnp.zeros(w_in.shape[:2] + (2 * HEADS,), BF16)], axis=2), 1, 2)
    gate_b = jnp.concatenate([mlstm_igate_b, mlstm_fgate_b], axis=1).astype(F32)
    gate_b_col = jnp.pad(gate_b, ((0, 0), (0, GATE_PAD - 2 * HEADS)))[:, None, :]
    gate_b_row = gate_b[:, :, None]
    wi1, wo1 = ffn1_wi.astype(BF16), ffn1_wo.astype(BF16)
    wi2, wo2 = ffn2_wi.astype(BF16), ffn2_wo.astype(BF16)
    w_o = w_out.astype(BF16)
    nw = norm_w.astype(F32)[:, :, None, :]
    conv_w = mlstm_conv_w.astype(F32)
    conv_b = mlstm_conv_b.astype(F32)[:, None, :]
    mnw = mlstm_norm_w.astype(F32)[:, None, :]
    hnw = hgrn_norm_w.astype(F32)[:, None, :]
    dnw_t = diff_norm_w.astype(F32)[:, :, None]
    lam_vecs = diff_lambda.astype(F32)

    lb_all = _hgrn_lower_bounds(hgrn_lb_logits)[:, None, :]
    bias_tiles = _rel_bias_tiles(rel_bias, tl["attn_tile"])

    xf = x.reshape(batch * seq, d)
    for l in range(depth):
        lam0 = jnp.full((1,), 0.8 - 0.6 * math.exp(-0.3 * l), F32)
        xf = _ffn(xf, nw, l, 0, 1, wi1, wo1, tm=tl["ffn_tm"])
        (qm, km, vmt, om, gc, gr, qd, kd, vdt, qh, lfh, kh, vh, gh) = _proj(
            xf, nw, l, w_proj, w_t, conv_w, conv_b, gate_b_col, gate_b_row, lb_all,
            batch=batch, tm=tl["proj_tm"], act_dtype=act_dtype)
        hm = _mlstm(qm, km, vmt, gc, gr, batch=batch, chunk=tl["mlstm_chunk"], nb=tl["mlstm_nb"])
        yd = _diff_attn(qd, kd, vdt, bias_tiles, lam0, lam_vecs, dnw_t, l, batch=batch, tile=tl["attn_tile"])
        oh = _hgrn(qh, kh, vh, lfh, batch=batch, tile=tl["hgrn_tile"], nb=tl["hgrn_nb"])
        xf = _mix_out(xf, hm, om, yd, oh, gh, mnw, hnw, w_o, nw, l, tm=tl["out_tm"])
        xf = _ffn(xf, nw, l, 4, 5, wi2, wo2, tm=tl["ffn_tm"])
    return xf.reshape(batch, seq, d)
```

```python
import functools
import math

import numpy as np
import jax
import jax.numpy as jnp
from jax import lax
from jax.experimental import pallas as pl
from jax.experimental.pallas import tpu as pltpu

F32 = jnp.float32
BF16 = jnp.bfloat16
EPS = 1e-6
NEG = -1e30
LOG2E = 1.4426950408889634

HEADS = 4
HEAD_W = 64
M_W = HEADS * HEAD_W
DF_HW = 128
DF_W = HEADS * DF_HW
CONV_W = 4
REL_BUCKETS = 32
REL_MAX_EXACT = 16
REL_MAX_DIST = 128
GATE_PAD = 128
SPLIT_SIZES = (2 * M_W, M_W, M_W, HEADS, HEADS, DF_W, DF_W, DF_W, M_W, M_W, M_W, M_W)
VMEM_LIMIT = 56 * 1024 * 1024


def _cparams(sem):
    return pltpu.CompilerParams(dimension_semantics=sem, vmem_limit_bytes=VMEM_LIMIT)


def _sigmoid(x):
    return 1.0 / (1.0 + jnp.exp(-x))


def _log_sigmoid(x):
    return jnp.minimum(x, 0.0) - jnp.log1p(jnp.exp(-jnp.abs(x)))


def _rms(x, w):
    return x * lax.rsqrt(jnp.mean(x * x, axis=-1, keepdims=True) + EPS) * w


def _split_bf16(x, parts):
    out = []
    r = x
    for _ in range(parts):
        p = r.astype(BF16)
        out.append(p)
        r = r - p.astype(F32)
    return out


def _dot01_rhs(x, m01, parts):
    acc = None
    for p in _split_bf16(x, parts):
        t = jnp.dot(p, m01, preferred_element_type=F32)
        acc = t if acc is None else acc + t
    return acc


def _dot01_lhs(m01, x, parts):
    acc = None
    for p in _split_bf16(x, parts):
        t = jnp.dot(m01, p, preferred_element_type=F32)
        acc = t if acc is None else acc + t
    return acc


def _head_of(idx):
    return lax.shift_right_logical(idx, HEAD_W.bit_length() - 1)


def _lb_kernel(lg_ref, o_ref):
    lg = lg_ref[...]
    e = jnp.exp(lg - jnp.max(lg, axis=0, keepdims=True))
    sm = e / jnp.sum(e, axis=0, keepdims=True)
    depth = lg.shape[0]
    rows = []
    run = sm[0:1]
    first = run
    for i in range(depth):
        if i > 0:
            run = run + sm[i:i + 1]
        rows.append(jnp.maximum(run - first, 0.0))
    o_ref[...] = jnp.concatenate(rows, axis=0)


def _hgrn_lower_bounds(logits):
    return pl.pallas_call(
        _lb_kernel, out_shape=jax.ShapeDtypeStruct(logits.shape, F32), name="hgrn_lb")(logits.astype(F32))


def _bias_kernel(tab_ref, o_ref, *, tile):
    h = pl.program_id(0)
    typ = pl.program_id(1)
    r = lax.broadcasted_iota(jnp.int32, (tile, tile), 0)
    c = lax.broadcasted_iota(jnp.int32, (tile, tile), 1)
    rel = c - r + typ * tile
    n = jnp.maximum(rel, 0)
    nf = jnp.maximum(n, 1).astype(F32)
    large = REL_MAX_EXACT + (jnp.log(nf / REL_MAX_EXACT) / math.log(REL_MAX_DIST / REL_MAX_EXACT)
                             * (REL_BUCKETS - REL_MAX_EXACT)).astype(jnp.int32)
    large = jnp.minimum(large, REL_BUCKETS - 1)
    bucket = jnp.where(n < REL_MAX_EXACT, n, large)
    bias = jnp.zeros((tile, tile), F32)
    for b in range(REL_BUCKETS):
        bias = jnp.where(bucket == b, tab_ref[b, h], bias)
    o_ref[...] = jnp.where(rel >= 0, bias * LOG2E, NEG)


def _rel_bias_tiles(rel_bias, tile):
    assert tile >= REL_MAX_DIST
    return pl.pallas_call(
        functools.partial(_bias_kernel, tile=tile),
        grid=(HEADS, 3),
        in_specs=[pl.BlockSpec(memory_space=pltpu.SMEM)],
        out_specs=pl.BlockSpec((None, None, tile, tile), lambda h, t: (h, t, 0, 0)),
        out_shape=jax.ShapeDtypeStruct((HEADS, 3, tile, tile), F32),
        name="rel_bias_tiles",
    )(rel_bias.astype(F32))


def _ffn_kernel(x_ref, nwi_ref, wi_ref, wo_ref, nwo_ref, o_ref):
    dff = wo_ref.shape[0]
    x = x_ref[...]
    xn = _rms(x, nwi_ref[...]).astype(BF16)
    g = jnp.dot(xn, wi_ref[:, 0:dff], preferred_element_type=F32)
    u = jnp.dot(xn, wi_ref[:, dff:2 * dff], preferred_element_type=F32)
    a = (g * _sigmoid(g) * u).astype(BF16)
    h = jnp.dot(a, wo_ref[...], preferred_element_type=F32)
    o_ref[...] = x + 0.5 * _rms(h, nwo_ref[...])


def _ffn(x, nw, layer, row_in, row_out, wi, wo, *, tm):
    m, d = x.shape
    dff = wo.shape[1]
    resident = pl.Buffered(1)
    return pl.pallas_call(
        _ffn_kernel,
        grid=(m // tm,),
        in_specs=[
            pl.BlockSpec((tm, d), lambda i: (i, 0)),
            pl.BlockSpec((None, None, 1, d), lambda i: (layer, row_in, 0, 0)),
            pl.BlockSpec((None, d, 2 * dff), lambda i: (layer, 0, 0), pipeline_mode=resident),
            pl.BlockSpec((None, dff, d), lambda i: (layer, 0, 0), pipeline_mode=resident),
            pl.BlockSpec((None, None, 1, d), lambda i: (layer, row_out, 0, 0)),
        ],
        out_specs=pl.BlockSpec((tm, d), lambda i: (i, 0)),
        out_shape=jax.ShapeDtypeStruct((m, d), F32),
        compiler_params=_cparams(("parallel",)),
        name="ffn",
    )(x, nw, wi, wo, nw)


def _proj_kernel(x_ref, nw_ref, w_ref, wt_ref, cw_ref, cb_ref, gbc_ref, gbr_ref, lb_ref,
                 qm_ref, km_ref, vmt_ref, om_ref, gc_ref, gr_ref, qd_ref, kd_ref, vdt_ref,
                 qh_ref, lfh_ref, kh_ref, vh_ref, gh_ref, cbuf_ref):
    t = pl.program_id(1)
    tm = x_ref.shape[0]
    xn = _rms(x_ref[...], nw_ref[...]).astype(BF16)

    def u(lo, hi):
        return jnp.dot(xn, w_ref[:, lo:hi], preferred_element_type=F32)

    @pl.when(t == 0)
    def _():
        cbuf_ref[0:8, :] = jnp.zeros((8, 2 * M_W), F32)

    qk = u(0, 2 * M_W)
    cbuf_ref[8:8 + tm, :] = qk
    cw = cw_ref[...]
    y = cb_ref[...] + cw[CONV_W - 1:CONV_W] * qk
    for d in range(1, CONV_W):
        y = y + cw[CONV_W - 1 - d:CONV_W - d] * cbuf_ref[8 - d:8 - d + tm, :]
    cbuf_ref[0:8, :] = cbuf_ref[tm:tm + 8, :]
    y = y * _sigmoid(y)
    qm_ref[...] = y[:, 0:M_W].astype(qm_ref.dtype)
    km_ref[...] = (y[:, M_W:2 * M_W] * (HEAD_W ** -0.5)).astype(km_ref.dtype)
    ut = lax.dot_general(wt_ref[...], xn, (((1,), (1,)), ((), ())), preferred_element_type=F32)
    vdt_ref[...] = ut[0:DF_W].astype(vdt_ref.dtype)
    vmt_ref[...] = ut[DF_W:DF_W + M_W].astype(vmt_ref.dtype)
    om_ref[...] = u(768, 1024).astype(om_ref.dtype)

    qd_ref[...] = (u(1024, 1536) * (HEAD_W ** -0.5 * LOG2E)).astype(qd_ref.dtype)
    kd_ref[...] = u(1536, 2048).astype(kd_ref.dtype)

    qh = u(2048, 2304)
    qh_ref[...] = (qh * _sigmoid(qh)).astype(qh_ref.dtype)
    fp = u(2304, 2560)
    lb = lb_ref[...]
    a = jnp.log(lb)
    bb = jnp.log1p(-lb) + _log_sigmoid(fp)
    lfh_ref[...] = jnp.maximum(a, bb) + jnp.log1p(jnp.exp(-jnp.abs(a - bb)))
    kh_ref[...] = ((1.0 - lb) * _sigmoid(-fp)).astype(kh_ref.dtype)
    vh_ref[...] = u(2560, 2816).astype(vh_ref.dtype)
    gh_ref[...] = u(2816, 3072).astype(gh_ref.dtype)

    zc = u(3072, 3072 + GATE_PAD) + gbc_ref[...]
    lane = lax.broadcasted_iota(jnp.int32, (1, GATE_PAD), 1)
    gc_ref[...] = jnp.where(lane < HEADS, zc, _log_sigmoid(zc))
    zr = ut[DF_W + M_W:DF_W + M_W + 2 * HEADS] + gbr_ref[...]
    row = lax.broadcasted_iota(jnp.int32, (2 * HEADS, 1), 0)
    gr_ref[...] = jnp.where(row < HEADS, zr, _log_sigmoid(zr))


def _proj(x, nw, layer, w, wt, cw, cb, gbc, gbr, lb, *, batch, tm, act_dtype):
    m, d = x.shape
    seq = m // batch
    nt = seq // tm
    npc = w.shape[-1]
    tok = lambda width: pl.BlockSpec((tm, width), lambda b, t: (b * nt + t, 0))
    lay = lambda *shape: pl.BlockSpec((None,) + shape, lambda b, t: (layer,) + (0,) * len(shape),
                                      pipeline_mode=pl.Buffered(1))
    sds = lambda width, dt: jax.ShapeDtypeStruct((m, width), dt)
    out_shape = [sds(M_W, act_dtype), sds(M_W, act_dtype),
                 jax.ShapeDtypeStruct((batch, nt, M_W, tm), act_dtype), sds(M_W, act_dtype),
                 sds(GATE_PAD, F32), jax.ShapeDtypeStruct((batch, 2 * HEADS, seq), F32),
                 sds(DF_W, act_dtype), sds(DF_W, act_dtype),
                 jax.ShapeDtypeStruct((batch, nt, DF_W, tm), act_dtype),
                 sds(M_W, act_dtype), sds(M_W, F32), sds(M_W, act_dtype), sds(M_W, act_dtype), sds(M_W, act_dtype)]
    out_specs = [tok(M_W), tok(M_W), pl.BlockSpec((None, None, M_W, tm), lambda b, t: (b, t, 0, 0)), tok(M_W),
                 tok(GATE_PAD),
                 pl.BlockSpec((None, 2 * HEADS, tm), lambda b, t: (b, 0, t)),
                 tok(DF_W), tok(DF_W),
                 pl.BlockSpec((None, None, DF_W, tm), lambda b, t: (b, t, 0, 0)),
                 tok(M_W), tok(M_W), tok(M_W), tok(M_W), tok(M_W)]
    return pl.pallas_call(
        _proj_kernel,
        grid=(batch, nt),
        in_specs=[tok(d),
                  pl.BlockSpec((None, None, 1, d), lambda b, t: (layer, 2, 0, 0)),
                  lay(d, npc), lay(wt.shape[1], d), lay(CONV_W, 2 * M_W), lay(1, 2 * M_W),
                  lay(1, GATE_PAD), lay(2 * HEADS, 1), lay(1, M_W)],
        out_specs=out_specs,
        out_shape=out_shape,
        scratch_shapes=[pltpu.VMEM((tm + 8, 2 * M_W), F32)],
        compiler_params=_cparams(("parallel", "arbitrary")),
        name="mixer_proj",
    )(x, nw, w, wt, cw, cb, gbc, gbr, lb)


def _mlstm_kernel(q_ref, k_ref, v_ref, gc_ref, gr_ref, h_ref, c_ref, n_ref, m_ref):
    @pl.when(pl.program_id(1) == 0)
    def _():
        c_ref[...] = jnp.zeros_like(c_ref)
        n_ref[...] = jnp.zeros_like(n_ref)
        m_ref[...] = jnp.zeros_like(m_ref)

    for sq in range(q_ref.shape[0]):
        _mlstm_chunk(*(r.at[sq] for r in (q_ref, k_ref, v_ref, gc_ref, gr_ref, h_ref, c_ref, n_ref, m_ref)))


def _mlstm_chunk(q_ref, k_ref, vt_ref, gc_ref, gr_ref, h_ref, ct_ref, n_ref, m_ref):
    L = q_ref.shape[0]
    qb = q_ref[...]
    kb = k_ref[...]
    vt = vt_ref[...]
    gc = gc_ref[...]
    gr = gr_ref[...]
    row = lax.broadcasted_iota(jnp.int32, (L, L), 0)
    col = lax.broadcasted_iota(jnp.int32, (L, L), 1)
    allowed = row <= col
    bcol = _dot01_lhs((row >= col).astype(BF16), gc, 3)
    brow = _dot01_rhs(gr, allowed.astype(BF16), 3)
    lane_head = _head_of(lax.broadcasted_iota(jnp.int32, (1, M_W), 1))
    r8 = lax.broadcasted_iota(jnp.int32, (2 * HEADS, M_W), 0)
    own8 = _head_of(lax.broadcasted_iota(jnp.int32, (2 * HEADS, M_W), 1)) == r8
    r8l = lax.broadcasted_iota(jnp.int32, (2 * HEADS, L), 0)
    ct = ct_ref[...]
    nrow = n_ref[...]
    nt_dims = (((1,), (1,)), ((), ()))
    qct = lax.dot_general(ct.astype(BF16), qb, nt_dims, preferred_element_type=F32)
    qn8 = None
    for part in _split_bf16(jnp.where(own8, nrow, 0.0), 3):
        t = lax.dot_general(part, qb, nt_dims, preferred_element_type=F32)
        qn8 = t if qn8 is None else qn8 + t
    hts, vws = [], []
    wa8 = jnp.zeros((2 * HEADS, L), F32)
    dec_all = jnp.zeros((1, M_W), F32)
    for h in range(HEADS):
        hm = lane_head == h
        rows = slice(h * HEAD_W, (h + 1) * HEAD_W)
        b_row = brow[HEADS + h:HEADS + h + 1, :]
        keycol = gc[:, h:h + 1] - bcol[:, HEADS + h:HEADS + h + 1]
        mprev = m_ref[h:h + 1, 0:1]
        dmat = jnp.where(allowed, b_row + keycol, -jnp.inf)
        m_inter = b_row + mprev
        mt = jnp.maximum(jnp.max(dmat, axis=0, keepdims=True), m_inter)
        w = jnp.exp(dmat - mt)
        qh = jnp.where(hm, qb, jnp.zeros_like(qb))
        sc = lax.dot_general(kb, qh, nt_dims, preferred_element_type=F32) * w
        g = jnp.exp(m_inter - mt)
        pv = jnp.dot(vt[rows], sc.astype(BF16), preferred_element_type=F32)
        den = jnp.sum(sc, axis=0, keepdims=True) + g * qn8[h:h + 1, :]
        scale = 1.0 / jnp.maximum(jnp.abs(den), jnp.exp(-mt))
        hh = (pv + g * qct[rows]) * scale
        hc = hh - jnp.mean(hh, axis=0, keepdims=True)
        hts.append(hc * lax.rsqrt(jnp.mean(hc * hc, axis=0, keepdims=True) + EPS))
        bl = b_row[:, L - 1:L]
        a = bl - b_row + gr[h:h + 1, :]
        mnew = jnp.maximum(bl + mprev, jnp.max(a, axis=1, keepdims=True))
        wa = jnp.exp(a - mnew)
        vws.append(vt[rows].astype(F32) * wa)
        wa8 = jnp.where(r8l == h, wa, wa8)
        dec_all = jnp.where(hm, jnp.exp(bl + mprev - mnew), dec_all)
        m_ref[h:h + 1, :] = jnp.broadcast_to(mnew, (1, m_ref.shape[1]))
    h_ref[...] = jnp.concatenate(hts, axis=0).T.astype(h_ref.dtype)
    cnew = jnp.dot(jnp.concatenate(vws, axis=0).astype(BF16), kb, preferred_element_type=F32)
    r2 = _head_of(lax.broadcasted_iota(jnp.int32, (M_W, M_W), 0))
    c2 = _head_of(lax.broadcasted_iota(jnp.int32, (M_W, M_W), 1))
    ct_ref[...] = dec_all * ct + jnp.where(r2 == c2, cnew, 0.0)
    kn8 = _dot_split_lhs(wa8, kb)
    n_ref[...] = dec_all * nrow + jnp.sum(jnp.where(own8, kn8, 0.0), axis=0, keepdims=True)


def _dot_split_lhs(x, y_bf16):
    acc = None
    for p in _split_bf16(x, 3):
        t = jnp.dot(p, y_bf16, preferred_element_type=F32)
        acc = t if acc is None else acc + t
    return acc


def _mlstm(qm, km, vmt, gc, gr, *, batch, chunk, nb):
    m = qm.shape[0]
    seq = m // batch
    per = vmt.shape[3] // chunk
    view = lambda a: a.reshape(batch, seq, a.shape[-1])
    tok = lambda width: pl.BlockSpec((nb, chunk, width), lambda b, c: (b, c, 0))
    out = pl.pallas_call(
        _mlstm_kernel,
        grid=(batch // nb, seq // chunk),
        in_specs=[tok(M_W), tok(M_W),
                  pl.BlockSpec((nb, None, M_W, chunk), lambda b, c: (b, c // per, 0, c % per)),
                  tok(GATE_PAD),
                  pl.BlockSpec((nb, 2 * HEADS, chunk), lambda b, c: (b, 0, c))],
        out_specs=tok(M_W),
        out_shape=jax.ShapeDtypeStruct((batch, seq, M_W), BF16),
        scratch_shapes=[pltpu.VMEM((nb, M_W, M_W), F32), pltpu.VMEM((nb, 1, M_W), F32),
                        pltpu.VMEM((nb, 8, 128), F32)],
        compiler_params=_cparams(("parallel", "arbitrary")),
        name="mlstm",
    )(view(qm), view(km), vmt, view(gc), gr)
    return out.reshape(m, M_W)


HG_BLK = 16


def _hgrn_kernel(q_ref, k_ref, v_ref, lf_ref, o_ref, st_ref, p_ref):
    @pl.when(pl.program_id(1) == 0)
    def _():
        st_ref[...] = jnp.zeros_like(st_ref)

    nb = q_ref.shape[0]
    nblk = q_ref.shape[1] // HG_BLK
    rowid =lax.broadcasted_iota(jnp.int32, (HG_BLK, M_W), 0)
    rowid8 = lax.broadcasted_iota(jnp.int32, (HG_BLK // 2, M_W), 0)
    r2 = _head_of(lax.broadcasted_iota(jnp.int32, (M_W, M_W), 0))
    c2 = _head_of(lax.broadcasted_iota(jnp.int32, (M_W, M_W), 1))
    same_head = r2 == c2
    ones_bd = same_head.astype(BF16)

    def block(i, sq):
        r0 = pl.multiple_of(i * HG_BLK, HG_BLK)
        q = q_ref[sq, pl.ds(r0, HG_BLK), :].astype(F32)
        k = k_ref[sq, pl.ds(r0, HG_BLK), :].astype(F32)
        v = v_ref[sq, pl.ds(r0, HG_BLK), :].astype(F32)
        lf = lf_ref[sq, pl.ds(r0, HG_BLK), :]
        b = lf
        for d in (1, 2, 4, 8):
            b = b + jnp.where(rowid >= d, pltpu.roll(b, d, axis=0), 0.0)
        bl = b[HG_BLK - 1:HG_BLK, :]
        st = st_ref[sq]
        o = lax.dot_general((q * jnp.exp(b)).astype(BF16), st.astype(BF16), (((1,), (1,)), ((), ())),
                            preferred_element_type=F32)
        hb = HG_BLK // 2
        pbuf = p_ref.at[sq]
        for half in range(2):
            qq, bq = q[half * hb:(half + 1) * hb], b[half * hb:(half + 1) * hb]
            for s in range(half * hb, (half + 1) * hb):
                dd = jnp.where(rowid8 >= s - half * hb, bq - b[s:s + 1, :], -jnp.inf)
                pbuf[s * hb:(s + 1) * hb, :] = qq * k[s:s + 1, :] * jnp.exp(dd)
        bmid = b[hb - 1:hb, :]
        qe1 = q[hb:] * jnp.exp(b[hb:] - bmid)
        ke0 = k[0:hb] * jnp.exp(bmid - b[0:hb])
        for s in range(hb):
            pbuf[(HG_BLK + s) * hb:(HG_BLK + s + 1) * hb, :] = qe1 * ke0[s:s + 1, :]
        abig = jnp.dot(pbuf[...].astype(BF16), ones_bd, preferred_element_type=F32)
        grp = lambda g: abig[g * hb:(g + 1) * hb, :]
        o0 = grp(0) * v[0:1, :]
        o1 = grp(hb) * v[hb:hb + 1, :] + grp(HG_BLK) * v[0:1, :]
        for s in range(1, hb):
            o0 = o0 + grp(s) * v[s:s + 1, :]
            o1 = o1 + grp(hb + s) * v[hb + s:hb + s + 1, :] + grp(HG_BLK + s) * v[s:s + 1, :]
        o_ref[sq, pl.ds(r0, HG_BLK), :] = (o + jnp.concatenate([o0, o1], axis=0)).astype(o_ref.dtype)
        ke = (k * jnp.exp(bl - b)).astype(BF16)
        upd = lax.dot_general(v.astype(BF16), ke, (((0,), (0,)), ((), ())), preferred_element_type=F32)
        st_ref[sq] = st * jnp.exp(bl) + jnp.where(same_head, upd, 0.0)

    def body(i, carry):
        for sq in range(nb):
            block(i, sq)
        return carry

    lax.fori_loop(0, nblk, body, 0)


def _hgrn(qh, kh, vh, lfh, *, batch, tile, nb):
    m = qh.shape[0]
    seq = m // batch
    view = lambda a: a.reshape(batch, seq, M_W)
    tok = pl.BlockSpec((nb, tile, M_W), lambda b, t: (b, t, 0))
    out = pl.pallas_call(
        _hgrn_kernel,
        grid=(batch // nb, seq // tile),
        in_specs=[tok, tok, tok, tok],
        out_specs=tok,
        out_shape=jax.ShapeDtypeStruct((batch, seq, M_W), BF16),
        scratch_shapes=[pltpu.VMEM((nb, M_W, M_W), F32), pltpu.VMEM((nb, 3 * HG_BLK * HG_BLK // 4, M_W), F32)],
        compiler_params=_cparams(("parallel", "arbitrary")),
        name="hgrn2",
    )(view(qh), view(kh), view(vh), view(lfh))
    return out.reshape(m, M_W)


ONES_ROWS = 16


def _attn_kernel(q_ref, k_ref, vt_ref, bias_ref, lam0_ref, lv_ref, nwt_ref, o_ref, m_ref, acc_ref, sa_ref, sb_ref,
                 mca_ref, mcb_ref, *, tile):
    qi = pl.program_id(2)
    vchunk = vt_ref.shape[2]
    nvc = tile // vchunk
    m_ref[...] = jnp.full_like(m_ref, NEG)
    acc_ref[...] = jnp.zeros_like(acc_ref)
    q = q_ref[...]
    first = lax.broadcasted_iota(jnp.int32, (1, DF_HW), 1) < HEAD_W
    zero = jnp.zeros_like(q)
    qs = (jnp.where(first, q, zero), jnp.where(first, zero, q))
    ones = jnp.ones((ONES_ROWS, vchunk), BF16)

    def scores(ki, s_ref, mc_ref):
        kb = k_ref[pl.ds(pl.multiple_of(ki * tile, tile), tile), :]
        bias = bias_ref[jnp.minimum(qi - ki, 2)]
        for j in range(2):
            st = bias + lax.dot_general(kb, qs[j], (((1,), (1,)), ((), ())), preferred_element_type=F32)
            s_ref[j] = st.astype(BF16)
            mc_ref[j] = jnp.max(st, axis=0, keepdims=True).astype(BF16).astype(F32)

    def softmax_pv(ki, s_ref, mc_ref):
        vts = [jnp.concatenate([vt_ref[ki * nvc + c], ones], axis=0) for c in range(nvc)]
        for j in range(2):
            m_old = m_ref[j]
            m_new = jnp.maximum(m_old, mc_ref[j])
            alpha = jnp.exp2(m_old - m_new)
            pt = jnp.exp2(s_ref[j] - m_new.astype(BF16))
            pv = None
            for c in range(nvc):
                t = jnp.dot(vts[c], pt[c * vchunk:(c + 1) * vchunk, :], preferred_element_type=F32)
                pv = t if pv is None else pv + t
            acc_ref[j] = alpha * acc_ref[j] + pv
            m_ref[j] = m_new

    ntiles = qi + 1
    scores(0, sa_ref, mca_ref)

    def pair(p, carry):
        ka = 2 * p
        scores(ka + 1, sb_ref, mcb_ref)
        softmax_pv(ka, sa_ref, mca_ref)
        scores(ka + 2, sa_ref, mca_ref)
        softmax_pv(ka + 1, sb_ref, mcb_ref)
        return carry

    lax.fori_loop(0, (ntiles - 1) // 2, pair, 0)

    @pl.when(ntiles % 2 == 0)
    def _():
        scores(qi, sb_ref, mcb_ref)
        softmax_pv(qi - 1, sa_ref, mca_ref)
        softmax_pv(qi, sb_ref, mcb_ref)

    @pl.when(ntiles % 2 == 1)
    def _():
        softmax_pv(qi, sa_ref, mca_ref)

    lv = lv_ref[...]
    lam0 = lam0_ref[0]
    lam = (jnp.exp(jnp.sum(lv[0:1] * lv[1:2], axis=1, keepdims=True))
           - jnp.exp(jnp.sum(lv[2:3] * lv[3:4], axis=1, keepdims=True)) + lam0)
    a1 = acc_ref[0]
    a2 = acc_ref[1]
    od = a1[0:DF_HW] / a1[DF_HW:DF_HW + 1] - lam * (a2[0:DF_HW] / a2[DF_HW:DF_HW + 1])
    ms = jnp.mean(od * od, axis=0, keepdims=True)
    yt = od * lax.rsqrt(ms + EPS) * nwt_ref[...] * (1.0 - lam0)
    o_ref[...] = yt.T.astype(o_ref.dtype)


def _diff_attn(qd, kd, vdt, bias_tiles, lam0, lam_vecs, norm_w_t, layer, *, batch, tile):
    m = qd.shape[0]
    seq = m // batch
    nq = seq // tile
    nchunk, vchunk = vdt.shape[1], vdt.shape[3]
    return pl.pallas_call(
        functools.partial(_attn_kernel, tile=tile),
        grid=(HEADS, batch, nq),
        in_specs=[
            pl.BlockSpec((tile, DF_HW), lambda h, b, i: (b * nq + i, h)),
            pl.BlockSpec((seq, DF_HW), lambda h, b, i: (b, h)),
            pl.BlockSpec((None, nchunk, DF_HW, vchunk), lambda h, b, i: (b, 0, h, 0)),
            pl.BlockSpec((None, 3, tile, tile), lambda h, b, i: (h, 0, 0, 0)),
            pl.BlockSpec(memory_space=pltpu.SMEM),
            pl.BlockSpec((None, 4, HEAD_W), lambda h, b, i: (layer, 0, 0)),
            pl.BlockSpec((None, DF_HW, 1), lambda h, b, i: (layer, h, 0)),
        ],
        out_specs=pl.BlockSpec((tile, DF_HW), lambda h, b, i: (b * nq + i, h)),
        out_shape=jax.ShapeDtypeStruct((m, DF_W), BF16),
        scratch_shapes=[pltpu.VMEM((2, 1, tile), F32), pltpu.VMEM((2, DF_HW + ONES_ROWS, tile), F32),
                        pltpu.VMEM((2, tile, tile), BF16), pltpu.VMEM((2, tile, tile), BF16),
                        pltpu.VMEM((2, 1, tile), F32), pltpu.VMEM((2, 1, tile), F32)],
        compiler_params=_cparams(("parallel", "parallel", "arbitrary")),
        name="diff_attn",
    )(qd, kd, vdt, bias_tiles, lam0, lam_vecs, norm_w_t)


def _mix_out_kernel(x_ref, hm_ref, om_ref, yd_ref, oh_ref, gh_ref, mnw_ref, hnw_ref, w_ref, nw_ref, o_ref):
    r2 = _head_of(lax.broadcasted_iota(jnp.int32, (M_W, M_W), 0))
    c2 = _head_of(lax.broadcasted_iota(jnp.int32, (M_W, M_W), 1))
    ones_bd = (r2 == c2).astype(BF16)
    inv = 1.0 / HEAD_W

    ym = hm_ref[...].astype(F32) * mnw_ref[...] * _sigmoid(om_ref[...].astype(F32))

    oh = oh_ref[...].astype(F32)
    ms = _dot01_rhs(oh * oh, ones_bd, 2) * inv
    gh = gh_ref[...].astype(F32)
    yh = oh * lax.rsqrt(ms + EPS) * hnw_ref[...] * (gh * _sigmoid(gh))

    w = w_ref[...]
    acc = jnp.dot(ym.astype(BF16), w[0:M_W], preferred_element_type=F32)
    acc = acc + jnp.dot(yd_ref[...].astype(BF16), w[M_W:M_W + DF_W], preferred_element_type=F32)
    acc = acc + jnp.dot(yh.astype(BF16), w[M_W + DF_W:], preferred_element_type=F32)
    o_ref[...] = x_ref[...] + _rms(acc, nw_ref[...])


def _mix_out(x, hm, om, yd, oh, gh, mnw, hnw, w_out, nw, layer, *, tm):
    m, d = x.shape
    tok = lambda width: pl.BlockSpec((tm, width), lambda i: (i, 0))
    lay = lambda *shape: pl.BlockSpec((None,) + shape, lambda i: (layer,) + (0,) * len(shape))
    return pl.pallas_call(
        _mix_out_kernel,
        grid=(m // tm,),
        in_specs=[tok(d), tok(M_W), tok(M_W), tok(DF_W), tok(M_W), tok(M_W),
                  lay(1, M_W), lay(1, M_W), lay(d, d),
                  pl.BlockSpec((None, None, 1, d), lambda i: (layer, 3, 0, 0))],
        out_specs=tok(d),
        out_shape=jax.ShapeDtypeStruct((m, d), F32),
        compiler_params=_cparams(("parallel",)),
        name="mix_out",
    )(x, hm, om, yd, oh, gh, mnw, hnw, w_out, nw)


def _tiles(batch, seq):
    return dict(
        ffn_tm=min(512, seq), proj_tm=min(512, seq), mlstm_chunk=min(256, seq), hgrn_tile=min(256, seq),
        attn_tile=min(512, seq), out_tm=min(512, seq), hgrn_nb=8 if batch % 8 == 0 else 1,
        mlstm_nb=2 if batch % 2 == 0 else 1)


def kernel(x, norm_w, ffn1_wi, ffn1_wo, ffn2_wi, ffn2_wo, w_in, w_out, mlstm_conv_w, mlstm_conv_b, mlstm_igate_b,
           mlstm_fgate_b, mlstm_norm_w, diff_lambda, diff_norm_w, rel_bias, hgrn_lb_logits, hgrn_norm_w):
    batch, seq, d = x.shape
    depth = norm_w.shape[0]
    tl = _tiles(batch, seq)
    act_dtype = BF16

    off = [0] + [int(v) for v in np.cumsum(SPLIT_SIZES)]
    seg = lambda i: w_in[:, :, off[i]:off[i + 1]].astype(BF16)
    gate_pad = jnp.zeros(w_in.shape[:2] + (GATE_PAD - 2 * HEADS,), BF16)
    w_proj = jnp.concatenate([seg(0), seg(1), seg(2), seg(5), seg(6), seg(8), seg(9), seg(10), seg(11),
                              seg(3), seg(4), gate_pad], axis=2)
    w_t = jnp.swapaxes(jnp.concatenate([seg(7), seg(1), seg(3), seg(4),
                                        jnp.zeros(w_in.shape[:2] + (2 * HEADS,), BF16)], axis=2), 1, 2)
    gate_b = jnp.concatenate([mlstm_igate_b, mlstm_fgate_b], axis=1).astype(F32)
    gate_b_col = jnp.pad(gate_b, ((0, 0), (0, GATE_PAD - 2 * HEADS)))[:, None, :]
    gate_b_row = gate_b[:, :, None]
    wi1, wo1 = ffn1_wi.astype(BF16), ffn1_wo.astype(BF16)
    wi2, wo2 = ffn2_wi.astype(BF16), ffn2_wo.astype(BF16)
    w_o = w_out.astype(BF16)
    nw = norm_w.astype(F32)[:, :, None, :]
    conv_w = mlstm_conv_w.astype(F32)
    conv_b = mlstm_conv_b.astype(F32)[:, None, :]
    mnw = mlstm_norm_w.astype(F32)[:, None, :]
    hnw = hgrn_norm_w.astype(F32)[:, None, :]
    dnw_t = diff_norm_w.astype(F32)[:, :, None]
    lam_vecs = diff_lambda.astype(F32)

    lb_all = _hgrn_lower_bounds(hgrn_lb_logits)[:, None, :]
    bias_tiles = _rel_bias_tiles(rel_bias, tl["attn_tile"])

    xf = x.reshape(batch * seq, d)
    for l in range(depth):
        lam0 = jnp.full((1,), 0.8 - 0.6 * math.exp(-0.3 * l), F32)
        xf = _ffn(xf, nw, l, 0, 1, wi1, wo1, tm=tl["ffn_tm"])
        (qm, km, vmt, om, gc, gr, qd, kd, vdt, qh, lfh, kh, vh, gh) = _proj(
            xf, nw, l, w_proj, w_t, conv_w, conv_b, gate_b_col, gate_b_row, lb_all,
            batch=batch, tm=tl["proj_tm"], act_dtype=act_dtype)
        hm = _mlstm(qm, km, vmt, gc, gr, batch=batch, chunk=tl["mlstm_chunk"], nb=tl["mlstm_nb"])
        yd = _diff_attn(qd, kd, vdt, bias_tiles, lam0, lam_vecs, dnw_t, l, batch=batch, tile=tl["attn_tile"])
        oh = _hgrn(qh, kh, vh, lfh, batch=batch, tile=tl["hgrn_tile"], nb=tl["hgrn_nb"])
        xf = _mix_out(xf, hm, om, yd, oh, gh, mnw, hnw, w_o, nw, l, tm=tl["out_tm"])
        xf = _ffn(xf, nw, l, 4, 5, wi2, wo2, tm=tl["ffn_tm"])
    return xf.reshape(batch, seq, d)
```

```python
import functools
import math

import numpy as np
import jax
import jax.numpy as jnp
from jax import lax
from jax.experimental import pallas as pl
from jax.experimental.pallas import tpu as pltpu

F32 = jnp.float32
BF16 = jnp.bfloat16
EPS = 1e-6
NEG = -1e30
LOG2E = 1.4426950408889634

HEADS = 4
HEAD_W = 64
M_W = HEADS * HEAD_W
DF_HW = 128
DF_W = HEADS * DF_HW
CONV_W = 4
REL_BUCKETS = 32
REL_MAX_EXACT = 16
REL_MAX_DIST = 128
GATE_PAD = 128
SPLIT_SIZES = (2 * M_W, M_W, M_W, HEADS, HEADS, DF_W, DF_W, DF_W, M_W, M_W, M_W, M_W)
VMEM_LIMIT = 56 * 1024 * 1024


def _cparams(sem):
    return pltpu.CompilerParams(dimension_semantics=sem, vmem_limit_bytes=VMEM_LIMIT)


def _sigmoid(x):
    return 1.0 / (1.0 + jnp.exp(-x))


def _log_sigmoid(x):
    return jnp.minimum(x, 0.0) - jnp.log1p(jnp.exp(-jnp.abs(x)))


def _rms(x, w):
    return x * lax.rsqrt(jnp.mean(x * x, axis=-1, keepdims=True) + EPS) * w


def _split_bf16(x, parts):
    out = []
    r = x
    for _ in range(parts):
        p = r.astype(BF16)
        out.append(p)
        r = r - p.astype(F32)
    return out


def _dot01_rhs(x, m01, parts):
    acc = None
    for p in _split_bf16(x, parts):
        t = jnp.dot(p, m01, preferred_element_type=F32)
        acc = t if acc is None else acc + t
    return acc


def _dot01_lhs(m01, x, parts):
    acc = None
    for p in _split_bf16(x, parts):
        t = jnp.dot(m01, p, preferred_element_type=F32)
        acc = t if acc is None else acc + t
    return acc


def _head_of(idx):
    return lax.shift_right_logical(idx, HEAD_W.bit_length() - 1)


def _lb_kernel(lg_ref, o_ref):
    lg = lg_ref[...]
    e = jnp.exp(lg - jnp.max(lg, axis=0, keepdims=True))
    sm = e / jnp.sum(e, axis=0, keepdims=True)
    depth = lg.shape[0]
    rows = []
    run = sm[0:1]
    first = run
    for i in range(depth):
        if i > 0:
            run = run + sm[i:i + 1]
        rows.append(jnp.maximum(run - first, 0.0))
    o_ref[...] = jnp.concatenate(rows, axis=0)


def _hgrn_lower_bounds(logits):
    return pl.pallas_call(
        _lb_kernel, out_shape=jax.ShapeDtypeStruct(logits.shape, F32), name="hgrn_lb")(logits.astype(F32))


def _bias_kernel(tab_ref, o_ref, *, tile):
    h = pl.program_id(0)
    typ = pl.program_id(1)
    r = lax.broadcasted_iota(jnp.int32, (tile, tile), 0)
    c = lax.broadcasted_iota(jnp.int32, (tile, tile), 1)
    rel = c - r + typ * tile
    n = jnp.maximum(rel, 0)
    nf = jnp.maximum(n, 1).astype(F32)
    large = REL_MAX_EXACT + (jnp.log(nf / REL_MAX_EXACT) / math.log(REL_MAX_DIST / REL_MAX_EXACT)
                             * (REL_BUCKETS - REL_MAX_EXACT)).astype(jnp.int32)
    large = jnp.minimum(large, REL_BUCKETS - 1)
    bucket = jnp.where(n < REL_MAX_EXACT, n, large)
    bias = jnp.zeros((tile, tile), F32)
    for b in range(REL_BUCKETS):
        bias = jnp.where(bucket == b, tab_ref[b, h], bias)
    o_ref[...] = jnp.where(rel >= 0, bias * LOG2E, NEG)


def _rel_bias_tiles(rel_bias, tile):
    assert tile >= REL_MAX_DIST
    return pl.pallas_call(
        functools.partial(_bias_kernel, tile=tile),
        grid=(HEADS, 3),
        in_specs=[pl.BlockSpec(memory_space=pltpu.SMEM)],
        out_specs=pl.BlockSpec((None, None, tile, tile), lambda h, t: (h, t, 0, 0)),
        out_shape=jax.ShapeDtypeStruct((HEADS, 3, tile, tile), F32),
        name="rel_bias_tiles",
    )(rel_bias.astype(F32))


def _ffn_kernel(x_ref, nwi_ref, wi_ref, wo_ref, nwo_ref, o_ref):
    dff = wo_ref.shape[0]
    x = x_ref[...]
    xn = _rms(x, nwi_ref[...]).astype(BF16)
    g = jnp.dot(xn, wi_ref[:, 0:dff], preferred_element_type=F32)
    u = jnp.dot(xn, wi_ref[:, dff:2 * dff], preferred_element_type=F32)
    a = (g * _sigmoid(g) * u).astype(BF16)
    h = jnp.dot(a, wo_ref[...], preferred_element_type=F32)
    o_ref[...] = x + 0.5 * _rms(h, nwo_ref[...])


def _ffn(x, nw, layer, row_in, row_out, wi, wo, *, tm):
    m, d = x.shape
    dff = wo.shape[1]
    resident = pl.Buffered(1)
    return pl.pallas_call(
        _ffn_kernel,
        grid=(m // tm,),
        in_specs=[
            pl.BlockSpec((tm, d), lambda i: (i, 0)),
            pl.BlockSpec((None, None, 1, d), lambda i: (layer, row_in, 0, 0)),
            pl.BlockSpec((None, d, 2 * dff), lambda i: (layer, 0, 0), pipeline_mode=resident),
            pl.BlockSpec((None, dff, d), lambda i: (layer, 0, 0), pipeline_mode=resident),
            pl.BlockSpec((None, None, 1, d), lambda i: (layer, row_out, 0, 0)),
        ],
        out_specs=pl.BlockSpec((tm, d), lambda i: (i, 0)),
        out_shape=jax.ShapeDtypeStruct((m, d), F32),
        compiler_params=_cparams(("parallel",)),
        name="ffn",
    )(x, nw, wi, wo, nw)


def _proj_kernel(x_ref, nw_ref, w_ref, wt_ref, cw_ref, cb_ref, gbc_ref, gbr_ref, lb_ref,
                 qm_ref, km_ref, vmt_ref, om_ref, gc_ref, gr_ref, qd_ref, kd_ref, vdt_ref,
                 qh_ref, lfh_ref, kh_ref, vh_ref, gh_ref, cbuf_ref):
    t = pl.program_id(1)
    tm = x_ref.shape[0]
    xn = _rms(x_ref[...], nw_ref[...]).astype(BF16)

    def u(lo, hi):
        return jnp.dot(xn, w_ref[:, lo:hi], preferred_element_type=F32)

    @pl.when(t == 0)
    def _():
        cbuf_ref[0:8, :] = jnp.zeros((8, 2 * M_W), F32)

    qk = u(0, 2 * M_W)
    cbuf_ref[8:8 + tm, :] = qk
    cw = cw_ref[...]
    y = cb_ref[...] + cw[CONV_W - 1:CONV_W] * qk
    for d in range(1, CONV_W):
        y = y + cw[CONV_W - 1 - d:CONV_W - d] * cbuf_ref[8 - d:8 - d + tm, :]
    cbuf_ref[0:8, :] = cbuf_ref[tm:tm + 8, :]
    y = y * _sigmoid(y)
    qm_ref[...] = y[:, 0:M_W].astype(qm_ref.dtype)
    km_ref[...] = (y[:, M_W:2 * M_W] * (HEAD_W ** -0.5)).astype(km_ref.dtype)
    ut = lax.dot_general(wt_ref[...], xn, (((1,), (1,)), ((), ())), preferred_element_type=F32)
    vdt_ref[...] = ut[0:DF_W].astype(vdt_ref.dtype)
    vmt_ref[...] = ut[DF_W:DF_W + M_W].astype(vmt_ref.dtype)
    om_ref[...] = u(768, 1024).astype(om_ref.dtype)

    qd_ref[...] = (u(1024, 1536) * (HEAD_W ** -0.5 * LOG2E)).astype(qd_ref.dtype)
    kd_ref[...] = u(1536, 2048).astype(kd_ref.dtype)

    qh = u(2048, 2304)
    qh_ref[...] = (qh * _sigmoid(qh)).astype(qh_ref.dtype)
    fp = u(2304, 2560)
    lb = lb_ref[...]
    a = jnp.log(lb)
    bb = jnp.log1p(-lb) + _log_sigmoid(fp)
    lfh_ref[...] = jnp.maximum(a, bb) + jnp.log1p(jnp.exp(-jnp.abs(a - bb)))
    kh_ref[...] = ((1.0 - lb) * _sigmoid(-fp)).astype(kh_ref.dtype)
    vh_ref[...] = u(2560, 2816).astype(vh_ref.dtype)
    gh_ref[...] = u(2816, 3072).astype(gh_ref.dtype)

    zc = u(3072, 3072 + GATE_PAD) + gbc_ref[...]
    lane = lax.broadcasted_iota(jnp.int32, (1, GATE_PAD), 1)
    gc_ref[...] = jnp.where(lane < HEADS, zc, _log_sigmoid(zc))
    zr = ut[DF_W + M_W:DF_W + M_W + 2 * HEADS] + gbr_ref[...]
    row = lax.broadcasted_iota(jnp.int32, (2 * HEADS, 1), 0)
    gr_ref[...] = jnp.where(row < HEADS, zr, _log_sigmoid(zr))


def _proj(x, nw, layer, w, wt, cw, cb, gbc, gbr, lb, *, batch, tm, act_dtype):
    m, d = x.shape
    seq = m // batch
    nt = seq // tm
    npc = w.shape[-1]
    tok = lambda width: pl.BlockSpec((tm, width), lambda b, t: (b * nt + t, 0))
    lay = lambda *shape: pl.BlockSpec((None,) + shape, lambda b, t: (layer,) + (0,) * len(shape),
                                      pipeline_mode=pl.Buffered(1))
    sds = lambda width, dt: jax.ShapeDtypeStruct((m, width), dt)
    out_shape = [sds(M_W, act_dtype), sds(M_W, act_dtype),
                 jax.ShapeDtypeStruct((batch, nt, M_W, tm), act_dtype), sds(M_W, act_dtype),
                 sds(GATE_PAD, F32), jax.ShapeDtypeStruct((batch, 2 * HEADS, seq), F32),
                 sds(DF_W, act_dtype), sds(DF_W, act_dtype),
                 jax.ShapeDtypeStruct((batch, nt, DF_W, tm), act_dtype),
                 sds(M_W, act_dtype), sds(M_W, F32), sds(M_W, act_dtype), sds(M_W, act_dtype), sds(M_W, act_dtype)]
    out_specs = [tok(M_W), tok(M_W), pl.BlockSpec((None, None, M_W, tm), lambda b, t: (b, t, 0, 0)), tok(M_W),
                 tok(GATE_PAD),
                 pl.BlockSpec((None, 2 * HEADS, tm), lambda b, t: (b, 0, t)),
                 tok(DF_W), tok(DF_W),
                 pl.BlockSpec((None, None, DF_W, tm), lambda b, t: (b, t, 0, 0)),
                 tok(M_W), tok(M_W), tok(M_W), tok(M_W), tok(M_W)]
    return pl.pallas_call(
        _proj_kernel,
        grid=(batch, nt),
        in_specs=[tok(d),
                  pl.BlockSpec((None, None, 1, d), lambda b, t: (layer, 2, 0, 0)),
                  lay(d, npc), lay(wt.shape[1], d), lay(CONV_W, 2 * M_W), lay(1, 2 * M_W),
                  lay(1, GATE_PAD), lay(2 * HEADS, 1), lay(1, M_W)],
        out_specs=out_specs,
        out_shape=out_shape,
        scratch_shapes=[pltpu.VMEM((tm + 8, 2 * M_W), F32)],
        compiler_params=_cparams(("parallel", "arbitrary")),
        name="mixer_proj",
    )(x, nw, w, wt, cw, cb, gbc, gbr, lb)


def _mlstm_kernel(q_ref, k_ref, v_ref, gc_ref, gr_ref, h_ref, c_ref, n_ref, m_ref):
    @pl.when(pl.program_id(1) == 0)
    def _():
        c_ref[...] = jnp.zeros_like(c_ref)
        n_ref[...] = jnp.zeros_like(n_ref)
        m_ref[...] = jnp.zeros_like(m_ref)

    for sq in range(q_ref.shape[0]):
        _mlstm_chunk(*(r.at[sq] for r in (q_ref, k_ref, v_ref, gc_ref, gr_ref, h_ref, c_ref, n_ref, m_ref)))


def _mlstm_chunk(q_ref, k_ref, vt_ref, gc_ref, gr_ref, h_ref, ct_ref, n_ref, m_ref):
    L = q_ref.shape[0]
    qb = q_ref[...]
    kb = k_ref[...]
    vt = vt_ref[...]
    gc = gc_ref[...]
    gr = gr_ref[...]
    row = lax.broadcasted_iota(jnp.int32, (L, L), 0)
    col = lax.broadcasted_iota(jnp.int32, (L, L), 1)
    allowed = row <= col
    bcol = _dot01_lhs((row >= col).astype(BF16), gc, 3)
    brow = _dot01_rhs(gr, allowed.astype(BF16), 3)
    lane_head = _head_of(lax.broadcasted_iota(jnp.int32, (1, M_W), 1))
    r8 = lax.broadcasted_iota(jnp.int32, (2 * HEADS, M_W), 0)
    own8 = _head_of(lax.broadcasted_iota(jnp.int32, (2 * HEADS, M_W), 1)) == r8
    r8l = lax.broadcasted_iota(jnp.int32, (2 * HEADS, L), 0)
    ct = ct_ref[...]
    nrow = n_ref[...]
    nt_dims = (((1,), (1,)), ((), ()))
    qct = lax.dot_general(ct.astype(BF16), qb, nt_dims, preferred_element_type=F32)
    qn8 = None
    for part in _split_bf16(jnp.where(own8, nrow, 0.0), 3):
        t = lax.dot_general(part, qb, nt_dims, preferred_element_type=F32)
        qn8 = t if qn8 is None else qn8 + t
    hts, vws = [], []
    wa8 = jnp.zeros((2 * HEADS, L), F32)
    dec_all = jnp.zeros((1, M_W), F32)
    for h in range(HEADS):
        hm = lane_head == h
        rows = slice(h * HEAD_W, (h + 1) * HEAD_W)
        b_row = brow[HEADS + h:HEADS + h + 1, :]
        keycol = gc[:, h:h + 1] - bcol[:, HEADS + h:HEADS + h + 1]
        mprev = m_ref[h:h + 1, 0:1]
        dmat = jnp.where(allowed, b_row + keycol, -jnp.inf)
        m_inter = b_row + mprev
        mt = jnp.maximum(jnp.max(dmat, axis=0, keepdims=True), m_inter)
        w = jnp.exp(dmat - mt)
        qh = jnp.where(hm, qb, jnp.zeros_like(qb))
        sc = lax.dot_general(kb, qh, nt_dims, preferred_element_type=F32) * w
        g = jnp.exp(m_inter - mt)
        pv = jnp.dot(vt[rows], sc.astype(BF16), preferred_element_type=F32)
        den = jnp.sum(sc, axis=0, keepdims=True) + g * qn8[h:h + 1, :]
        scale = 1.0 / jnp.maximum(jnp.abs(den), jnp.exp(-mt))
        hh = (pv + g * qct[rows]) * scale
        hc = hh - jnp.mean(hh, axis=0, keepdims=True)
        hts.append(hc * lax.rsqrt(jnp.mean(hc * hc, axis=0, keepdims=True) + EPS))
        bl = b_row[:, L - 1:L]
        a = bl - b_row + gr[h:h + 1, :]
        mnew = jnp.maximum(bl + mprev, jnp.max(a, axis=1, keepdims=True))
        wa = jnp.exp(a - mnew)
        vws.append(vt[rows].astype(F32) * wa)
        wa8 = jnp.where(r8l == h, wa, wa8)
        dec_all = jnp.where(hm, jnp.exp(bl + mprev - mnew), dec_all)
        m_ref[h:h + 1, :] = jnp.broadcast_to(mnew, (1, m_ref.shape[1]))
    h_ref[...] = jnp.concatenate(hts, axis=0).T.astype(h_ref.dtype)
    cnew = jnp.dot(jnp.concatenate(vws, axis=0).astype(BF16), kb, preferred_element_type=F32)
    r2 = _head_of(lax.broadcasted_iota(jnp.int32, (M_W, M_W), 0))
    c2 = _head_of(lax.broadcasted_iota(jnp.int32, (M_W, M_W), 1))
    ct_ref[...] = dec_all * ct + jnp.where(r2 == c2, cnew, 0.0)
    kn8 = _dot_split_lhs(wa8, kb)
    n_ref[...] = dec_all * nrow + jnp.sum(jnp.where(own8, kn8, 0.0), axis=0, keepdims=True)


def _dot_split_lhs(x, y_bf16):
    acc = None
    for p in _split_bf16(x, 3):
        t = jnp.dot(p, y_bf16, preferred_element_type=F32)
        acc = t if acc is None else acc + t
    return acc


def _mlstm(qm, km, vmt, gc, gr, *, batch, chunk, nb):
    m = qm.shape[0]
    seq = m // batch
    per = vmt.shape[3] // chunk
    view = lambda a: a.reshape(batch, seq, a.shape[-1])
    tok = lambda width: pl.BlockSpec((nb, chunk, width), lambda b, c: (b, c, 0))
    out = pl.pallas_call(
        _mlstm_kernel,
        grid=(batch // nb, seq // chunk),
        in_specs=[tok(M_W), tok(M_W),
                  pl.BlockSpec((nb, None, M_W, chunk), lambda b, c: (b, c // per, 0, c % per)),
                  tok(GATE_PAD),
                  pl.BlockSpec((nb, 2 * HEADS, chunk), lambda b, c: (b, 0, c))],
        out_specs=tok(M_W),
        out_shape=jax.ShapeDtypeStruct((batch, seq, M_W), BF16),
        scratch_shapes=[pltpu.VMEM((nb, M_W, M_W), F32), pltpu.VMEM((nb, 1, M_W), F32),
                        pltpu.VMEM((nb, 8, 128), F32)],
        compiler_params=_cparams(("parallel", "arbitrary")),
        name="mlstm",
    )(view(qm), view(km), vmt, view(gc), gr)
    return out.reshape(m, M_W)


HG_BLK = 16


def _hgrn_kernel(q_ref, k_ref, v_ref, lf_ref, o_ref, st_ref, p_ref):
    @pl.when(pl.program_id(1) == 0)
    def _():
        st_ref[...] = jnp.zeros_like(st_ref)

    nb = q_ref.shape[0]
    nblk = q_ref.shape[1] // HG_BLK
    rowid =lax.broadcasted_iota(jnp.int32, (HG_BLK, M_W), 0)
    rowid8 = lax.broadcasted_iota(jnp.int32, (HG_BLK // 2, M_W), 0)
    r2 = _head_of(lax.broadcasted_iota(jnp.int32, (M_W, M_W), 0))
    c2 = _head_of(lax.broadcasted_iota(jnp.int32, (M_W, M_W), 1))
    same_head = r2 == c2
    ones_bd = same_head.astype(BF16)

    def block(i, sq):
        r0 = pl.multiple_of(i * HG_BLK, HG_BLK)
        q = q_ref[sq, pl.ds(r0, HG_BLK), :].astype(F32)
        k = k_ref[sq, pl.ds(r0, HG_BLK), :].astype(F32)
        v = v_ref[sq, pl.ds(r0, HG_BLK), :].astype(F32)
        lf = lf_ref[sq, pl.ds(r0, HG_BLK), :]
        b = lf
        for d in (1, 2, 4, 8):
            b = b + jnp.where(rowid >= d, pltpu.roll(b, d, axis=0), 0.0)
        bl = b[HG_BLK - 1:HG_BLK, :]
        st = st_ref[sq]
        o = lax.dot_general((q * jnp.exp(b)).astype(BF16), st.astype(BF16), (((1,), (1,)), ((), ())),
                            preferred_element_type=F32)
        hb = HG_BLK // 2
        pbuf = p_ref.at[sq]
        for half in range(2):
            qq, bq = q[half * hb:(half + 1) * hb], b[half * hb:(half + 1) * hb]
            for s in range(half * hb, (half + 1) * hb):
                dd = jnp.where(rowid8 >= s - half * hb, bq - b[s:s + 1, :], -jnp.inf)
                pbuf[s * hb:(s + 1) * hb, :] = qq * k[s:s + 1, :] * jnp.exp(dd)
        bmid = b[hb - 1:hb, :]
        qe1 = q[hb:] * jnp.exp(b[hb:] - bmid)
        ke0 = k[0:hb] * jnp.exp(bmid - b[0:hb])
        for s in range(hb):
            pbuf[(HG_BLK + s) * hb:(HG_BLK + s + 1) * hb, :] = qe1 * ke0[s:s + 1, :]
        abig = jnp.dot(pbuf[...].astype(BF16), ones_bd, preferred_element_type=F32)
        grp = lambda g: abig[g * hb:(g + 1) * hb, :]
        o0 = grp(0) * v[0:1, :]
        o1 = grp(hb) * v[hb:hb + 1, :] + grp(HG_BLK) * v[0:1, :]
        for s in range(1, hb):
            o0 = o0 + grp(s) * v[s:s + 1, :]
            o1 = o1 + grp(hb + s) * v[hb + s:hb + s + 1, :] + grp(HG_BLK + s) * v[s:s + 1, :]
        o_ref[sq, pl.ds(r0, HG_BLK), :] = (o + jnp.concatenate([o0, o1], axis=0)).astype(o_ref.dtype)
        ke = (k * jnp.exp(bl - b)).astype(BF16)
        upd = lax.dot_general(v.astype(BF16), ke, (((0,), (0,)), ((), ())), preferred_element_type=F32)
        st_ref[sq] = st * jnp.exp(bl) + jnp.where(same_head, upd, 0.0)

    def body(i, carry):
        for sq in range(nb):
            block(i, sq)
        return carry

    lax.fori_loop(0, nblk, body, 0)


def _hgrn(qh, kh, vh, lfh, *, batch, tile, nb):
    m = qh.shape[0]
    seq = m // batch
    view = lambda a: a.reshape(batch, seq, M_W)
    tok = pl.BlockSpec((nb, tile, M_W), lambda b, t: (b, t, 0))
    out = pl.pallas_call(
        _hgrn_kernel,
        grid=(batch // nb, seq // tile),
        in_specs=[tok, tok, tok, tok],
        out_specs=tok,
        out_shape=jax.ShapeDtypeStruct((batch, seq, M_W), BF16),
        scratch_shapes=[pltpu.VMEM((nb, M_W, M_W), F32), pltpu.VMEM((nb, 3 * HG_BLK * HG_BLK // 4, M_W), F32)],
        compiler_params=_cparams(("parallel", "arbitrary")),
        name="hgrn2",
    )(view(qh), view(kh), view(vh), view(lfh))
    return out.reshape(m, M_W)


ONES_ROWS = 16


def _attn_kernel(q_ref, k_ref, vt_ref, bias_ref, lam0_ref, lv_ref, nwt_ref, o_ref, m_ref, acc_ref, sa_ref, sb_ref,
                 mca_ref, mcb_ref, *, tile):
    qi = pl.program_id(2)
    vchunk = vt_ref.shape[2]
    nvc = tile // vchunk
    m_ref[...] = jnp.full_like(m_ref, NEG)
    acc_ref[...] = jnp.zeros_like(acc_ref)
    q = q_ref[...]
    first = lax.broadcasted_iota(jnp.int32, (1, DF_HW), 1) < HEAD_W
    zero = jnp.zeros_like(q)
    qs = (jnp.where(first, q, zero), jnp.where(first, zero, q))
    ones = jnp.ones((ONES_ROWS, vchunk), BF16)

    def scores(ki, s_ref, mc_ref):
        kb = k_ref[pl.ds(pl.multiple_of(ki * tile, tile), tile), :]
        bias = bias_ref[jnp.minimum(qi - ki, 2)]
        for j in range(2):
            st = bias + lax.dot_general(kb, qs[j], (((1,), (1,)), ((), ())), preferred_element_type=F32)
            s_ref[j] = st.astype(BF16)
            mc_ref[j] = jnp.max(st, axis=0, keepdims=True).astype(BF16).astype(F32)

    def softmax_pv(ki, s_ref, mc_ref):
        vts = [jnp.concatenate([vt_ref[ki * nvc + c], ones], axis=0) for c in range(nvc)]
        for j in range(2):
            m_old = m_ref[j]
            m_new = jnp.maximum(m_old, mc_ref[j])
            alpha = jnp.exp2(m_old - m_new)
            pt = jnp.exp2(s_ref[j] - m_new.astype(BF16))
            pv = None
            for c in range(nvc):
                t = jnp.dot(vts[c], pt[c * vchunk:(c + 1) * vchunk, :], preferred_element_type=F32)
                pv = t if pv is None else pv + t
            acc_ref[j] = alpha * acc_ref[j] + pv
            m_ref[j] = m_new

    ntiles = qi + 1
    scores(0, sa_ref, mca_ref)

    def pair(p, carry):
        ka = 2 * p
        scores(ka + 1, sb_ref, mcb_ref)
        softmax_pv(ka, sa_ref, mca_ref)
        scores(ka + 2, sa_ref, mca_ref)
        softmax_pv(ka + 1, sb_ref, mcb_ref)
        return carry

    lax.fori_loop(0, (ntiles - 1) // 2, pair, 0)

    @pl.when(ntiles % 2 == 0)
    def _():
        scores(qi, sb_ref, mcb_ref)
        softmax_pv(qi - 1, sa_ref, mca_ref)
        softmax_pv(qi, sb_ref, mcb_ref)

    @pl.when(ntiles % 2 == 1)
    def _():
        softmax_pv(qi, sa_ref, mca_ref)

    lv = lv_ref[...]
    lam0 = lam0_ref[0]
    lam = (jnp.exp(jnp.sum(lv[0:1] * lv[1:2], axis=1, keepdims=True))
           - jnp.exp(jnp.sum(lv[2:3] * lv[3:4], axis=1, keepdims=True)) + lam0)
    a1 = acc_ref[0]
    a2 = acc_ref[1]
    od = a1[0:DF_HW] / a1[DF_HW:DF_HW + 1] - lam * (a2[0:DF_HW] / a2[DF_HW:DF_HW + 1])
    ms = jnp.mean(od * od, axis=0, keepdims=True)
    yt = od * lax.rsqrt(ms + EPS) * nwt_ref[...] * (1.0 - lam0)
    o_ref[...] = yt.T.astype(o_ref.dtype)


def _diff_attn(qd, kd, vdt, bias_tiles, lam0, lam_vecs, norm_w_t, layer, *, batch, tile):
    m = qd.shape[0]
    seq = m // batch
    nq = seq // tile
    nchunk, vchunk = vdt.shape[1], vdt.shape[3]
    return pl.pallas_call(
        functools.partial(_attn_kernel, tile=tile),
        grid=(HEADS, batch, nq),
        in_specs=[
            pl.BlockSpec((tile, DF_HW), lambda h, b, i: (b * nq + i, h)),
            pl.BlockSpec((seq, DF_HW), lambda h, b, i: (b, h)),
            pl.BlockSpec((None, nchunk, DF_HW, vchunk), lambda h, b, i: (b, 0, h, 0)),
            pl.BlockSpec((None, 3, tile, tile), lambda h, b, i: (h, 0, 0, 0)),
            pl.BlockSpec(memory_space=pltpu.SMEM),
            pl.BlockSpec((None, 4, HEAD_W), lambda h, b, i: (layer, 0, 0)),
            pl.BlockSpec((None, DF_HW, 1), lambda h, b, i: (layer, h, 0)),
        ],
        out_specs=pl.BlockSpec((tile, DF_HW), lambda h, b, i: (b * nq + i, h)),
        out_shape=jax.ShapeDtypeStruct((m, DF_W), BF16),
        scratch_shapes=[pltpu.VMEM((2, 1, tile), F32), pltpu.VMEM((2, DF_HW + ONES_ROWS, tile), F32),
                        pltpu.VMEM((2, tile, tile), BF16), pltpu.VMEM((2, tile, tile), BF16),
                        pltpu.VMEM((2, 1, tile), F32), pltpu.VMEM((2, 1, tile), F32)],
        compiler_params=_cparams(("parallel", "parallel", "arbitrary")),
        name="diff_attn",
    )(qd, kd, vdt, bias_tiles, lam0, lam_vecs, norm_w_t)


def _mix_out_kernel(x_ref, hm_ref, om_ref, yd_ref, oh_ref, gh_ref, mnw_ref, hnw_ref, w_ref, nw_ref, o_ref):
    r2 = _head_of(lax.broadcasted_iota(jnp.int32, (M_W, M_W), 0))
    c2 = _head_of(lax.broadcasted_iota(jnp.int32, (M_W, M_W), 1))
    ones_bd = (r2 == c2).astype(BF16)
    inv = 1.0 / HEAD_W

    ym = hm_ref[...].astype(F32) * mnw_ref[...] * _sigmoid(om_ref[...].astype(F32))

    oh = oh_ref[...].astype(F32)
    ms = _dot01_rhs(oh * oh, ones_bd, 2) * inv
    gh = gh_ref[...].astype(F32)
    yh = oh * lax.rsqrt(ms + EPS) * hnw_ref[...] * (gh * _sigmoid(gh))

    w = w_ref[...]
    acc = jnp.dot(ym.astype(BF16), w[0:M_W], preferred_element_type=F32)
    acc = acc + jnp.dot(yd_ref[...].astype(BF16), w[M_W:M_W + DF_W], preferred_element_type=F32)
    acc = acc + jnp.dot(yh.astype(BF16), w[M_W + DF_W:], preferred_element_type=F32)
    o_ref[...] = x_ref[...] + _rms(acc, nw_ref[...])


def _mix_out(x, hm, om, yd, oh, gh, mnw, hnw, w_out, nw, layer, *, tm):
    m, d = x.shape
    tok = lambda width: pl.BlockSpec((tm, width), lambda i: (i, 0))
    lay = lambda *shape: pl.BlockSpec((None,) + shape, lambda i: (layer,) + (0,) * len(shape))
    return pl.pallas_call(
        _mix_out_kernel,
        grid=(m // tm,),
        in_specs=[tok(d), tok(M_W), tok(M_W), tok(DF_W), tok(M_W), tok(M_W),
                  lay(1, M_W), lay(1, M_W), lay(d, d),
                  pl.BlockSpec((None, None, 1, d), lambda i: (layer, 3, 0, 0))],
        out_specs=tok(d),
        out_shape=jax.ShapeDtypeStruct((m, d), F32),
        compiler_params=_cparams(("parallel",)),
        name="mix_out",
    )(x, hm, om, yd, oh, gh, mnw, hnw, w_out, nw)


def _tiles(batch, seq):
    return dict(
        ffn_tm=min(512, seq), proj_tm=min(512, seq), mlstm_chunk=min(256, seq), hgrn_tile=min(256, seq),
        attn_tile=min(1024, seq), out_tm=min(512, seq), hgrn_nb=8 if batch % 8 == 0 else 1,
        mlstm_nb=2 if batch % 2 == 0 else 1)


def kernel(x, norm_w, ffn1_wi, ffn1_wo, ffn2_wi, ffn2_wo, w_in, w_out, mlstm_conv_w, mlstm_conv_b, mlstm_igate_b,
           mlstm_fgate_b, mlstm_norm_w, diff_lambda, diff_norm_w, rel_bias, hgrn_lb_logits, hgrn_norm_w):
    batch, seq, d = x.shape
    depth = norm_w.shape[0]
    tl = _tiles(batch, seq)
    act_dtype = BF16

    off = [0] + [int(v) for v in np.cumsum(SPLIT_SIZES)]
    seg = lambda i: w_in[:, :, off[i]:off[i + 1]].astype(BF16)
    gate_pad = jnp.zeros(w_in.shape[:2] + (GATE_PAD - 2 * HEADS,), BF16)
    w_proj = jnp.concatenate([seg(0), seg(1), seg(2), seg(5), seg(6), seg(8), seg(9), seg(10), seg(11),
                              seg(3), seg(4), gate_pad], axis=2)
    w_t = jnp.swapaxes(jnp.concatenate([seg(7), seg(1), seg(3), seg(4),
                                        jnp.zeros(w_in.shape[:2] + (2 * HEADS,), BF16)], axis=2), 1, 2)
    gate_b = jnp.concatenate([mlstm_igate_b, mlstm_fgate_b], axis=1).astype(F32)
    gate_b_col = jnp.pad(gate_b, ((0, 0), (0, GATE_PAD - 2 * HEADS)))[:, None, :]
    gate_b_row = gate_b[:, :, None]
    wi1, wo1 = ffn1_wi.astype(BF16), ffn1_wo.astype(BF16)
    wi2, wo2 = ffn2_wi.astype(BF16), ffn2_wo.astype(BF16)
    w_o = w_out.astype(BF16)
    nw = norm_w.astype(F32)[:, :, None, :]
    conv_w = mlstm_conv_w.astype(F32)
    conv_b = mlstm_conv_b.astype(F32)[:, None, :]
    mnw = mlstm_norm_w.astype(F32)[:, None, :]
    hnw = hgrn_norm_w.astype(F32)[:, None, :]
    dnw_t = diff_norm_w.astype(F32)[:, :, None]
    lam_vecs = diff_lambda.astype(F32)

    lb_all = _hgrn_lower_bounds(hgrn_lb_logits)[:, None, :]
    bias_tiles = _rel_bias_tiles(rel_bias, tl["attn_tile"])

    xf = x.reshape(batch * seq, d)
    for l in range(depth):
        lam0 = jnp.full((1,), 0.8 - 0.6 * math.exp(-0.3 * l), F32)
        xf = _ffn(xf, nw, l, 0, 1, wi1, wo1, tm=tl["ffn_tm"])
        (qm, km, vmt, om, gc, gr, qd, kd, vdt, qh, lfh, kh, vh, gh) = _proj(
            xf, nw, l, w_proj, w_t, conv_w, conv_b, gate_b_col, gate_b_row, lb_all,
            batch=batch, tm=tl["proj_tm"], act_dtype=act_dtype)
        hm = _mlstm(qm, km, vmt, gc, gr, batch=batch, chunk=tl["mlstm_chunk"], nb=tl["mlstm_nb"])
        yd = _diff_attn(qd, kd, vdt, bias_tiles, lam0, lam_vecs, dnw_t, l, batch=batch, tile=tl["attn_tile"])
        oh = _hgrn(qh, kh, vh, lfh, batch=batch, tile=tl["hgrn_tile"], nb=tl["hgrn_nb"])
        xf = _mix_out(xf, hm, om, yd, oh, gh, mnw, hnw, w_o, nw, l, tm=tl["out_tm"])
        xf = _ffn(xf, nw, l, 4, 5, wi2, wo2, tm=tl["ffn_tm"])
    return xf.reshape(batch, seq, d)
```

```python
import functools
import math

import numpy as np
import jax
import jax.numpy as jnp
from jax import lax
from jax.experimental import pallas as pl
from jax.experimental.pallas import tpu as pltpu

F32 = jnp.float32
BF16 = jnp.bfloat16
EPS = 1e-6
NEG = -1e30
LOG2E = 1.4426950408889634

HEADS = 4
HEAD_W = 64
M_W = HEADS * HEAD_W
DF_HW = 128
DF_W = HEADS * DF_HW
CONV_W = 4
REL_BUCKETS = 32
REL_MAX_EXACT = 16
REL_MAX_DIST = 128
GATE_PAD = 128
SPLIT_SIZES = (2 * M_W, M_W, M_W, HEADS, HEADS, DF_W, DF_W, DF_W, M_W, M_W, M_W, M_W)
VMEM_LIMIT = 56 * 1024 * 1024


def _cparams(sem):
    return pltpu.CompilerParams(dimension_semantics=sem, vmem_limit_bytes=VMEM_LIMIT)


def _sigmoid(x):
    return 1.0 / (1.0 + jnp.exp(-x))


def _log_sigmoid(x):
    return jnp.minimum(x, 0.0) - jnp.log1p(jnp.exp(-jnp.abs(x)))


def _rms(x, w):
    return x * lax.rsqrt(jnp.mean(x * x, axis=-1, keepdims=True) + EPS) * w


def _split_bf16(x, parts):
    out = []
    r = x
    for _ in range(parts):
        p = r.astype(BF16)
        out.append(p)
        r = r - p.astype(F32)
    return out


def _dot01_rhs(x, m01, parts):
    acc = None
    for p in _split_bf16(x, parts):
        t = jnp.dot(p, m01, preferred_element_type=F32)
        acc = t if acc is None else acc + t
    return acc


def _dot01_lhs(m01, x, parts):
    acc = None
    for p in _split_bf16(x, parts):
        t = jnp.dot(m01, p, preferred_element_type=F32)
        acc = t if acc is None else acc + t
    return acc


def _head_of(idx):
    return lax.shift_right_logical(idx, HEAD_W.bit_length() - 1)


def _lb_kernel(lg_ref, o_ref):
    lg = lg_ref[...]
    e = jnp.exp(lg - jnp.max(lg, axis=0, keepdims=True))
    sm = e / jnp.sum(e, axis=0, keepdims=True)
    depth = lg.shape[0]
    rows = []
    run = sm[0:1]
    first = run
    for i in range(depth):
        if i > 0:
            run = run + sm[i:i + 1]
        rows.append(jnp.maximum(run - first, 0.0))
    o_ref[...] = jnp.concatenate(rows, axis=0)


def _hgrn_lower_bounds(logits):
    return pl.pallas_call(
        _lb_kernel, out_shape=jax.ShapeDtypeStruct(logits.shape, F32), name="hgrn_lb")(logits.astype(F32))


def _bias_kernel(tab_ref, o_ref, *, tile):
    h = pl.program_id(0)
    typ = pl.program_id(1)
    r = lax.broadcasted_iota(jnp.int32, (tile, tile), 0)
    c = lax.broadcasted_iota(jnp.int32, (tile, tile), 1)
    rel = c - r + typ * tile
    n = jnp.maximum(rel, 0)
    nf = jnp.maximum(n, 1).astype(F32)
    large = REL_MAX_EXACT + (jnp.log(nf / REL_MAX_EXACT) / math.log(REL_MAX_DIST / REL_MAX_EXACT)
                             * (REL_BUCKETS - REL_MAX_EXACT)).astype(jnp.int32)
    large = jnp.minimum(large, REL_BUCKETS - 1)
    bucket = jnp.where(n < REL_MAX_EXACT, n, large)
    bias = jnp.zeros((tile, tile), F32)
    for b in range(REL_BUCKETS):
        bias = jnp.where(bucket == b, tab_ref[b, h], bias)
    o_ref[...] = jnp.where(rel >= 0, bias * LOG2E, NEG)


def _rel_bias_tiles(rel_bias, tile):
    assert tile >= REL_MAX_DIST
    return pl.pallas_call(
        functools.partial(_bias_kernel, tile=tile),
        grid=(HEADS, 3),
        in_specs=[pl.BlockSpec(memory_space=pltpu.SMEM)],
        out_specs=pl.BlockSpec((None, None, tile, tile), lambda h, t: (h, t, 0, 0)),
        out_shape=jax.ShapeDtypeStruct((HEADS, 3, tile, tile), F32),
        name="rel_bias_tiles",
    )(rel_bias.astype(F32))


def _ffn_kernel(x_ref, nwi_ref, wi_ref, wo_ref, nwo_ref, o_ref):
    dff = wo_ref.shape[0]
    x = x_ref[...]
    xn = _rms(x, nwi_ref[...]).astype(BF16)
    g = jnp.dot(xn, wi_ref[:, 0:dff], preferred_element_type=F32)
    u = jnp.dot(xn, wi_ref[:, dff:2 * dff], preferred_element_type=F32)
    a = (g * _sigmoid(g) * u).astype(BF16)
    h = jnp.dot(a, wo_ref[...], preferred_element_type=F32)
    o_ref[...] = x + 0.5 * _rms(h, nwo_ref[...])


def _ffn(x, nw, layer, row_in, row_out, wi, wo, *, tm):
    m, d = x.shape
    dff = wo.shape[1]
    resident = pl.Buffered(1)
    return pl.pallas_call(
        _ffn_kernel,
        grid=(m // tm,),
        in_specs=[
            pl.BlockSpec((tm, d), lambda i: (i, 0)),
            pl.BlockSpec((None, None, 1, d), lambda i: (layer, row_in, 0, 0)),
            pl.BlockSpec((None, d, 2 * dff), lambda i: (layer, 0, 0), pipeline_mode=resident),
            pl.BlockSpec((None, dff, d), lambda i: (layer, 0, 0), pipeline_mode=resident),
            pl.BlockSpec((None, None, 1, d), lambda i: (layer, row_out, 0, 0)),
        ],
        out_specs=pl.BlockSpec((tm, d), lambda i: (i, 0)),
        out_shape=jax.ShapeDtypeStruct((m, d), F32),
        compiler_params=_cparams(("parallel",)),
        name="ffn",
    )(x, nw, wi, wo, nw)


def _proj_kernel(x_ref, nw_ref, w_ref, wt_ref, cw_ref, cb_ref, gbc_ref, gbr_ref, lb_ref,
                 qm_ref, km_ref, vmt_ref, om_ref, gc_ref, gr_ref, qd_ref, kd_ref, vdt_ref,
                 qh_ref, lfh_ref, kh_ref, vh_ref, gh_ref, cbuf_ref):
    t = pl.program_id(1)
    tm = x_ref.shape[0]
    xn = _rms(x_ref[...], nw_ref[...]).astype(BF16)

    def u(lo, hi):
        return jnp.dot(xn, w_ref[:, lo:hi], preferred_element_type=F32)

    @pl.when(t == 0)
    def _():
        cbuf_ref[0:8, :] = jnp.zeros((8, 2 * M_W), F32)

    qk = u(0, 2 * M_W)
    cbuf_ref[8:8 + tm, :] = qk
    cw = cw_ref[...]
    y = cb_ref[...] + cw[CONV_W - 1:CONV_W] * qk
    for d in range(1, CONV_W):
        y = y + cw[CONV_W - 1 - d:CONV_W - d] * cbuf_ref[8 - d:8 - d + tm, :]
    cbuf_ref[0:8, :] = cbuf_ref[tm:tm + 8, :]
    y = y * _sigmoid(y)
    qm_ref[...] = y[:, 0:M_W].astype(qm_ref.dtype)
    km_ref[...] = (y[:, M_W:2 * M_W] * (HEAD_W ** -0.5)).astype(km_ref.dtype)
    ut = lax.dot_general(wt_ref[...], xn, (((1,), (1,)), ((), ())), preferred_element_type=F32)
    vdt_ref[...] = ut[0:DF_W].astype(vdt_ref.dtype)
    vmt_ref[...] = ut[DF_W:DF_W + M_W].astype(vmt_ref.dtype)
    om_ref[...] = u(768, 1024).astype(om_ref.dtype)

    qd_ref[...] = (u(1024, 1536) * (HEAD_W ** -0.5 * LOG2E)).astype(qd_ref.dtype)
    kd_ref[...] = u(1536, 2048).astype(kd_ref.dtype)

    qh = u(2048, 2304)
    qh_ref[...] = (qh * _sigmoid(qh)).astype(qh_ref.dtype)
    fp = u(2304, 2560)
    lb = lb_ref[...]
    a = jnp.log(lb)
    bb = jnp.log1p(-lb) + _log_sigmoid(fp)
    lfh_ref[...] = jnp.maximum(a, bb) + jnp.log1p(jnp.exp(-jnp.abs(a - bb)))
    kh_ref[...] = ((1.0 - lb) * _sigmoid(-fp)).astype(kh_ref.dtype)
    vh_ref[...] = u(2560, 2816).astype(vh_ref.dtype)
    gh_ref[...] = u(2816, 3072).astype(gh_ref.dtype)

    zc = u(3072, 3072 + GATE_PAD) + gbc_ref[...]
    lane = lax.broadcasted_iota(jnp.int32, (1, GATE_PAD), 1)
    gc_ref[...] = jnp.where(lane < HEADS, zc, _log_sigmoid(zc))
    zr = ut[DF_W + M_W:DF_W + M_W + 2 * HEADS] + gbr_ref[...]
    row = lax.broadcasted_iota(jnp.int32, (2 * HEADS, 1), 0)
    gr_ref[...] = jnp.where(row < HEADS, zr, _log_sigmoid(zr))


def _proj(x, nw, layer, w, wt, cw, cb, gbc, gbr, lb, *, batch, tm, act_dtype):
    m, d = x.shape
    seq = m // batch
    nt = seq // tm
    npc = w.shape[-1]
    tok = lambda width: pl.BlockSpec((tm, width), lambda b, t: (b * nt + t, 0))
    lay = lambda *shape: pl.BlockSpec((None,) + shape, lambda b, t: (layer,) + (0,) * len(shape),
                                      pipeline_mode=pl.Buffered(1))
    sds = lambda width, dt: jax.ShapeDtypeStruct((m, width), dt)
    out_shape = [sds(M_W, act_dtype), sds(M_W, act_dtype),
                 jax.ShapeDtypeStruct((batch, nt, M_W, tm), act_dtype), sds(M_W, act_dtype),
                 sds(GATE_PAD, F32), jax.ShapeDtypeStruct((batch, 2 * HEADS, seq), F32),
                 sds(DF_W, act_dtype), sds(DF_W, act_dtype),
                 jax.ShapeDtypeStruct((batch, nt, DF_W, tm), act_dtype),
                 sds(M_W, act_dtype), sds(M_W, F32), sds(M_W, act_dtype), sds(M_W, act_dtype), sds(M_W, act_dtype)]
    out_specs = [tok(M_W), tok(M_W), pl.BlockSpec((None, None, M_W, tm), lambda b, t: (b, t, 0, 0)), tok(M_W),
                 tok(GATE_PAD),
                 pl.BlockSpec((None, 2 * HEADS, tm), lambda b, t: (b, 0, t)),
                 tok(DF_W), tok(DF_W),
                 pl.BlockSpec((None, None, DF_W, tm), lambda b, t: (b, t, 0, 0)),
                 tok(M_W), tok(M_W), tok(M_W), tok(M_W), tok(M_W)]
    return pl.pallas_call(
        _proj_kernel,
        grid=(batch, nt),
        in_specs=[tok(d),
                  pl.BlockSpec((None, None, 1, d), lambda b, t: (layer, 2, 0, 0)),
                  lay(d, npc), lay(wt.shape[1], d), lay(CONV_W, 2 * M_W), lay(1, 2 * M_W),
                  lay(1, GATE_PAD), lay(2 * HEADS, 1), lay(1, M_W)],
        out_specs=out_specs,
        out_shape=out_shape,
        scratch_shapes=[pltpu.VMEM((tm + 8, 2 * M_W), F32)],
        compiler_params=_cparams(("parallel", "arbitrary")),
        name="mixer_proj",
    )(x, nw, w, wt, cw, cb, gbc, gbr, lb)


def _mlstm_kernel(q_ref, k_ref, v_ref, gc_ref, gr_ref, h_ref, c_ref, n_ref, m_ref):
    @pl.when(pl.program_id(1) == 0)
    def _():
        c_ref[...] = jnp.zeros_like(c_ref)
        n_ref[...] = jnp.zeros_like(n_ref)
        m_ref[...] = jnp.zeros_like(m_ref)

    for sq in range(q_ref.shape[0]):
        _mlstm_chunk(*(r.at[sq] for r in (q_ref, k_ref, v_ref, gc_ref, gr_ref, h_ref, c_ref, n_ref, m_ref)))


def _mlstm_chunk(q_ref, k_ref, vt_ref, gc_ref, gr_ref, h_ref, ct_ref, n_ref, m_ref):
    L = q_ref.shape[0]
    qb = q_ref[...]
    kb = k_ref[...]
    vt = vt_ref[...]
    gc = gc_ref[...]
    gr = gr_ref[...]
    row = lax.broadcasted_iota(jnp.int32, (L, L), 0)
    col = lax.broadcasted_iota(jnp.int32, (L, L), 1)
    allowed = row <= col
    bcol = _dot01_lhs((row >= col).astype(BF16), gc, 3)
    brow = _dot01_rhs(gr, allowed.astype(BF16), 3)
    lane_head = _head_of(lax.broadcasted_iota(jnp.int32, (1, M_W), 1))
    r8 = lax.broadcasted_iota(jnp.int32, (2 * HEADS, M_W), 0)
    own8 = _head_of(lax.broadcasted_iota(jnp.int32, (2 * HEADS, M_W), 1)) == r8
    r8l = lax.broadcasted_iota(jnp.int32, (2 * HEADS, L), 0)
    ct = ct_ref[...]
    nrow = n_ref[...]
    nt_dims = (((1,), (1,)), ((), ()))
    qct = lax.dot_general(ct.astype(BF16), qb, nt_dims, preferred_element_type=F32)
    qn8 = None
    for part in _split_bf16(jnp.where(own8, nrow, 0.0), 3):
        t = lax.dot_general(part, qb, nt_dims, preferred_element_type=F32)
        qn8 = t if qn8 is None else qn8 + t
    hts, vws = [], []
    wa8 = jnp.zeros((2 * HEADS, L), F32)
    dec_all = jnp.zeros((1, M_W), F32)
    for h in range(HEADS):
        hm = lane_head == h
        rows = slice(h * HEAD_W, (h + 1) * HEAD_W)
        b_row = brow[HEADS + h:HEADS + h + 1, :]
        keycol = gc[:, h:h + 1] - bcol[:, HEADS + h:HEADS + h + 1]
        mprev = m_ref[h:h + 1, 0:1]
        dmat = jnp.where(allowed, b_row + keycol, -jnp.inf)
        m_inter = b_row + mprev
        mt = jnp.maximum(jnp.max(dmat, axis=0, keepdims=True), m_inter)
        w = jnp.exp(dmat - mt)
        qh = jnp.where(hm, qb, jnp.zeros_like(qb))
        sc = lax.dot_general(kb, qh, nt_dims, preferred_element_type=F32) * w
        g = jnp.exp(m_inter - mt)
        pv = jnp.dot(vt[rows], sc.astype(BF16), preferred_element_type=F32)
        den = jnp.sum(sc, axis=0, keepdims=True) + g * qn8[h:h + 1, :]
        scale = 1.0 / jnp.maximum(jnp.abs(den), jnp.exp(-mt))
        hh = (pv + g * qct[rows]) * scale
        hc = hh - jnp.mean(hh, axis=0, keepdims=True)
        hts.append(hc * lax.rsqrt(jnp.mean(hc * hc, axis=0, keepdims=True) + EPS))
        bl = b_row[:, L - 1:L]
        a = bl - b_row + gr[h:h + 1, :]
        mnew = jnp.maximum(bl + mprev, jnp.max(a, axis=1, keepdims=True))
        wa = jnp.exp(a - mnew)
        vws.append(vt[rows].astype(F32) * wa)
        wa8 = jnp.where(r8l == h, wa, wa8)
        dec_all = jnp.where(hm, jnp.exp(bl + mprev - mnew), dec_all)
        m_ref[h:h + 1, :] = jnp.broadcast_to(mnew, (1, m_ref.shape[1]))
    h_ref[...] = jnp.concatenate(hts, axis=0).T.astype(h_ref.dtype)
    cnew = jnp.dot(jnp.concatenate(vws, axis=0).astype(BF16), kb, preferred_element_type=F32)
    r2 = _head_of(lax.broadcasted_iota(jnp.int32, (M_W, M_W), 0))
    c2 = _head_of(lax.broadcasted_iota(jnp.int32, (M_W, M_W), 1))
    ct_ref[...] = dec_all * ct + jnp.where(r2 == c2, cnew, 0.0)
    kn8 = _dot_split_lhs(wa8, kb)
    n_ref[...] = dec_all * nrow + jnp.sum(jnp.where(own8, kn8, 0.0), axis=0, keepdims=True)


def _dot_split_lhs(x, y_bf16):
    acc = None
    for p in _split_bf16(x, 3):
        t = jnp.dot(p, y_bf16, preferred_element_type=F32)
        acc = t if acc is None else acc + t
    return acc


def _mlstm(qm, km, vmt, gc, gr, *, batch, chunk, nb):
    m = qm.shape[0]
    seq = m // batch
    per = vmt.shape[3] // chunk
    view = lambda a: a.reshape(batch, seq, a.shape[-1])
    tok = lambda width: pl.BlockSpec((nb, chunk, width), lambda b, c: (b, c, 0))
    out = pl.pallas_call(
        _mlstm_kernel,
        grid=(batch // nb, seq // chunk),
        in_specs=[tok(M_W), tok(M_W),
                  pl.BlockSpec((nb, None, M_W, chunk), lambda b, c: (b, c // per, 0, c % per)),
                  tok(GATE_PAD),
                  pl.BlockSpec((nb, 2 * HEADS, chunk), lambda b, c: (b, 0, c))],
        out_specs=tok(M_W),
        out_shape=jax.ShapeDtypeStruct((batch, seq, M_W), BF16),
        scratch_shapes=[pltpu.VMEM((nb, M_W, M_W), F32), pltpu.VMEM((nb, 1, M_W), F32),
                        pltpu.VMEM((nb, 8, 128), F32)],
        compiler_params=_cparams(("parallel", "arbitrary")),
        name="mlstm",
    )(view(qm), view(km), vmt, view(gc), gr)
    return out.reshape(m, M_W)


HG_BLK = 16


def _hgrn_kernel(q_ref, k_ref, v_ref, lf_ref, o_ref, st_ref, p_ref):
    @pl.when(pl.program_id(1) == 0)
    def _():
        st_ref[...] = jnp.zeros_like(st_ref)

    nb = q_ref.shape[0]
    nblk = q_ref.shape[1] // HG_BLK
    rowid =lax.broadcasted_iota(jnp.int32, (HG_BLK, M_W), 0)
    rowid8 = lax.broadcasted_iota(jnp.int32, (HG_BLK // 2, M_W), 0)
    r2 = _head_of(lax.broadcasted_iota(jnp.int32, (M_W, M_W), 0))
    c2 = _head_of(lax.broadcasted_iota(jnp.int32, (M_W, M_W), 1))
    same_head = r2 == c2
    ones_bd = same_head.astype(BF16)

    def block(i, sq):
        r0 = pl.multiple_of(i * HG_BLK, HG_BLK)
        q = q_ref[sq, pl.ds(r0, HG_BLK), :].astype(F32)
        k = k_ref[sq, pl.ds(r0, HG_BLK), :].astype(F32)
        v = v_ref[sq, pl.ds(r0, HG_BLK), :].astype(F32)
        lf = lf_ref[sq, pl.ds(r0, HG_BLK), :]
        b = lf
        for d in (1, 2, 4, 8):
            b = b + jnp.where(rowid >= d, pltpu.roll(b, d, axis=0), 0.0)
        bl = b[HG_BLK - 1:HG_BLK, :]
        st = st_ref[sq]
        o = lax.dot_general((q * jnp.exp(b)).astype(BF16), st.astype(BF16), (((1,), (1,)), ((), ())),
                            preferred_element_type=F32)
        hb = HG_BLK // 2
        pbuf = p_ref.at[sq]
        for half in range(2):
            qq, bq = q[half * hb:(half + 1) * hb], b[half * hb:(half + 1) * hb]
            for s in range(half * hb, (half + 1) * hb):
                dd = jnp.where(rowid8 >= s - half * hb, bq - b[s:s + 1, :], -jnp.inf)
                pbuf[s * hb:(s + 1) * hb, :] = qq * k[s:s + 1, :] * jnp.exp(dd)
        bmid = b[hb - 1:hb, :]
        qe1 = q[hb:] * jnp.exp(b[hb:] - bmid)
        ke0 = k[0:hb] * jnp.exp(bmid - b[0:hb])
        for s in range(hb):
            pbuf[(HG_BLK + s) * hb:(HG_BLK + s + 1) * hb, :] = qe1 * ke0[s:s + 1, :]
        abig = jnp.dot(pbuf[...].astype(BF16), ones_bd, preferred_element_type=F32)
        grp = lambda g: abig[g * hb:(g + 1) * hb, :]
        o0 = grp(0) * v[0:1, :]
        o1 = grp(hb) * v[hb:hb + 1, :] + grp(HG_BLK) * v[0:1, :]
        for s in range(1, hb):
            o0 = o0 + grp(s) * v[s:s + 1, :]
            o1 = o1 + grp(hb + s) * v[hb + s:hb + s + 1, :] + grp(HG_BLK + s) * v[s:s + 1, :]
        o_ref[sq, pl.ds(r0, HG_BLK), :] = (o + jnp.concatenate([o0, o1], axis=0)).astype(o_ref.dtype)
        ke = (k * jnp.exp(bl - b)).astype(BF16)
        upd = lax.dot_general(v.astype(BF16), ke, (((0,), (0,)), ((), ())), preferred_element_type=F32)
        st_ref[sq] = st * jnp.exp(bl) + jnp.where(same_head, upd, 0.0)

    def body(i, carry):
        for sq in range(nb):
            block(i, sq)
        return carry

    lax.fori_loop(0, nblk, body, 0)


def _hgrn(qh, kh, vh, lfh, *, batch, tile, nb):
    m = qh.shape[0]
    seq = m // batch
    view = lambda a: a.reshape(batch, seq, M_W)
    tok = pl.BlockSpec((nb, tile, M_W), lambda b, t: (b, t, 0))
    out = pl.pallas_call(
        _hgrn_kernel,
        grid=(batch // nb, seq // tile),
        in_specs=[tok, tok, tok, tok],
        out_specs=tok,
        out_shape=jax.ShapeDtypeStruct((batch, seq, M_W), BF16),
        scratch_shapes=[pltpu.VMEM((nb, M_W, M_W), F32), pltpu.VMEM((nb, 3 * HG_BLK * HG_BLK // 4, M_W), F32)],
        compiler_params=_cparams(("parallel", "arbitrary")),
        name="hgrn2",
    )(view(qh), view(kh), view(vh), view(lfh))
    return out.reshape(m, M_W)


ONES_ROWS = 16


def _attn_kernel(q_ref, k_ref, vt_ref, bias_ref, lam0_ref, lv_ref, nwt_ref, o_ref, m_ref, acc_ref, sa_ref, sb_ref,
                 mca_ref, mcb_ref, *, tile):
    qi = pl.program_id(2)
    vchunk = vt_ref.shape[2]
    nvc = tile // vchunk
    m_ref[...] = jnp.full_like(m_ref, NEG)
    acc_ref[...] = jnp.zeros_like(acc_ref)
    q = q_ref[...]
    first = lax.broadcasted_iota(jnp.int32, (1, DF_HW), 1) < HEAD_W
    zero = jnp.zeros_like(q)
    qs = (jnp.where(first, q, zero), jnp.where(first, zero, q))
    ones = jnp.ones((ONES_ROWS, vchunk), BF16)

    def scores(ki, s_ref, mc_ref):
        kb = k_ref[pl.ds(pl.multiple_of(ki * tile, tile), tile), :]
        bias = bias_ref[jnp.minimum(qi - ki, 2)]
        for j in range(2):
            st = bias + lax.dot_general(kb, qs[j], (((1,), (1,)), ((), ())), preferred_element_type=F32)
            s_ref[j] = st.astype(BF16)
            mc_ref[j] = jnp.max(st, axis=0, keepdims=True).astype(BF16).astype(F32)

    def softmax_pv(ki, s_ref, mc_ref):
        vts = [jnp.concatenate([vt_ref[ki * nvc + c], ones], axis=0) for c in range(nvc)]
        for j in range(2):
            m_old = m_ref[j]
            m_new = jnp.maximum(m_old, mc_ref[j])
            alpha = jnp.exp2(m_old - m_new)
            pt = jnp.exp2(s_ref[j] - m_new.astype(BF16))
            pv = None
            for c in range(nvc):
                t = jnp.dot(vts[c], pt[c * vchunk:(c + 1) * vchunk, :], preferred_element_type=F32)
                pv = t if pv is None else pv + t
            acc_ref[j] = alpha * acc_ref[j] + pv
            m_ref[j] = m_new

    ntiles = qi + 1
    scores(0, sa_ref, mca_ref)

    def pair(p, carry):
        ka = 2 * p
        scores(ka + 1, sb_ref, mcb_ref)
        softmax_pv(ka, sa_ref, mca_ref)
        scores(ka + 2, sa_ref, mca_ref)
        softmax_pv(ka + 1, sb_ref, mcb_ref)
        return carry

    lax.fori_loop(0, (ntiles - 1) // 2, pair, 0)

    @pl.when(ntiles % 2 == 0)
    def _():
        scores(qi, sb_ref, mcb_ref)
        softmax_pv(qi - 1, sa_ref, mca_ref)
        softmax_pv(qi, sb_ref, mcb_ref)

    @pl.when(ntiles % 2 == 1)
    def _():
        softmax_pv(qi, sa_ref, mca_ref)

    lv = lv_ref[...]
    lam0 = lam0_ref[0]
    lam = (jnp.exp(jnp.sum(lv[0:1] * lv[1:2], axis=1, keepdims=True))
           - jnp.exp(jnp.sum(lv[2:3] * lv[3:4], axis=1, keepdims=True)) + lam0)
    a1 = acc_ref[0]
    a2 = acc_ref[1]
    od = a1[0:DF_HW] / a1[DF_HW:DF_HW + 1] - lam * (a2[0:DF_HW] / a2[DF_HW:DF_HW + 1])
    ms = jnp.mean(od * od, axis=0, keepdims=True)
    yt = od * lax.rsqrt(ms + EPS) * nwt_ref[...] * (1.0 - lam0)
    o_ref[...] = yt.T.astype(o_ref.dtype)


def _diff_attn(qd, kd, vdt, bias_tiles, lam0, lam_vecs, norm_w_t, layer, *, batch, tile):
    m = qd.shape[0]
    seq = m // batch
    nq = seq // tile
    nchunk, vchunk = vdt.shape[1], vdt.shape[3]
    return pl.pallas_call(
        functools.partial(_attn_kernel, tile=tile),
        grid=(HEADS, batch, nq),
        in_specs=[
            pl.BlockSpec((tile, DF_HW), lambda h, b, i: (b * nq + i, h)),
            pl.BlockSpec((seq, DF_HW), lambda h, b, i: (b, h)),
            pl.BlockSpec((None, nchunk, DF_HW, vchunk), lambda h, b, i: (b, 0, h, 0)),
            pl.BlockSpec((None, 3, tile, tile), lambda h, b, i: (h, 0, 0, 0)),
            pl.BlockSpec(memory_space=pltpu.SMEM),
            pl.BlockSpec((None, 4, HEAD_W), lambda h, b, i: (layer, 0, 0)),
            pl.BlockSpec((None, DF_HW, 1), lambda h, b, i: (layer, h, 0)),
        ],
        out_specs=pl.BlockSpec((tile, DF_HW), lambda h, b, i: (b * nq + i, h)),
        out_shape=jax.ShapeDtypeStruct((m, DF_W), BF16),
        scratch_shapes=[pltpu.VMEM((2, 1, tile), F32), pltpu.VMEM((2, DF_HW + ONES_ROWS, tile), F32),
                        pltpu.VMEM((2, tile, tile), BF16), pltpu.VMEM((2, tile, tile), BF16),
                        pltpu.VMEM((2, 1, tile), F32), pltpu.VMEM((2, 1, tile), F32)],
        compiler_params=_cparams(("parallel", "parallel", "arbitrary")),
        name="diff_attn",
    )(qd, kd, vdt, bias_tiles, lam0, lam_vecs, norm_w_t)


def _mix_out_kernel(x_ref, hm_ref, om_ref, yd_ref, oh_ref, gh_ref, mnw_ref, hnw_ref, w_ref, nw_ref, o_ref):
    r2 = _head_of(lax.broadcasted_iota(jnp.int32, (M_W, M_W), 0))
    c2 = _head_of(lax.broadcasted_iota(jnp.int32, (M_W, M_W), 1))
    ones_bd = (r2 == c2).astype(BF16)
    inv = 1.0 / HEAD_W

    ym = hm_ref[...].astype(F32) * mnw_ref[...] * _sigmoid(om_ref[...].astype(F32))

    oh = oh_ref[...].astype(F32)
    ms = _dot01_rhs(oh * oh, ones_bd, 2) * inv
    gh = gh_ref[...].astype(F32)
    yh = oh * lax.rsqrt(ms + EPS) * hnw_ref[...] * (gh * _sigmoid(gh))

    w = w_ref[...]
    acc = jnp.dot(ym.astype(BF16), w[0:M_W], preferred_element_type=F32)
    acc = acc + jnp.dot(yd_ref[...].astype(BF16), w[M_W:M_W + DF_W], preferred_element_type=F32)
    acc = acc + jnp.dot(yh.astype(BF16), w[M_W + DF_W:], preferred_element_type=F32)
    o_ref[...] = x_ref[...] + _rms(acc, nw_ref[...])


def _mix_out(x, hm, om, yd, oh, gh, mnw, hnw, w_out, nw, layer, *, tm):
    m, d = x.shape
    tok = lambda width: pl.BlockSpec((tm, width), lambda i: (i, 0))
    lay = lambda *shape: pl.BlockSpec((None,) + shape, lambda i: (layer,) + (0,) * len(shape))
    return pl.pallas_call(
        _mix_out_kernel,
        grid=(m // tm,),
        in_specs=[tok(d), tok(M_W), tok(M_W), tok(DF_W), tok(M_W), tok(M_W),
                  lay(1, M_W), lay(1, M_W), lay(d, d),
                  pl.BlockSpec((None, None, 1, d), lambda i: (layer, 3, 0, 0))],
        out_specs=tok(d),
        out_shape=jax.ShapeDtypeStruct((m, d), F32),
        compiler_params=_cparams(("parallel",)),
        name="mix_out",
    )(x, hm, om, yd, oh, gh, mnw, hnw, w_out, nw)


def _tiles(batch, seq):
    return dict(
        ffn_tm=min(512, seq), proj_tm=min(1024, seq), mlstm_chunk=min(256, seq), hgrn_tile=min(256, seq),
        attn_tile=min(1024, seq), out_tm=min(512, seq), hgrn_nb=8 if batch % 8 == 0 else 1,
        mlstm_nb=2 if batch % 2 == 0 else 1)


def kernel(x, norm_w, ffn1_wi, ffn1_wo, ffn2_wi, ffn2_wo, w_in, w_out, mlstm_conv_w, mlstm_conv_b, mlstm_igate_b,
           mlstm_fgate_b, mlstm_norm_w, diff_lambda, diff_norm_w, rel_bias, hgrn_lb_logits, hgrn_norm_w):
    batch, seq, d = x.shape
    depth = norm_w.shape[0]
    tl = _tiles(batch, seq)
    act_dtype = BF16

    off = [0] + [int(v) for v in np.cumsum(SPLIT_SIZES)]
    seg = lambda i: w_in[:, :, off[i]:off[i + 1]].astype(BF16)
    gate_pad = jnp.zeros(w_in.shape[:2] + (GATE_PAD - 2 * HEADS,), BF16)
    w_proj = jnp.concatenate([seg(0), seg(1), seg(2), seg(5), seg(6), seg(8), seg(9), seg(10), seg(11),
                              seg(3), seg(4), gate_pad], axis=2)
    w_t = jnp.swapaxes(jnp.concatenate([seg(7), seg(1), seg(3), seg(4),
                                        jnp.zeros(w_in.shape[:2] + (2 * HEADS,), BF16)], axis=2), 1, 2)
    gate_b = jnp.concatenate([mlstm_igate_b, mlstm_fgate_b], axis=1).astype(F32)
    gate_b_col = jnp.pad(gate_b, ((0, 0), (0, GATE_PAD - 2 * HEADS)))[:, None, :]
    gate_b_row = gate_b[:, :, None]
    wi1, wo1 = ffn1_wi.astype(BF16), ffn1_wo.astype(BF16)
    wi2, wo2 = ffn2_wi.astype(BF16), ffn2_wo.astype(BF16)
    w_o = w_out.astype(BF16)
    nw = norm_w.astype(F32)[:, :, None, :]
    conv_w = mlstm_conv_w.astype(F32)
    conv_b = mlstm_conv_b.astype(F32)[:, None, :]
    mnw = mlstm_norm_w.astype(F32)[:, None, :]
    hnw = hgrn_norm_w.astype(F32)[:, None, :]
    dnw_t = diff_norm_w.astype(F32)[:, :, None]
    lam_vecs = diff_lambda.astype(F32)

    lb_all = _hgrn_lower_bounds(hgrn_lb_logits)[:, None, :]
    bias_tiles = _rel_bias_tiles(rel_bias, tl["attn_tile"])

    xf = x.reshape(batch * seq, d)
    for l in range(depth):
        lam0 = jnp.full((1,), 0.8 - 0.6 * math.exp(-0.3 * l), F32)
        xf = _ffn(xf, nw, l, 0, 1, wi1, wo1, tm=tl["ffn_tm"])
        (qm, km, vmt, om, gc, gr, qd, kd, vdt, qh, lfh, kh, vh, gh) = _proj(
            xf, nw, l, w_proj, w_t, conv_w, conv_b, gate_b_col, gate_b_row, lb_all,
            batch=batch, tm=tl["proj_tm"], act_dtype=act_dtype)
        hm = _mlstm(qm, km, vmt, gc, gr, batch=batch, chunk=tl["mlstm_chunk"], nb=tl["mlstm_nb"])
        yd = _diff_attn(qd, kd, vdt, bias_tiles, lam0, lam_vecs, dnw_t, l, batch=batch, tile=tl["attn_tile"])
        oh = _hgrn(qh, kh, vh, lfh, batch=batch, tile=tl["hgrn_tile"], nb=tl["hgrn_nb"])
        xf = _mix_out(xf, hm, om, yd, oh, gh, mnw, hnw, w_o, nw, l, tm=tl["out_tm"])
        xf = _ffn(xf, nw, l, 4, 5, wi2, wo2, tm=tl["ffn_tm"])
    return xf.reshape(batch, seq, d)
```

```python
import functools
import math

import numpy as np
import jax
import jax.numpy as jnp
from jax import lax
from jax.experimental import pallas as pl
from jax.experimental.pallas import tpu as pltpu

F32 = jnp.float32
BF16 = jnp.bfloat16
EPS = 1e-6
NEG = -1e30
LOG2E = 1.4426950408889634

HEADS = 4
HEAD_W = 64
M_W = HEADS * HEAD_W
DF_HW = 128
DF_W = HEADS * DF_HW
CONV_W = 4
REL_BUCKETS = 32
REL_MAX_EXACT = 16
REL_MAX_DIST = 128
GATE_PAD = 128
SPLIT_SIZES = (2 * M_W, M_W, M_W, HEADS, HEADS, DF_W, DF_W, DF_W, M_W, M_W, M_W, M_W)
VMEM_LIMIT = 56 * 1024 * 1024


def _cparams(sem):
    return pltpu.CompilerParams(dimension_semantics=sem, vmem_limit_bytes=VMEM_LIMIT)


def _sigmoid(x):
    return 1.0 / (1.0 + jnp.exp(-x))


def _log_sigmoid(x):
    return jnp.minimum(x, 0.0) - jnp.log1p(jnp.exp(-jnp.abs(x)))


def _rms(x, w):
    return x * lax.rsqrt(jnp.mean(x * x, axis=-1, keepdims=True) + EPS) * w


def _split_bf16(x, parts):
    out = []
    r = x
    for _ in range(parts):
        p = r.astype(BF16)
        out.append(p)
        r = r - p.astype(F32)
    return out


def _dot01_rhs(x, m01, parts):
    acc = None
    for p in _split_bf16(x, parts):
        t = jnp.dot(p, m01, preferred_element_type=F32)
        acc = t if acc is None else acc + t
    return acc


def _dot01_lhs(m01, x, parts):
    acc = None
    for p in _split_bf16(x, parts):
        t = jnp.dot(m01, p, preferred_element_type=F32)
        acc = t if acc is None else acc + t
    return acc


def _head_of(idx):
    return lax.shift_right_logical(idx, HEAD_W.bit_length() - 1)


def _lb_kernel(lg_ref, o_ref):
    lg = lg_ref[...]
    e = jnp.exp(lg - jnp.max(lg, axis=0, keepdims=True))
    sm = e / jnp.sum(e, axis=0, keepdims=True)
    depth = lg.shape[0]
    rows = []
    run = sm[0:1]
    first = run
    for i in range(depth):
        if i > 0:
            run = run + sm[i:i + 1]
        rows.append(jnp.maximum(run - first, 0.0))
    o_ref[...] = jnp.concatenate(rows, axis=0)


def _hgrn_lower_bounds(logits):
    return pl.pallas_call(
        _lb_kernel, out_shape=jax.ShapeDtypeStruct(logits.shape, F32), name="hgrn_lb")(logits.astype(F32))


def _bias_kernel(tab_ref, o_ref, *, tile):
    h = pl.program_id(0)
    typ = pl.program_id(1)
    r = lax.broadcasted_iota(jnp.int32, (tile, tile), 0)
    c = lax.broadcasted_iota(jnp.int32, (tile, tile), 1)
    rel = c - r + typ * tile
    n = jnp.maximum(rel, 0)
    nf = jnp.maximum(n, 1).astype(F32)
    large = REL_MAX_EXACT + (jnp.log(nf / REL_MAX_EXACT) / math.log(REL_MAX_DIST / REL_MAX_EXACT)
                             * (REL_BUCKETS - REL_MAX_EXACT)).astype(jnp.int32)
    large = jnp.minimum(large, REL_BUCKETS - 1)
    bucket = jnp.where(n < REL_MAX_EXACT, n, large)
    bias = jnp.zeros((tile, tile), F32)
    for b in range(REL_BUCKETS):
        bias = jnp.where(bucket == b, tab_ref[b, h], bias)
    o_ref[...] = jnp.where(rel >= 0, bias * LOG2E, NEG)


def _rel_bias_tiles(rel_bias, tile):
    assert tile >= REL_MAX_DIST
    return pl.pallas_call(
        functools.partial(_bias_kernel, tile=tile),
        grid=(HEADS, 3),
        in_specs=[pl.BlockSpec(memory_space=pltpu.SMEM)],
        out_specs=pl.BlockSpec((None, None, tile, tile), lambda h, t: (h, t, 0, 0)),
        out_shape=jax.ShapeDtypeStruct((HEADS, 3, tile, tile), F32),
        name="rel_bias_tiles",
    )(rel_bias.astype(F32))


def _ffn_kernel(x_ref, nwi_ref, wi_ref, wo_ref, nwo_ref, o_ref):
    dff = wo_ref.shape[0]
    x = x_ref[...]
    xn = _rms(x, nwi_ref[...]).astype(BF16)
    g = jnp.dot(xn, wi_ref[:, 0:dff], preferred_element_type=F32)
    u = jnp.dot(xn, wi_ref[:, dff:2 * dff], preferred_element_type=F32)
    a = (g * _sigmoid(g) * u).astype(BF16)
    h = jnp.dot(a, wo_ref[...], preferred_element_type=F32)
    o_ref[...] = x + 0.5 * _rms(h, nwo_ref[...])


def _ffn(x, nw, layer, row_in, row_out, wi, wo, *, tm):
    m, d = x.shape
    dff = wo.shape[1]
    resident = pl.Buffered(1)
    return pl.pallas_call(
        _ffn_kernel,
        grid=(m // tm,),
        in_specs=[
            pl.BlockSpec((tm, d), lambda i: (i, 0)),
            pl.BlockSpec((None, None, 1, d), lambda i: (layer, row_in, 0, 0)),
            pl.BlockSpec((None, d, 2 * dff), lambda i: (layer, 0, 0), pipeline_mode=resident),
            pl.BlockSpec((None, dff, d), lambda i: (layer, 0, 0), pipeline_mode=resident),
            pl.BlockSpec((None, None, 1, d), lambda i: (layer, row_out, 0, 0)),
        ],
        out_specs=pl.BlockSpec((tm, d), lambda i: (i, 0)),
        out_shape=jax.ShapeDtypeStruct((m, d), F32),
        compiler_params=_cparams(("parallel",)),
        name="ffn",
    )(x, nw, wi, wo, nw)


def _proj_kernel(x_ref, nw_ref, w_ref, wt_ref, cw_ref, cb_ref, gbc_ref, gbr_ref, lb_ref,
                 qm_ref, km_ref, vmt_ref, om_ref, gc_ref, gr_ref, qd_ref, kd_ref, vdt_ref,
                 qh_ref, lfh_ref, kh_ref, vh_ref, gh_ref, cbuf_ref):
    t = pl.program_id(1)
    tm = x_ref.shape[0]
    xn = _rms(x_ref[...], nw_ref[...]).astype(BF16)

    def u(lo, hi):
        return jnp.dot(xn, w_ref[:, lo:hi], preferred_element_type=F32)

    @pl.when(t == 0)
    def _():
        cbuf_ref[0:8, :] = jnp.zeros((8, 2 * M_W), F32)

    qk = u(0, 2 * M_W)
    cbuf_ref[8:8 + tm, :] = qk
    cw = cw_ref[...]
    y = cb_ref[...] + cw[CONV_W - 1:CONV_W] * qk
    for d in range(1, CONV_W):
        y = y + cw[CONV_W - 1 - d:CONV_W - d] * cbuf_ref[8 - d:8 - d + tm, :]
    cbuf_ref[0:8, :] = cbuf_ref[tm:tm + 8, :]
    y = y * _sigmoid(y)
    qm_ref[...] = y[:, 0:M_W].astype(qm_ref.dtype)
    km_ref[...] = (y[:, M_W:2 * M_W] * (HEAD_W ** -0.5)).astype(km_ref.dtype)
    ut = lax.dot_general(wt_ref[...], xn, (((1,), (1,)), ((), ())), preferred_element_type=F32)
    vdt_ref[...] = ut[0:DF_W].astype(vdt_ref.dtype)
    vmt_ref[...] = ut[DF_W:DF_W + M_W].astype(vmt_ref.dtype)
    om_ref[...] = u(768, 1024).astype(om_ref.dtype)

    qd_ref[...] = (u(1024, 1536) * (HEAD_W ** -0.5 * LOG2E)).astype(qd_ref.dtype)
    kd_ref[...] = u(1536, 2048).astype(kd_ref.dtype)

    qh = u(2048, 2304)
    qh_ref[...] = (qh * _sigmoid(qh)).astype(qh_ref.dtype)
    fp = u(2304, 2560)
    lb = lb_ref[...]
    a = jnp.log(lb)
    bb = jnp.log1p(-lb) + _log_sigmoid(fp)
    lfh_ref[...] = jnp.maximum(a, bb) + jnp.log1p(jnp.exp(-jnp.abs(a - bb)))
    kh_ref[...] = ((1.0 - lb) * _sigmoid(-fp)).astype(kh_ref.dtype)
    vh_ref[...] = u(2560, 2816).astype(vh_ref.dtype)
    gh_ref[...] = u(2816, 3072).astype(gh_ref.dtype)

    zc = u(3072, 3072 + GATE_PAD) + gbc_ref[...]
    lane = lax.broadcasted_iota(jnp.int32, (1, GATE_PAD), 1)
    gc_ref[...] = jnp.where(lane < HEADS, zc, _log_sigmoid(zc))
    zr = ut[DF_W + M_W:DF_W + M_W + 2 * HEADS] + gbr_ref[...]
    row = lax.broadcasted_iota(jnp.int32, (2 * HEADS, 1), 0)
    gr_ref[...] = jnp.where(row < HEADS, zr, _log_sigmoid(zr))


def _proj(x, nw, layer, w, wt, cw, cb, gbc, gbr, lb, *, batch, tm, act_dtype):
    m, d = x.shape
    seq = m // batch
    nt = seq // tm
    npc = w.shape[-1]
    tok = lambda width: pl.BlockSpec((tm, width), lambda b, t: (b * nt + t, 0))
    lay = lambda *shape: pl.BlockSpec((None,) + shape, lambda b, t: (layer,) + (0,) * len(shape),
                                      pipeline_mode=pl.Buffered(1))
    sds = lambda width, dt: jax.ShapeDtypeStruct((m, width), dt)
    out_shape = [sds(M_W, act_dtype), sds(M_W, act_dtype),
                 jax.ShapeDtypeStruct((batch, nt, M_W, tm), act_dtype), sds(M_W, act_dtype),
                 sds(GATE_PAD, F32), jax.ShapeDtypeStruct((batch, 2 * HEADS, seq), F32),
                 sds(DF_W, act_dtype), sds(DF_W, act_dtype),
                 jax.ShapeDtypeStruct((batch, nt, DF_W, tm), act_dtype),
                 sds(M_W, act_dtype), sds(M_W, F32), sds(M_W, act_dtype), sds(M_W, act_dtype), sds(M_W, act_dtype)]
    out_specs = [tok(M_W), tok(M_W), pl.BlockSpec((None, None, M_W, tm), lambda b, t: (b, t, 0, 0)), tok(M_W),
                 tok(GATE_PAD),
                 pl.BlockSpec((None, 2 * HEADS, tm), lambda b, t: (b, 0, t)),
                 tok(DF_W), tok(DF_W),
                 pl.BlockSpec((None, None, DF_W, tm), lambda b, t: (b, t, 0, 0)),
                 tok(M_W), tok(M_W), tok(M_W), tok(M_W), tok(M_W)]
    return pl.pallas_call(
        _proj_kernel,
        grid=(batch, nt),
        in_specs=[tok(d),
                  pl.BlockSpec((None, None, 1, d), lambda b, t: (layer, 2, 0, 0)),
                  lay(d, npc), lay(wt.shape[1], d), lay(CONV_W, 2 * M_W), lay(1, 2 * M_W),
                  lay(1, GATE_PAD), lay(2 * HEADS, 1), lay(1, M_W)],
        out_specs=out_specs,
        out_shape=out_shape,
        scratch_shapes=[pltpu.VMEM((tm + 8, 2 * M_W), F32)],
        compiler_params=_cparams(("parallel", "arbitrary")),
        name="mixer_proj",
    )(x, nw, w, wt, cw, cb, gbc, gbr, lb)


def _mlstm_kernel(q_ref, k_ref, v_ref, gc_ref, gr_ref, h_ref, c_ref, n_ref, m_ref):
    @pl.when(pl.program_id(1) == 0)
    def _():
        c_ref[...] = jnp.zeros_like(c_ref)
        n_ref[...] = jnp.zeros_like(n_ref)
        m_ref[...] = jnp.zeros_like(m_ref)

    for sq in range(q_ref.shape[0]):
        _mlstm_chunk(*(r.at[sq] for r in (q_ref, k_ref, v_ref, gc_ref, gr_ref, h_ref, c_ref, n_ref, m_ref)))


def _mlstm_chunk(q_ref, k_ref, vt_ref, gc_ref, gr_ref, h_ref, ct_ref, n_ref, m_ref):
    L = q_ref.shape[0]
    qb = q_ref[...]
    kb = k_ref[...]
    vt = vt_ref[...]
    gc = gc_ref[...]
    gr = gr_ref[...]
    row = lax.broadcasted_iota(jnp.int32, (L, L), 0)
    col = lax.broadcasted_iota(jnp.int32, (L, L), 1)
    allowed = row <= col
    bcol = _dot01_lhs((row >= col).astype(BF16), gc, 3)
    brow = _dot01_rhs(gr, allowed.astype(BF16), 3)
    lane_head = _head_of(lax.broadcasted_iota(jnp.int32, (1, M_W), 1))
    r8 = lax.broadcasted_iota(jnp.int32, (2 * HEADS, M_W), 0)
    own8 = _head_of(lax.broadcasted_iota(jnp.int32, (2 * HEADS, M_W), 1)) == r8
    r8l = lax.broadcasted_iota(jnp.int32, (2 * HEADS, L), 0)
    ct = ct_ref[...]
    nrow = n_ref[...]
    nt_dims = (((1,), (1,)), ((), ()))
    qct = lax.dot_general(ct.astype(BF16), qb, nt_dims, preferred_element_type=F32)
    qn8 = None
    for part in _split_bf16(jnp.where(own8, nrow, 0.0), 3):
        t = lax.dot_general(part, qb, nt_dims, preferred_element_type=F32)
        qn8 = t if qn8 is None else qn8 + t
    hts, vws = [], []
    wa8 = jnp.zeros((2 * HEADS, L), F32)
    dec_all = jnp.zeros((1, M_W), F32)
    for h in range(HEADS):
        hm = lane_head == h
        rows = slice(h * HEAD_W, (h + 1) * HEAD_W)
        b_row = brow[HEADS + h:HEADS + h + 1, :]
        keycol = gc[:, h:h + 1] - bcol[:, HEADS + h:HEADS + h + 1]
        mprev = m_ref[h:h + 1, 0:1]
        dmat = jnp.where(allowed, b_row + keycol, -jnp.inf)
        m_inter = b_row + mprev
        mt = jnp.maximum(jnp.max(dmat, axis=0, keepdims=True), m_inter)
        w = jnp.exp(dmat - mt)
        qh = jnp.where(hm, qb, jnp.zeros_like(qb))
        sc = lax.dot_general(kb, qh, nt_dims, preferred_element_type=F32) * w
        g = jnp.exp(m_inter - mt)
        pv = jnp.dot(vt[rows], sc.astype(BF16), preferred_element_type=F32)
        den = jnp.sum(sc, axis=0, keepdims=True) + g * qn8[h:h + 1, :]
        scale = 1.0 / jnp.maximum(jnp.abs(den), jnp.exp(-mt))
        hh = (pv + g * qct[rows]) * scale
        hc = hh - jnp.mean(hh, axis=0, keepdims=True)
        hts.append(hc * lax.rsqrt(jnp.mean(hc * hc, axis=0, keepdims=True) + EPS))
        bl = b_row[:, L - 1:L]
        a = bl - b_row + gr[h:h + 1, :]
        mnew = jnp.maximum(bl + mprev, jnp.max(a, axis=1, keepdims=True))
        wa = jnp.exp(a - mnew)
        vws.append(vt[rows].astype(F32) * wa)
        wa8 = jnp.where(r8l == h, wa, wa8)
        dec_all = jnp.where(hm, jnp.exp(bl + mprev - mnew), dec_all)
        m_ref[h:h + 1, :] = jnp.broadcast_to(mnew, (1, m_ref.shape[1]))
    h_ref[...] = jnp.concatenate(hts, axis=0).T.astype(h_ref.dtype)
    cnew = jnp.dot(jnp.concatenate(vws, axis=0).astype(BF16), kb, preferred_element_type=F32)
    r2 = _head_of(lax.broadcasted_iota(jnp.int32, (M_W, M_W), 0))
    c2 = _head_of(lax.broadcasted_iota(jnp.int32, (M_W, M_W), 1))
    ct_ref[...] = dec_all * ct + jnp.where(r2 == c2, cnew, 0.0)
    kn8 = _dot_split_lhs(wa8, kb)
    n_ref[...] = dec_all * nrow + jnp.sum(jnp.where(own8, kn8, 0.0), axis=0, keepdims=True)


def _dot_split_lhs(x, y_bf16):
    acc = None
    for p in _split_bf16(x, 3):
        t = jnp.dot(p, y_bf16, preferred_element_type=F32)
        acc = t if acc is None else acc + t
    return acc


def _mlstm(qm, km, vmt, gc, gr, *, batch, chunk, nb):
    m = qm.shape[0]
    seq = m // batch
    per = vmt.shape[3] // chunk
    view = lambda a: a.reshape(batch, seq, a.shape[-1])
    tok = lambda width: pl.BlockSpec((nb, chunk, width), lambda b, c: (b, c, 0))
    out = pl.pallas_call(
        _mlstm_kernel,
        grid=(batch // nb, seq // chunk),
        in_specs=[tok(M_W), tok(M_W),
                  pl.BlockSpec((nb, None, M_W, chunk), lambda b, c: (b, c // per, 0, c % per)),
                  tok(GATE_PAD),
                  pl.BlockSpec((nb, 2 * HEADS, chunk), lambda b, c: (b, 0, c))],
        out_specs=tok(M_W),
        out_shape=jax.ShapeDtypeStruct((batch, seq, M_W), BF16),
        scratch_shapes=[pltpu.VMEM((nb, M_W, M_W), F32), pltpu.VMEM((nb, 1, M_W), F32),
                        pltpu.VMEM((nb, 8, 128), F32)],
        compiler_params=_cparams(("parallel", "arbitrary")),
        name="mlstm",
    )(view(qm), view(km), vmt, view(gc), gr)
    return out.reshape(m, M_W)


HG_BLK = 16


def _hgrn_kernel(q_ref, k_ref, v_ref, lf_ref, o_ref, st_ref, p_ref):
    @pl.when(pl.program_id(1) == 0)
    def _():
        st_ref[...] = jnp.zeros_like(st_ref)

    nb = q_ref.shape[0]
    nblk = q_ref.shape[1] // HG_BLK
    rowid =lax.broadcasted_iota(jnp.int32, (HG_BLK, M_W), 0)
    rowid8 = lax.broadcasted_iota(jnp.int32, (HG_BLK // 2, M_W), 0)
    r2 = _head_of(lax.broadcasted_iota(jnp.int32, (M_W, M_W), 0))
    c2 = _head_of(lax.broadcasted_iota(jnp.int32, (M_W, M_W), 1))
    same_head = r2 == c2
    ones_bd = same_head.astype(BF16)

    def block(i, sq):
        r0 = pl.multiple_of(i * HG_BLK, HG_BLK)
        q = q_ref[sq, pl.ds(r0, HG_BLK), :].astype(F32)
        k = k_ref[sq, pl.ds(r0, HG_BLK), :].astype(F32)
        v = v_ref[sq, pl.ds(r0, HG_BLK), :].astype(F32)
        lf = lf_ref[sq, pl.ds(r0, HG_BLK), :]
        b = lf
        for d in (1, 2, 4, 8):
            b = b + jnp.where(rowid >= d, pltpu.roll(b, d, axis=0), 0.0)
        bl = b[HG_BLK - 1:HG_BLK, :]
        st = st_ref[sq]
        o = lax.dot_general((q * jnp.exp(b)).astype(BF16), st.astype(BF16), (((1,), (1,)), ((), ())),
                            preferred_element_type=F32)
        hb = HG_BLK // 2
        pbuf = p_ref.at[sq]
        for half in range(2):
            qq, bq = q[half * hb:(half + 1) * hb], b[half * hb:(half + 1) * hb]
            for s in range(half * hb, (half + 1) * hb):
                dd = jnp.where(rowid8 >= s - half * hb, bq - b[s:s + 1, :], -jnp.inf)
                pbuf[s * hb:(s + 1) * hb, :] = qq * k[s:s + 1, :] * jnp.exp(dd)
        bmid = b[hb - 1:hb, :]
        qe1 = q[hb:] * jnp.exp(b[hb:] - bmid)
        ke0 = k[0:hb] * jnp.exp(bmid - b[0:hb])
        for s in range(hb):
            pbuf[(HG_BLK + s) * hb:(HG_BLK + s + 1) * hb, :] = qe1 * ke0[s:s + 1, :]
        abig = jnp.dot(pbuf[...].astype(BF16), ones_bd, preferred_element_type=F32)
        grp = lambda g: abig[g * hb:(g + 1) * hb, :]
        o0 = grp(0) * v[0:1, :]
        o1 = grp(hb) * v[hb:hb + 1, :] + grp(HG_BLK) * v[0:1, :]
        for s in range(1, hb):
            o0 = o0 + grp(s) * v[s:s + 1, :]
            o1 = o1 + grp(hb + s) * v[hb + s:hb + s + 1, :] + grp(HG_BLK + s) * v[s:s + 1, :]
        o_ref[sq, pl.ds(r0, HG_BLK), :] = (o + jnp.concatenate([o0, o1], axis=0)).astype(o_ref.dtype)
        ke = (k * jnp.exp(bl - b)).astype(BF16)
        upd = lax.dot_general(v.astype(BF16), ke, (((0,), (0,)), ((), ())), preferred_element_type=F32)
        st_ref[sq] = st * jnp.exp(bl) + jnp.where(same_head, upd, 0.0)

    def body(i, carry):
        for sq in range(nb):
            block(i, sq)
        return carry

    lax.fori_loop(0, nblk, body, 0)


def _hgrn(qh, kh, vh, lfh, *, batch, tile, nb):
    m = qh.shape[0]
    seq = m // batch
    view = lambda a: a.reshape(batch, seq, M_W)
    tok = pl.BlockSpec((nb, tile, M_W), lambda b, t: (b, t, 0))
    out = pl.pallas_call(
        _hgrn_kernel,
        grid=(batch // nb, seq // tile),
        in_specs=[tok, tok, tok, tok],
        out_specs=tok,
        out_shape=jax.ShapeDtypeStruct((batch, seq, M_W), BF16),
        scratch_shapes=[pltpu.VMEM((nb, M_W, M_W), F32), pltpu.VMEM((nb, 3 * HG_BLK * HG_BLK // 4, M_W), F32)],
        compiler_params=_cparams(("parallel", "arbitrary")),
        name="hgrn2",
    )(view(qh), view(kh), view(vh), view(lfh))
    return out.reshape(m, M_W)


ONES_ROWS = 16


def _attn_kernel(q_ref, k_ref, vt_ref, bias_ref, lam0_ref, lv_ref, nwt_ref, o_ref, m_ref, acc_ref, sa_ref, sb_ref,
                 mca_ref, mcb_ref, *, tile):
    qi = pl.program_id(2)
    vchunk = vt_ref.shape[2]
    nvc = tile // vchunk
    m_ref[...] = jnp.full_like(m_ref, NEG)
    acc_ref[...] = jnp.zeros_like(acc_ref)
    q = q_ref[...]
    first = lax.broadcasted_iota(jnp.int32, (1, DF_HW), 1) < HEAD_W
    zero = jnp.zeros_like(q)
    qs = (jnp.where(first, q, zero), jnp.where(first, zero, q))
    ones = jnp.ones((ONES_ROWS, vchunk), BF16)

    def scores(ki, s_ref, mc_ref):
        kb = k_ref[pl.ds(pl.multiple_of(ki * tile, tile), tile), :]
        bias = bias_ref[jnp.minimum(qi - ki, 2)]
        for j in range(2):
            st = bias + lax.dot_general(kb, qs[j], (((1,), (1,)), ((), ())), preferred_element_type=F32)
            s_ref[j] = st.astype(BF16)
            mc_ref[j] = jnp.max(st, axis=0, keepdims=True).astype(BF16).astype(F32)

    def softmax_pv(ki, s_ref, mc_ref):
        vts = [jnp.concatenate([vt_ref[ki * nvc + c], ones], axis=0) for c in range(nvc)]
        for j in range(2):
            m_old = m_ref[j]
            m_new = jnp.maximum(m_old, mc_ref[j])
            alpha = jnp.exp2(m_old - m_new)
            pt = jnp.exp2(s_ref[j] - m_new.astype(BF16))
            pv = None
            for c in range(nvc):
                t = jnp.dot(vts[c], pt[c * vchunk:(c + 1) * vchunk, :], preferred_element_type=F32)
                pv = t if pv is None else pv + t
            acc_ref[j] = alpha * acc_ref[j] + pv
            m_ref[j] = m_new

    ntiles = qi + 1
    scores(0, sa_ref, mca_ref)

    def pair(p, carry):
        ka = 2 * p
        scores(ka + 1, sb_ref, mcb_ref)
        softmax_pv(ka, sa_ref, mca_ref)
        scores(ka + 2, sa_ref, mca_ref)
        softmax_pv(ka + 1, sb_ref, mcb_ref)
        return carry

    lax.fori_loop(0, (ntiles - 1) // 2, pair, 0)

    @pl.when(ntiles % 2 == 0)
    def _():
        scores(qi, sb_ref, mcb_ref)
        softmax_pv(qi - 1, sa_ref, mca_ref)
        softmax_pv(qi, sb_ref, mcb_ref)

    @pl.when(ntiles % 2 == 1)
    def _():
        softmax_pv(qi, sa_ref, mca_ref)

    lv = lv_ref[...]
    lam0 = lam0_ref[0]
    lam = (jnp.exp(jnp.sum(lv[0:1] * lv[1:2], axis=1, keepdims=True))
           - jnp.exp(jnp.sum(lv[2:3] * lv[3:4], axis=1, keepdims=True)) + lam0)
    a1 = acc_ref[0]
    a2 = acc_ref[1]
    od = a1[0:DF_HW] / a1[DF_HW:DF_HW + 1] - lam * (a2[0:DF_HW] / a2[DF_HW:DF_HW + 1])
    ms = jnp.mean(od * od, axis=0, keepdims=True)
    yt = od * lax.rsqrt(ms + EPS) * nwt_ref[...] * (1.0 - lam0)
    o_ref[...] = yt.T.astype(o_ref.dtype)


def _diff_attn(qd, kd, vdt, bias_tiles, lam0, lam_vecs, norm_w_t, layer, *, batch, tile):
    m = qd.shape[0]
    seq = m // batch
    nq = seq // tile
    nchunk, vchunk = vdt.shape[1], vdt.shape[3]
    return pl.pallas_call(
        functools.partial(_attn_kernel, tile=tile),
        grid=(HEADS, batch, nq),
        in_specs=[
            pl.BlockSpec((tile, DF_HW), lambda h, b, i: (b * nq + i, h)),
            pl.BlockSpec((seq, DF_HW), lambda h, b, i: (b, h)),
            pl.BlockSpec((None, nchunk, DF_HW, vchunk), lambda h, b, i: (b, 0, h, 0)),
            pl.BlockSpec((None, 3, tile, tile), lambda h, b, i: (h, 0, 0, 0)),
            pl.BlockSpec(memory_space=pltpu.SMEM),
            pl.BlockSpec((None, 4, HEAD_W), lambda h, b, i: (layer, 0, 0)),
            pl.BlockSpec((None, DF_HW, 1), lambda h, b, i: (layer, h, 0)),
        ],
        out_specs=pl.BlockSpec((tile, DF_HW), lambda h, b, i: (b * nq + i, h)),
        out_shape=jax.ShapeDtypeStruct((m, DF_W), BF16),
        scratch_shapes=[pltpu.VMEM((2, 1, tile), F32), pltpu.VMEM((2, DF_HW + ONES_ROWS, tile), F32),
                        pltpu.VMEM((2, tile, tile), BF16), pltpu.VMEM((2, tile, tile), BF16),
                        pltpu.VMEM((2, 1, tile), F32), pltpu.VMEM((2, 1, tile), F32)],
        compiler_params=_cparams(("parallel", "parallel", "arbitrary")),
        name="diff_attn",
    )(qd, kd, vdt, bias_tiles, lam0, lam_vecs, norm_w_t)


def _mix_out_kernel(x_ref, hm_ref, om_ref, yd_ref, oh_ref, gh_ref, mnw_ref, hnw_ref, w_ref, nw_ref, o_ref):
    r2 = _head_of(lax.broadcasted_iota(jnp.int32, (M_W, M_W), 0))
    c2 = _head_of(lax.broadcasted_iota(jnp.int32, (M_W, M_W), 1))
    ones_bd = (r2 == c2).astype(BF16)
    inv = 1.0 / HEAD_W

    ym = hm_ref[...].astype(F32) * mnw_ref[...] * _sigmoid(om_ref[...].astype(F32))

    oh = oh_ref[...].astype(F32)
    ms = _dot01_rhs(oh * oh, ones_bd, 2) * inv
    gh = gh_ref[...].astype(F32)
    yh = oh * lax.rsqrt(ms + EPS) * hnw_ref[...] * (gh * _sigmoid(gh))

    w = w_ref[...]
    acc = jnp.dot(ym.astype(BF16), w[0:M_W], preferred_element_type=F32)
    acc = acc + jnp.dot(yd_ref[...].astype(BF16), w[M_W:M_W + DF_W], preferred_element_type=F32)
    acc = acc + jnp.dot(yh.astype(BF16), w[M_W + DF_W:], preferred_element_type=F32)
    o_ref[...] = x_ref[...] + _rms(acc, nw_ref[...])


def _mix_out(x, hm, om, yd, oh, gh, mnw, hnw, w_out, nw, layer, *, tm):
    m, d = x.shape
    tok = lambda width: pl.BlockSpec((tm, width), lambda i: (i, 0))
    lay = lambda *shape: pl.BlockSpec((None,) + shape, lambda i: (layer,) + (0,) * len(shape))
    return pl.pallas_call(
        _mix_out_kernel,
        grid=(m // tm,),
        in_specs=[tok(d), tok(M_W), tok(M_W), tok(DF_W), tok(M_W), tok(M_W),
                  lay(1, M_W), lay(1, M_W), lay(d, d),
                  pl.BlockSpec((None, None, 1, d), lambda i: (layer, 3, 0, 0))],
        out_specs=tok(d),
        out_shape=jax.ShapeDtypeStruct((m, d), F32),
        compiler_params=_cparams(("parallel",)),
        name="mix_out",
    )(x, hm, om, yd, oh, gh, mnw, hnw, w_out, nw)


def _tiles(batch, seq):
    return dict(
        ffn_tm=min(512, seq), proj_tm=min(1024, seq), mlstm_chunk=min(256, seq), hgrn_tile=min(256, seq),
        attn_tile=min(1024, seq), out_tm=min(1024, seq), hgrn_nb=8 if batch % 8 == 0 else 1,
        mlstm_nb=2 if batch % 2 == 0 else 1)


def kernel(x, norm_w, ffn1_wi, ffn1_wo, ffn2_wi, ffn2_wo, w_in, w_out, mlstm_conv_w, mlstm_conv_b, mlstm_igate_b,
           mlstm_fgate_b, mlstm_norm_w, diff_lambda, diff_norm_w, rel_bias, hgrn_lb_logits, hgrn_norm_w):
    batch, seq, d = x.shape
    depth = norm_w.shape[0]
    tl = _tiles(batch, seq)
    act_dtype = BF16

    off = [0] + [int(v) for v in np.cumsum(SPLIT_SIZES)]
    seg = lambda i: w_in[:, :, off[i]:off[i + 1]].astype(BF16)
    gate_pad = jnp.zeros(w_in.shape[:2] + (GATE_PAD - 2 * HEADS,), BF16)
    w_proj = jnp.concatenate([seg(0), seg(1), seg(2), seg(5), seg(6), seg(8), seg(9), seg(10), seg(11),
                              seg(3), seg(4), gate_pad], axis=2)
    w_t = jnp.swapaxes(jnp.concatenate([seg(7), seg(1), seg(3), seg(4),
                                        jnp.zeros(w_in.shape[:2] + (2 * HEADS,), BF16)], axis=2), 1, 2)
    gate_b = jnp.concatenate([mlstm_igate_b, mlstm_fgate_b], axis=1).astype(F32)
    gate_b_col = jnp.pad(gate_b, ((0, 0), (0, GATE_PAD - 2 * HEADS)))[:, None, :]
    gate_b_row = gate_b[:, :, None]
    wi1, wo1 = ffn1_wi.astype(BF16), ffn1_wo.astype(BF16)
    wi2, wo2 = ffn2_wi.astype(BF16), ffn2_wo.astype(BF16)
    w_o = w_out.astype(BF16)
    nw = norm_w.astype(F32)[:, :, None, :]
    conv_w = mlstm_conv_w.astype(F32)
    conv_b = mlstm_conv_b.astype(F32)[:, None, :]
    mnw = mlstm_norm_w.astype(F32)[:, None, :]
    hnw = hgrn_norm_w.astype(F32)[:, None, :]
    dnw_t = diff_norm_w.astype(F32)[:, :, None]
    lam_vecs = diff_lambda.astype(F32)

    lb_all = _hgrn_lower_bounds(hgrn_lb_logits)[:, None, :]
    bias_tiles = _rel_bias_tiles(rel_bias, tl["attn_tile"])

    xf = x.reshape(batch * seq, d)
    for l in range(depth):
        lam0 = jnp.full((1,), 0.8 - 0.6 * math.exp(-0.3 * l), F32)
        xf = _ffn(xf, nw, l, 0, 1, wi1, wo1, tm=tl["ffn_tm"])
        (qm, km, vmt, om, gc, gr, qd, kd, vdt, qh, lfh, kh, vh, gh) = _proj(
            xf, nw, l, w_proj, w_t, conv_w, conv_b, gate_b_col, gate_b_row, lb_all,
            batch=batch, tm=tl["proj_tm"], act_dtype=act_dtype)
        hm = _mlstm(qm, km, vmt, gc, gr, batch=batch, chunk=tl["mlstm_chunk"], nb=tl["mlstm_nb"])
        yd = _diff_attn(qd, kd, vdt, bias_tiles, lam0, lam_vecs, dnw_t, l, batch=batch, tile=tl["attn_tile"])
        oh = _hgrn(qh, kh, vh, lfh, batch=batch, tile=tl["hgrn_tile"], nb=tl["hgrn_nb"])
        xf = _mix_out(xf, hm, om, yd, oh, gh, mnw, hnw, w_o, nw, l, tm=tl["out_tm"])
        xf = _ffn(xf, nw, l, 4, 5, wi2, wo2, tm=tl["ffn_tm"])
    return xf.reshape(batch, seq, d)
```
